```python
import math, functools
import jax, jax.numpy as jnp
from jax import lax
import numpy as np

D_MODEL = 1024
BATCH = 16
SEQ = 2048
DEPTH = 1
DEC_BATCH = 128
DEC_SEQ = 8
PAST_LEN = 8192
PAGE_SIZE = 128

MIX_W = D_MODEL
POOL_W = MIX_W // 2
POOL_WINDOWS = (2, 4, 8, 16)
N_POOL_GROUPS = len(POOL_WINDOWS)
POOL_GROUP_W = POOL_W // N_POOL_GROUPS
POOL_HIST = max(POOL_WINDOWS) - 1
N_HEADS = 8
V_DIM = (MIX_W - POOL_W) // N_HEADS
NOPE_DIM = 64
ROPE_DIM = 32
Q_LORA = D_MODEL // 4
KV_LORA = D_MODEL // 4
IN_W = POOL_W + Q_LORA + KV_LORA + ROPE_DIM
D_FF = 4 * D_MODEL
N_MOD = 6
ROPE_BASE = 10000.0
EPS = 1e-6
Q_BLOCK = 128
SOFTMAX_SCALE = 1.0 / math.sqrt(NOPE_DIM + ROPE_DIM)
NEG_INF = -1e30

kernel_name = "hymba_pool_mla_adaln_decode_step"


def rmsnorm(x, g):
    xf = x.astype(jnp.float32)
    y = xf * lax.rsqrt(jnp.mean(xf * xf, axis=-1, keepdims=True) + EPS)
    return (y * g.astype(jnp.float32)).astype(x.dtype)


def modulation(c, w_mod, b_mod):
    m = jax.nn.silu(c) @ w_mod + b_mod
    return jnp.split(m[:, None, :], N_MOD, axis=-1)


def rope_tables(positions):
    inv = ROPE_BASE ** (-jnp.arange(0, ROPE_DIM, 2, dtype=jnp.float32) / ROPE_DIM)
    ang = positions[:, None] * inv[None, :]
    return jnp.cos(ang), jnp.sin(ang)


def apply_rope(x, cos, sin):
    xf = x.astype(jnp.float32)
    half = ROPE_DIM // 2
    x1, x2 = xf[..., :half], xf[..., half:]
    return jnp.concatenate([x1 * cos - x2 * sin, x2 * cos + x1 * sin], axis=-1).astype(x.dtype)


def causal_multiscale_pool(u_hist, u_new, positions, w_pool, pool_scale):
    B, S, _ = u_new.shape
    ext = jnp.concatenate([u_hist, u_new], axis=1)
    ef = ext.astype(jnp.float32)
    cs = jnp.concatenate([jnp.zeros_like(ef[:, :1]), jnp.cumsum(ef, axis=1)], axis=1)
    end = POOL_HIST + 1
    uf = u_new.astype(jnp.float32)
    outs = []
    for g, w in enumerate(POOL_WINDOWS):
        sl = slice(g * POOL_GROUP_W, (g + 1) * POOL_GROUP_W)
        wsum = cs[:, end:end + S, sl] - cs[:, end - w:end - w + S, sl]
        count = jnp.minimum(float(w), positions + 1.0)[:, None]
        outs.append(wsum / count - uf[..., sl])
    d = jnp.stack(outs, axis=2)
    y = jnp.einsum('bsgi,gio->bsgo', d, w_pool.astype(jnp.float32)).reshape(B, S, POOL_W)
    y = y * pool_scale.astype(jnp.float32)
    return y.astype(u_new.dtype), ext[:, -POOL_HIST:]


def mla_prompt_attention(q_nope, q_rope, lat, k_rope, w_uk, w_uv):
    B, S = q_nope.shape[:2]
    k_nope = jnp.einsum('btc,chd->bthd', lat, w_uk)
    v = jnp.einsum('btc,chd->bthd', lat, w_uv)
    nb = S // Q_BLOCK
    qn = q_nope.reshape(B, nb, Q_BLOCK, N_HEADS, NOPE_DIM).swapaxes(0, 1)
    qr = q_rope.reshape(B, nb, Q_BLOCK, N_HEADS, ROPE_DIM).swapaxes(0, 1)
    kpos = jnp.arange(S)

    def one_block(args):
        qn_b, qr_b, blk = args
        s = (jnp.einsum('bqhd,bkhd->bhqk', qn_b, k_nope)
             + jnp.einsum('bqhr,bkr->bhqk', qr_b, k_rope)).astype(jnp.float32) * SOFTMAX_SCALE
        qpos = blk * Q_BLOCK + jnp.arange(Q_BLOCK)
        s = jnp.where(qpos[:, None] >= kpos[None, :], s, NEG_INF)
        p = jax.nn.softmax(s, axis=-1).astype(v.dtype)
        return jnp.einsum('bhqk,bkhd->bqhd', p, v)

    o = lax.map(one_block, (qn, qr, jnp.arange(nb)))
    return o.swapaxes(0, 1).reshape(B, S, N_HEADS * V_DIM)


def mla_sample_attention(q_nope, q_rope, lat, k_rope, w_uk, w_uv, lat_past, kr_past):
    B, S = q_nope.shape[:2]
    T = lat_past.shape[1]
    q_lat = jnp.einsum('bshd,chd->bshc', q_nope, w_uk)
    s_past = (jnp.einsum('bshc,btc->bhst', q_lat, lat_past)
              + jnp.einsum('bshr,btr->bhst', q_rope, kr_past)).astype(jnp.float32) * SOFTMAX_SCALE
    s_new = (jnp.einsum('bshc,btc->bhst', q_lat, lat)
             + jnp.einsum('bshr,btr->bhst', q_rope, k_rope)).astype(jnp.float32) * SOFTMAX_SCALE
    causal = jnp.tril(jnp.ones((S, S), dtype=bool))
    s_new = jnp.where(causal, s_new, NEG_INF)
    p = jax.nn.softmax(jnp.concatenate([s_past, s_new], axis=-1), axis=-1).astype(lat.dtype)
    o_lat = (jnp.einsum('bhst,btc->bshc', p[..., :T], lat_past)
             + jnp.einsum('bhst,btc->bshc', p[..., T:], lat))
    o = jnp.einsum('bshc,chd->bshd', o_lat, w_uv)
    return o.reshape(B, S, N_HEADS * V_DIM)


def decoder_layer(x, c, positions, u_hist, attend, w_mod, b_mod, g_mix, w_in, g_q, w_uq,
                  g_kv, w_uk, w_uv, w_pool, pool_scale, w_out, g_mlp, w_up, w_down):
    shift1, scale1, gate1, shift2, scale2, gate2 = modulation(c, w_mod, b_mod)
    h = rmsnorm(x, g_mix) * (1.0 + scale1) + shift1
    proj = h @ w_in
    o1 = POOL_W
    o2 = o1 + Q_LORA
    o3 = o2 + KV_LORA
    u = proj[..., :o1]
    cq = proj[..., o1:o2]
    ckv = proj[..., o2:o3]
    kr = proj[..., o3:]
    cos, sin = rope_tables(positions)
    q = jnp.einsum('bsl,lhd->bshd', rmsnorm(cq, g_q), w_uq)
    q_nope = q[..., :NOPE_DIM]
    q_rope = apply_rope(q[..., NOPE_DIM:], cos[:, None, :], sin[:, None, :])
    lat = rmsnorm(ckv, g_kv)
    k_rope = apply_rope(kr, cos, sin)
    pool_out, new_hist = causal_multiscale_pool(u_hist, u, positions, w_pool, pool_scale)
    attn_out = attend(q_nope, q_rope, lat, k_rope, w_uk, w_uv)
    x = x + gate1 * (jnp.concatenate([pool_out, attn_out], axis=-1) @ w_out)
    h2 = rmsnorm(x, g_mlp) * (1.0 + scale2) + shift2
    x = x + gate2 * (jnp.square(jax.nn.relu(h2 @ w_up)) @ w_down)
    return x, lat, k_rope, new_hist


def setup_inputs(seed: int = 0) -> dict:
    key = jax.random.key(seed)
    ks = jax.random.split(key, 32)
    f32 = jnp.float32
    n_pages = PAST_LEN // PAGE_SIZE
    n_used = DEC_BATCH * n_pages
    n_phys = n_used + n_used // 4
    nrm = lambda k, shape, s: jax.random.normal(k, shape, f32) * s
    page_table = jax.random.permutation(ks[0], n_phys)[:n_used].reshape(DEC_BATCH, n_pages).astype(jnp.int32)
    return {
        "x_prompt": nrm(ks[1], (BATCH, SEQ, D_MODEL), 1.0),
        "x_sample": nrm(ks[2], (DEC_BATCH, DEC_SEQ, D_MODEL), 1.0),
        "cache_latent": nrm(ks[3], (DEPTH, n_phys, PAGE_SIZE, KV_LORA), 1.0),
        "cache_krope": nrm(ks[4], (DEPTH, n_phys, PAGE_SIZE, ROPE_DIM), 1.0),
        "state_pool": nrm(ks[5], (DEPTH, DEC_BATCH, POOL_HIST, POOL_W), 1.0),
        "page_table": page_table,
        "c_prompt": nrm(ks[6], (BATCH, D_MODEL), 1.0),
        "c_sample": nrm(ks[7], (DEC_BATCH, D_MODEL), 1.0),
        "w_mod": nrm(ks[8], (DEPTH, D_MODEL, N_MOD * D_MODEL), 0.5 * D_MODEL ** -0.5),
        "b_mod": nrm(ks[9], (DEPTH, N_MOD * D_MODEL), 0.02),
        "g_mix": 1.0 + nrm(ks[10], (DEPTH, D_MODEL), 0.05),
        "w_in": nrm(ks[11], (DEPTH, D_MODEL, IN_W), D_MODEL ** -0.5),
        "g_q": 1.0 + nrm(ks[12], (DEPTH, Q_LORA), 0.05),
        "w_uq": nrm(ks[13], (DEPTH, Q_LORA, N_HEADS, NOPE_DIM + ROPE_DIM), Q_LORA ** -0.5),
        "g_kv": 1.0 + nrm(ks[14], (DEPTH, KV_LORA), 0.05),
        "w_uk": nrm(ks[15], (DEPTH, KV_LORA, N_HEADS, NOPE_DIM), KV_LORA ** -0.5),
        "w_uv": nrm(ks[16], (DEPTH, KV_LORA, N_HEADS, V_DIM), KV_LORA ** -0.5),
        "w_pool": nrm(ks[17], (DEPTH, N_POOL_GROUPS, POOL_GROUP_W, POOL_GROUP_W), POOL_GROUP_W ** -0.5),
        "pool_scale": 1.0 + nrm(ks[18], (DEPTH, POOL_W), 0.1),
        "w_out": nrm(ks[19], (DEPTH, MIX_W, D_MODEL), MIX_W ** -0.5),
        "g_mlp": 1.0 + nrm(ks[20], (DEPTH, D_MODEL), 0.05),
        "w_up": nrm(ks[21], (DEPTH, D_MODEL, D_FF), D_MODEL ** -0.5),
        "w_down": nrm(ks[22], (DEPTH, D_FF, D_MODEL), D_FF ** -0.5),
        "g_final": 1.0 + nrm(ks[23], (D_MODEL,), 0.05),
    }


def reference(x_prompt, x_sample, cache_latent, cache_krope, state_pool, page_table, c_prompt, c_sample,
              w_mod, b_mod, g_mix, w_in, g_q, w_uq, g_kv, w_uk, w_uv, w_pool, pool_scale, w_out,
              g_mlp, w_up, w_down, g_final):
    B, S_p, _ = x_prompt.shape
    DB, S_s, _ = x_sample.shape
    past = page_table.shape[1] * cache_latent.shape[2]
    pos_p = jnp.arange(S_p, dtype=jnp.float32)
    pos_s = jnp.arange(S_s, dtype=jnp.float32) + float(past)
    hist_p = jnp.zeros((B, POOL_HIST, POOL_W), x_prompt.dtype)
    xp, xs = x_prompt, x_sample
    lat_p, kr_p, pool_p, lat_s, kr_s, pool_s = [], [], [], [], [], []
    for l in range(DEPTH):
        params = (w_mod[l], b_mod[l], g_mix[l], w_in[l], g_q[l], w_uq[l], g_kv[l], w_uk[l], w_uv[l],
                  w_pool[l], pool_scale[l], w_out[l], g_mlp[l], w_up[l], w_down[l])
        xp, lat, kr, hist = decoder_layer(xp, c_prompt, pos_p, hist_p, mla_prompt_attention, *params)
        lat_p.append(lat)
        kr_p.append(kr)
        pool_p.append(hist)
        lat_past = cache_latent[l][page_table].reshape(DB, past, KV_LORA)
        kr_past = cache_krope[l][page_table].reshape(DB, past, ROPE_DIM)
        attend_s = functools.partial(mla_sample_attention, lat_past=lat_past, kr_past=kr_past)
        xs, lat, kr, hist = decoder_layer(xs, c_sample, pos_s, state_pool[l], attend_s, *params)
        lat_s.append(lat)
        kr_s.append(kr)
        pool_s.append(hist)
    y_prompt = rmsnorm(xp, g_final)
    y_sample = rmsnorm(xs, g_final)
    return (y_prompt, y_sample, jnp.stack(lat_p), jnp.stack(kr_p), jnp.stack(pool_p),
            jnp.stack(lat_s), jnp.stack(kr_s), jnp.stack(pool_s))
```

```python
import functools
import math

import jax
import jax.numpy as jnp
from jax import lax
from jax.experimental import pallas as pl
from jax.experimental.pallas import tpu as pltpu

F32 = jnp.float32
BF16 = jnp.bfloat16

N_HEADS = 8
NOPE_DIM = 64
ROPE_DIM = 32
V_DIM = 64
POOL_WINDOWS = (2, 4, 8, 16)
POOL_HIST = max(POOL_WINDOWS) - 1
HIST_PAD = POOL_HIST + 1
N_MOD = 6
ROPE_BASE = 10000.0
EPS = 1e-6
NEG_INF = -1e30
SOFTMAX_SCALE = 1.0 / math.sqrt(NOPE_DIM + ROPE_DIM)
LOG2E = math.log2(math.e)

LANES = 128
SUBLANES = 8
VMEM_LIMIT = 56 * 1024 * 1024

HEAD_PAD = LANES


def _params(sem, vmem=VMEM_LIMIT):
    return pltpu.CompilerParams(dimension_semantics=sem, vmem_limit_bytes=vmem)


def _nt_dot(a, b):
    return lax.dot_general(a, b, (((1,), (1,)), ((), ())), preferred_element_type=F32)


def _dot(a, b):
    return jnp.dot(a, b, preferred_element_type=F32)


def _rms(x, g):
    return x * lax.rsqrt(jnp.mean(x * x, axis=-1, keepdims=True) + EPS) * g


def _mod_kernel(c_ref, w_ref, b_ref, o_ref):
    c = c_ref[...]
    sc = c * (1.0 / (1.0 + jnp.exp(-c)))
    o_ref[...] = _dot(sc.astype(BF16), w_ref[...].astype(BF16)) + b_ref[...]


def _modulation(c, w_mod, b_mod):
    n, d = c.shape
    nout = w_mod.shape[1]
    tn = d
    return pl.pallas_call(
        _mod_kernel,
        grid=(nout // tn,),
        in_specs=[
            pl.BlockSpec((n, d), lambda j: (0, 0)),
            pl.BlockSpec((d, tn), lambda j: (0, j)),
            pl.BlockSpec((1, tn), lambda j: (0, j)),
        ],
        out_specs=pl.BlockSpec((n, tn), lambda j: (0, j)),
        out_shape=jax.ShapeDtypeStruct((n, nout), F32),
        compiler_params=_params(("arbitrary",)),
        name="mod",
    )(c, w_mod, b_mod.reshape(1, nout))


def _pre_kernel(*refs, pos0, absorb, pool_w, q_lora, kv_lora):
    (x_ref, mod_ref, hist_ref, tab_ref, win_ref, wq_ref, wkv_ref, wpool_ref,
     gmix_ref, gq_ref, gkv_ref, pscale_ref) = refs[:12]
    if absorb:
        (q_ref, qlat_ref, lat_ref, kr_ref, pool_ref, nh_ref, ext_ref) = refs[12:]
    else:
        (q_ref, kcat_ref, v_ref, lat_ref, kr_ref, pool_ref, nh_ref, ext_ref) = refs[12:]
    si = pl.program_id(1)
    tb, ts, d = x_ref.shape
    r = tb * ts
    hq = N_HEADS * HEAD_PAD

    x = x_ref[...]
    shift1 = mod_ref[:, 0:1, :]
    scale1 = mod_ref[:, 1:2, :]
    h = _rms(x, gmix_ref[...]) * (1.0 + scale1) + shift1
    proj = _dot(h.reshape(r, d).astype(BF16), win_ref[...])
    o1 = pool_w
    o2 = o1 + q_lora
    o3 = o2 + kv_lora
    u = proj[:, :o1].reshape(tb, ts, pool_w)
    cq = proj[:, o1:o2]
    ckv = proj[:, o2:o3]
    kr_a = proj[:, o3:o3 + LANES].reshape(tb, ts, LANES)
    kr_b = proj[:, o3 + LANES:o3 + 2 * LANES].reshape(tb, ts, LANES)

    tab = tab_ref[...]
    t_qa = tab[:, 0 * LANES:1 * LANES]
    t_qb = tab[:, 1 * LANES:2 * LANES]
    t_ka = tab[:, 2 * LANES:3 * LANES]
    t_kb = tab[:, 3 * LANES:4 * LANES]

    qn = _rms(cq, gq_ref[...]).astype(BF16)
    if absorb:
        qab = _dot(qn, wq_ref[...])
        for hd in range(N_HEADS):
            lo = hd * HEAD_PAD
            qa = qab[:, lo:lo + HEAD_PAD].reshape(tb, ts, HEAD_PAD)
            qb = qab[:, hq + lo:hq + lo + HEAD_PAD].reshape(tb, ts, HEAD_PAD)
            qh = (qa * t_qa + qb * t_qb).astype(BF16)
            q_ref[:, :, lo:lo + HEAD_PAD] = qh
            ql = _dot(qh.reshape(r, HEAD_PAD), wkv_ref[hd])
            qlat_ref[:, :, hd * kv_lora:(hd + 1) * kv_lora] = ql.reshape(tb, ts, kv_lora).astype(BF16)
    else:
        qab = _dot(qn, wq_ref[...])
        for hd in range(N_HEADS):
            lo = hd * HEAD_PAD
            qa = qab[:, lo:lo + HEAD_PAD].reshape(tb, ts, HEAD_PAD)
            qb = qab[:, hq + lo:hq + lo + HEAD_PAD].reshape(tb, ts, HEAD_PAD)
            q_ref[:, :, lo:lo + HEAD_PAD] = (qa * t_qa + qb * t_qb).astype(BF16)

    lat = _rms(ckv, gkv_ref[...])
    lat_ref[...] = lat.reshape(tb, ts, kv_lora)
    kr128 = kr_a * t_ka + kr_b * t_kb
    kr_ref[...] = kr128[:, :, :ROPE_DIM]
    if not absorb:
        kv = _dot(lat.astype(BF16), wkv_ref[...])
        for hd in range(N_HEADS):
            lo = hd * HEAD_PAD
            kcat_ref[:, :, lo:lo + HEAD_PAD] = (kv[:, lo:lo + HEAD_PAD].reshape(tb, ts, HEAD_PAD) + kr128).astype(BF16)
        v_ref[...] = kv[:, hq:].reshape(tb, ts, N_HEADS * V_DIM).astype(BF16)

    @pl.when(si == 0)
    def _():
        ext_ref[:, 0:HIST_PAD, :] = hist_ref[...]

    ext_ref[:, HIST_PAD:HIST_PAD + ts, :] = u
    gw = pool_w // len(POOL_WINDOWS)
    pos = (pos0 + (si * ts).astype(F32)
           + lax.broadcasted_iota(jnp.int32, (1, ts, gw), 1).astype(F32))
    parts = []
    for g, w in enumerate(POOL_WINDOWS):
        lo = g * gw
        acc = ext_ref[:, HIST_PAD:HIST_PAD + ts, lo:lo + gw]
        for i in range(1, w):
            acc = acc + ext_ref[:, HIST_PAD - i:HIST_PAD - i + ts, lo:lo + gw]
        inv = 1.0 / jnp.minimum(float(w), pos + 1.0)
        parts.append(acc * inv - u[:, :, lo:lo + gw])
    dpool = jnp.concatenate(parts, axis=-1).reshape(r, pool_w).astype(BF16)
    y = _dot(dpool, wpool_ref[...]) * pscale_ref[...]
    pool_ref[...] = y.reshape(tb, ts, pool_w).astype(BF16)
    tail = ext_ref[:, ts:ts + HIST_PAD, :]
    nh_ref[...] = tail
    ext_ref[:, 0:HIST_PAD, :] = tail


def _pre(x, mod, hist16, tab, w, *, pos0, absorb, tb, ts):
    b, s, d = x.shape
    pool_w = hist16.shape[-1]
    q_lora = w["gq"].shape[-1]
    kv_lora = w["gkv"].shape[-1]
    hq = N_HEADS * HEAD_PAD
    grid = (b // tb, s // ts)
    tile = lambda width: pl.BlockSpec((tb, ts, width), lambda i, j: (i, j, 0))
    full = lambda a: pl.BlockSpec(a.shape, lambda i, j: (0,) * a.ndim)
    wkv = w["wabs"] if absorb else w["wkv"]
    in_specs = [
        tile(d),
        pl.BlockSpec((tb, N_MOD, d), lambda i, j: (i, 0, 0)),
        pl.BlockSpec((tb, HIST_PAD, pool_w), lambda i, j: (i, 0, 0)),
        pl.BlockSpec((ts, 4 * LANES), lambda i, j: (j, 0)),
        full(w["win"]), full(w["wq"]), full(wkv), full(w["wpool"]),
        full(w["gmix"]), full(w["gq"]), full(w["gkv"]), full(w["pscale"]),
    ]
    sds = jax.ShapeDtypeStruct
    if absorb:
        out_shape = [sds((b, s, hq), BF16), sds((b, s, N_HEADS * kv_lora), BF16)]
        out_specs = [tile(hq), tile(N_HEADS * kv_lora)]
    else:
        out_shape = [sds((b, s, hq), BF16), sds((b, s, hq), BF16), sds((b, s, N_HEADS * V_DIM), BF16)]
        out_specs = [tile(hq), tile(hq), tile(N_HEADS * V_DIM)]
    out_shape += [sds((b, s, kv_lora), F32), sds((b, s, ROPE_DIM), F32), sds((b, s, pool_w), BF16),
                  sds((b, HIST_PAD, pool_w), F32)]
    out_specs += [tile(kv_lora), tile(ROPE_DIM), tile(pool_w),
                  pl.BlockSpec((tb, HIST_PAD, pool_w), lambda i, j: (i, 0, 0))]
    kern = functools.partial(_pre_kernel, pos0=float(pos0), absorb=absorb, pool_w=pool_w,
                             q_lora=q_lora, kv_lora=kv_lora)
    return pl.pallas_call(
        kern,
        grid=grid,
        in_specs=in_specs,
        out_specs=out_specs,
        out_shape=out_shape,
        scratch_shapes=[pltpu.VMEM((tb, HIST_PAD + ts, pool_w), F32)],
        compiler_params=_params(("arbitrary", "arbitrary")),
        name="pre_sample" if absorb else "pre_prompt",
    )(x, mod, hist16, tab, w["win"], w["wq"], wkv, w["wpool"], w["gmix"], w["gq"], w["gkv"], w["pscale"])


def _attn_kernel(q_ref, k_ref, v_ref, o_ref, *, tq):
    s = q_ref.shape[1]
    row = lax.broadcasted_iota(jnp.int32, (tq, tq), 0)
    col = lax.broadcasted_iota(jnp.int32, (tq, tq), 1)
    causal = row >= col
    first_head = lax.broadcasted_iota(jnp.int32, (tq, 2 * V_DIM), 1) < V_DIM
    for qi in range(s // tq):
        q0 = qi * tq
        outs = []
        for hh in range(2):
            hs = slice(hh * HEAD_PAD, (hh + 1) * HEAD_PAD)
            q = q_ref[0, q0:q0 + tq, hs]
            s_d = jnp.where(causal, _nt_dot(q, k_ref[0, q0:q0 + tq, hs]), NEG_INF)
            m = jnp.max(s_d, axis=-1, keepdims=True)
            if qi > 0:
                s_o = _nt_dot(q, k_ref[0, 0:q0, hs])
                m = jnp.maximum(m, jnp.max(s_o, axis=-1, keepdims=True))
                p_o = jnp.exp2(s_o - m)
                l = jnp.sum(p_o, axis=-1, keepdims=True)
                acc = _dot(p_o.astype(BF16), v_ref[0, 0:q0, :])
            p_d = jnp.exp2(s_d - m)
            if qi > 0:
                l = l + jnp.sum(p_d, axis=-1, keepdims=True)
                acc = acc + _dot(p_d.astype(BF16), v_ref[0, q0:q0 + tq, :])
            else:
                l = jnp.sum(p_d, axis=-1, keepdims=True)
                acc = _dot(p_d.astype(BF16), v_ref[0, q0:q0 + tq, :])
            outs.append(acc * (1.0 / l))
        o_ref[0, q0:q0 + tq, :] = jnp.where(first_head, outs[0], outs[1]).astype(BF16)


def _attention(q, kcat, v, *, tq):
    b, s, _ = q.shape
    pairs = N_HEADS // 2
    return pl.pallas_call(
        functools.partial(_attn_kernel, tq=tq),
        grid=(b, pairs),
        in_specs=[
            pl.BlockSpec((1, s, 2 * HEAD_PAD), lambda i, p: (i, 0, p)),
            pl.BlockSpec((1, s, 2 * HEAD_PAD), lambda i, p: (i, 0, p)),
            pl.BlockSpec((1, s, 2 * V_DIM), lambda i, p: (i, 0, p)),
        ],
        out_specs=pl.BlockSpec((1, s, 2 * V_DIM), lambda i, p: (i, 0, p)),
        out_shape=jax.ShapeDtypeStruct((b, s, N_HEADS * V_DIM), BF16),
        compiler_params=_params(("arbitrary", "arbitrary")),
        name="attn_prompt",
    )(q, kcat, v)


def _dec_kernel(pt_ref, qlat_ref, qpad_ref, latn_ref, krn_ref, clat_hbm, ckr_hbm, o_ref,
                latbuf, krbuf, sem, *, n_pages, page, chunk, s_new):
    b = pl.program_id(0)
    nb = pl.num_programs(0)
    slot = b % 2
    t_past = n_pages * page
    rows = qlat_ref.shape[1]

    def page_copies(bb, sl, p):
        pg = pt_ref[bb, p]
        dst = pl.ds(pl.multiple_of(p * page, page), page)
        return (pltpu.make_async_copy(clat_hbm.at[pg], latbuf.at[sl, dst], sem.at[0, sl]),
                pltpu.make_async_copy(ckr_hbm.at[pg], krbuf.at[sl, dst], sem.at[1, sl]))

    def start_all(bb, sl):
        def body(p, carry):
            c_lat, c_kr = page_copies(bb, sl, p)
            c_lat.start()
            c_kr.start()
            return carry
        lax.fori_loop(0, n_pages, body, 0)

    def wait_all(bb, sl):
        def body(p, carry):
            c_lat, c_kr = page_copies(bb, sl, p)
            c_lat.wait()
            c_kr.wait()
            return carry
        lax.fori_loop(0, n_pages, body, 0)

    @pl.when(b == 0)
    def _():
        start_all(0, 0)

    @pl.when(b + 1 < nb)
    def _():
        start_all(b + 1, 1 - slot)

    latbuf[slot, t_past:t_past + page, :] = jnp.zeros((page, latbuf.shape[-1]), F32)
    krbuf[slot, t_past:t_past + page, :] = jnp.zeros((page, ROPE_DIM), F32)
    latbuf[slot, t_past:t_past + s_new, :] = latn_ref[0]
    krbuf[slot, t_past:t_past + s_new, :] = krn_ref[0]

    q1 = qlat_ref[0]
    q2 = qpad_ref[0][:, :ROPE_DIM]

    def scores(lo, n):
        lat_c = latbuf[slot, lo:lo + n, :].astype(BF16)
        kr_c = krbuf[slot, lo:lo + n, :].astype(BF16)
        return _nt_dot(q1, lat_c) + _nt_dot(q2, kr_c), lat_c

    s_n, lat_n = scores(t_past, page)
    tok = lax.broadcasted_iota(jnp.int32, (rows, page), 0) // N_HEADS
    col = lax.broadcasted_iota(jnp.int32, (rows, page), 1)
    s_n = jnp.where(col <= tok, s_n, NEG_INF)
    m = jnp.max(s_n, axis=-1, keepdims=True)
    p_n = jnp.exp2(s_n - m)
    l = jnp.sum(p_n, axis=-1, keepdims=True)
    acc = _dot(p_n.astype(BF16), lat_n)

    wait_all(b, slot)

    for c in range(t_past // chunk):
        s_c, lat_c = scores(c * chunk, chunk)
        m_new = jnp.maximum(m, jnp.max(s_c, axis=-1, keepdims=True))
        alpha = jnp.exp2(m - m_new)
        p_c = jnp.exp2(s_c - m_new)
        l = alpha * l + jnp.sum(p_c, axis=-1, keepdims=True)
        acc = alpha * acc + _dot(p_c.astype(BF16), lat_c)
        m = m_new
    o_ref[0] = (acc * (1.0 / l)).astype(BF16)


def _decode_attention(page_table, qlat, qpad, lat_new, kr_new, cache_lat, cache_kr):
    db, rows, kv_lora = qlat.shape
    n_pages = page_table.shape[1]
    page = cache_lat.shape[1]
    s_new = lat_new.shape[1]
    t_past = n_pages * page
    chunk = min(2048, t_past)
    kern = functools.partial(_dec_kernel, n_pages=n_pages, page=page, chunk=chunk, s_new=s_new)
    grid_spec = pltpu.PrefetchScalarGridSpec(
        num_scalar_prefetch=1,
        grid=(db,),
        in_specs=[
            pl.BlockSpec((1, rows, kv_lora), lambda i, pt: (i, 0, 0)),
            pl.BlockSpec((1, rows, HEAD_PAD), lambda i, pt: (i, 0, 0)),
            pl.BlockSpec((1, s_new, kv_lora), lambda i, pt: (i, 0, 0)),
            pl.BlockSpec((1, s_new, ROPE_DIM), lambda i, pt: (i, 0, 0)),
            pl.BlockSpec(memory_space=pl.ANY),
            pl.BlockSpec(memory_space=pl.ANY),
        ],
        out_specs=pl.BlockSpec((1, rows, kv_lora), lambda i, pt: (i, 0, 0)),
        scratch_shapes=[
            pltpu.VMEM((2, t_past + page, kv_lora), F32),
            pltpu.VMEM((2, t_past + page, ROPE_DIM), F32),
            pltpu.SemaphoreType.DMA((2, 2)),
        ],
    )
    return pl.pallas_call(
        kern,
        grid_spec=grid_spec,
        out_shape=jax.ShapeDtypeStruct((db, rows, kv_lora), BF16),
        compiler_params=_params(("arbitrary",)),
        name="attn_sample",
    )(page_table, qlat, qpad, lat_new, kr_new, cache_lat, cache_kr)


def _post_kernel(x_ref, mod_ref, pool_ref, att_ref, wout_ref, wuv_ref, wup_ref, wdown_ref,
                 gmlp_ref, gfin_ref, y_ref, *, absorb, final, pool_w, kv_lora, ff_chunk):
    tb, ts, d = x_ref.shape
    r = tb * ts
    x = x_ref[...]
    gate1 = mod_ref[:, 2:3, :]
    shift2 = mod_ref[:, 3:4, :]
    scale2 = mod_ref[:, 4:5, :]
    gate2 = mod_ref[:, 5:6, :]

    mix = _dot(pool_ref[...].reshape(r, pool_w), wout_ref[0:pool_w, :])
    if absorb:
        for p in range(N_HEADS // 2):
            o_pair = None
            for hh in range(2):
                hd = 2 * p + hh
                o_lat = att_ref[:, :, hd * kv_lora:(hd + 1) * kv_lora].reshape(r, kv_lora)
                t = _dot(o_lat, wuv_ref[hd])
                o_pair = t if o_pair is None else o_pair + t
            lo = pool_w + p * 2 * V_DIM
            mix = mix + _dot(o_pair.astype(BF16), wout_ref[lo:lo + 2 * V_DIM, :])
    else:
        mix = mix + _dot(att_ref[...].reshape(r, N_HEADS * V_DIM), wout_ref[pool_w:, :])
    x1 = x + gate1 * mix.reshape(tb, ts, d)

    h2 = (_rms(x1, gmlp_ref[...]) * (1.0 + scale2) + shift2).reshape(r, d).astype(BF16)
    d_ff = wup_ref.shape[1]
    mlp = None
    for c in range(d_ff // ff_chunk):
        cs = slice(c * ff_chunk, (c + 1) * ff_chunk)
        hid = jnp.maximum(_dot(h2, wup_ref[:, cs]), 0.0)
        t = _dot((hid * hid).astype(BF16), wdown_ref[cs, :])
        mlp = t if mlp is None else mlp + t
    x2 = x1 + gate2 * mlp.reshape(tb, ts, d)
    y_ref[...] = _rms(x2, gfin_ref[...]) if final else x2


def _post(x, mod, pool, att, w, *, absorb, final, tb, ts):
    b, s, d = x.shape
    pool_w = pool.shape[-1]
    kv_lora = w["gkv"].shape[-1]
    tile = lambda width: pl.BlockSpec((tb, ts, width), lambda i, j: (i, j, 0))
    const = lambda a: pl.BlockSpec(a.shape, lambda i, j: (0,) * a.ndim, pipeline_mode=pl.Buffered(1))
    kern = functools.partial(_post_kernel, absorb=absorb, final=final, pool_w=pool_w, kv_lora=kv_lora,
                             ff_chunk=min(1024, w["wup"].shape[1]))
    return pl.pallas_call(
        kern,
        grid=(b // tb, s // ts),
        in_specs=[
            tile(d),
            pl.BlockSpec((tb, N_MOD, d), lambda i, j: (i, 0, 0)),
            tile(pool_w),
            tile(att.shape[-1]),
            const(w["wout"]), const(w["wuvp"]), const(w["wup"]), const(w["wdown"]),
            const(w["gmlp"]), const(w["gfin"]),
        ],
        out_specs=tile(d),
        out_shape=jax.ShapeDtypeStruct((b, s, d), F32),
        compiler_params=_params(("arbitrary", "arbitrary")),
        name="post_sample" if absorb else "post_prompt",
    )(x, mod, pool, att, w["wout"], w["wuvp"], w["wup"], w["wdown"], w["gmlp"], w["gfin"])


def _rot_half(wr):
    half = ROPE_DIM // 2
    return jnp.concatenate([-wr[..., half:], wr[..., :half]], axis=-1)


def _layer_weights(w_in, g_mix, g_q, w_uq, g_kv, w_uk, w_uv, w_pool, pool_scale, w_out, g_mlp,
                   w_up, w_down, g_final):
    d, _ = w_in.shape
    q_lora = g_q.shape[-1]
    kv_lora = g_kv.shape[-1]
    n_groups, gw, _ = w_pool.shape
    pool_w = n_groups * gw
    o3 = pool_w + q_lora + kv_lora
    zpad = lambda rows, cols: jnp.zeros((rows, cols), F32)

    w_kr = w_in[:, o3:]
    win = jnp.concatenate([w_in[:, :o3], w_kr, zpad(d, LANES - ROPE_DIM),
                           _rot_half(w_kr), zpad(d, LANES - ROPE_DIM)], axis=1)

    q_rope = w_uq[:, :, NOPE_DIM:]
    q_nope = w_uq[:, :, :NOPE_DIM]
    hz = jnp.zeros((q_lora, N_HEADS, HEAD_PAD - ROPE_DIM - NOPE_DIM), F32)
    wq_a = jnp.concatenate([q_rope, q_nope, hz], axis=-1).reshape(q_lora, N_HEADS * HEAD_PAD)
    wq_b = jnp.concatenate([_rot_half(q_rope), jnp.zeros((q_lora, N_HEADS, HEAD_PAD - ROPE_DIM), F32)],
                           axis=-1).reshape(q_lora, N_HEADS * HEAD_PAD)
    wq = jnp.concatenate([wq_a, wq_b], axis=1)

    kz_lo = jnp.zeros((kv_lora, N_HEADS, ROPE_DIM), F32)
    kz_hi = jnp.zeros((kv_lora, N_HEADS, HEAD_PAD - ROPE_DIM - NOPE_DIM), F32)
    wk = jnp.concatenate([kz_lo, w_uk, kz_hi], axis=-1).reshape(kv_lora, N_HEADS * HEAD_PAD)
    wkv = jnp.concatenate([wk, w_uv.reshape(kv_lora, N_HEADS * V_DIM)], axis=1)
    wabs = jnp.transpose(jnp.concatenate([kz_lo, w_uk, kz_hi], axis=-1), (1, 2, 0))

    uv = jnp.transpose(w_uv, (1, 0, 2))
    uz = jnp.zeros_like(uv)
    even = (jnp.arange(N_HEADS) % 2 == 0)[:, None, None]
    wuvp = jnp.where(even, jnp.concatenate([uv, uz], axis=-1), jnp.concatenate([uz, uv], axis=-1))

    wpool = jax.scipy.linalg.block_diag(*[w_pool[g] for g in range(n_groups)])

    bf = lambda a: a.astype(BF16)
    return dict(
        win=bf(win), wq=bf(wq), wkv=bf(wkv), wabs=bf(wabs), wuvp=bf(wuvp), wpool=bf(wpool),
        wout=bf(w_out), wup=bf(w_up), wdown=bf(w_down),
        gmix=g_mix.reshape(1, -1), gq=g_q.reshape(1, -1), gkv=g_kv.reshape(1, -1),
        pscale=pool_scale.reshape(1, -1), gmlp=g_mlp.reshape(1, -1), gfin=g_final.reshape(1, -1),
    )


def _rope_tables(positions):
    inv = ROPE_BASE ** (-jnp.arange(0, ROPE_DIM, 2, dtype=F32) / ROPE_DIM)
    ang = positions[:, None] * inv[None, :]
    cos2 = jnp.tile(jnp.cos(ang), (1, 2))
    sin2 = jnp.tile(jnp.sin(ang), (1, 2))
    n = positions.shape[0]
    qs = SOFTMAX_SCALE * LOG2E
    t_qa = jnp.concatenate([cos2, jnp.ones((n, NOPE_DIM), F32),
                            jnp.zeros((n, HEAD_PAD - ROPE_DIM - NOPE_DIM), F32)], axis=1) * qs
    zr = jnp.zeros((n, HEAD_PAD - ROPE_DIM), F32)
    t_qb = jnp.concatenate([sin2, zr], axis=1) * qs
    t_ka = jnp.concatenate([cos2, zr], axis=1)
    t_kb = jnp.concatenate([sin2, zr], axis=1)
    return jnp.concatenate([t_qa, t_qb, t_ka, t_kb], axis=1)


def _seq_tile(s, target):
    t = min(s, target)
    while s % t:
        t //= 2
    return t


def kernel(x_prompt, x_sample, cache_latent, cache_krope, state_pool, page_table, c_prompt, c_sample,
           w_mod, b_mod, g_mix, w_in, g_q, w_uq, g_kv, w_uk, w_uv, w_pool, pool_scale, w_out,
           g_mlp, w_up, w_down, g_final):
    bp, sp, d = x_prompt.shape
    db, ss, _ = x_sample.shape
    depth = w_mod.shape[0]
    n_pages = page_table.shape[1]
    page = cache_latent.shape[2]
    past = n_pages * page
    pool_w = state_pool.shape[-1]
    kv_lora = g_kv.shape[-1]

    tab_p = _rope_tables(jnp.arange(sp, dtype=F32))
    tab_s = _rope_tables(jnp.arange(ss, dtype=F32) + float(past))
    ts_p = _seq_tile(sp, 512)
    tb_s = _seq_tile(db, max(1, 256 // ss))

    xp, xs = x_prompt, x_sample
    outs = [[] for _ in range(6)]
    for l in range(depth):
        w = _layer_weights(w_in[l], g_mix[l], g_q[l], w_uq[l], g_kv[l], w_uk[l], w_uv[l], w_pool[l],
                           pool_scale[l], w_out[l], g_mlp[l], w_up[l], w_down[l], g_final)
        mod = _modulation(jnp.concatenate([c_prompt, c_sample], axis=0), w_mod[l], b_mod[l])
        mod = mod.reshape(bp + db, N_MOD, d)
        mod_p, mod_s = mod[:bp], mod[bp:]

        hist_p = jnp.zeros((bp, HIST_PAD, pool_w), F32)
        q, kcat, v, lat_p, kr_p, pool_p, nh_p = _pre(
            xp, mod_p, hist_p, tab_p, w, pos0=0.0, absorb=False, tb=1, ts=ts_p)
        att_p = _attention(q, kcat, v, tq=_seq_tile(sp, 512))
        final = l == depth - 1
        xp = _post(xp, mod_p, pool_p, att_p, w, absorb=False, final=final, tb=1, ts=ts_p)

        hist_s = jnp.pad(state_pool[l], ((0, 0), (HIST_PAD - POOL_HIST, 0), (0, 0)))
        qs, qlat, lat_s, kr_s, pool_s, nh_s = _pre(
            xs, mod_s, hist_s, tab_s, w, pos0=float(past), absorb=True, tb=tb_s, ts=ss)
        rows = ss * N_HEADS
        o_lat = _decode_attention(
            page_table,
            qlat.reshape(db, rows, kv_lora),
            qs.reshape(db, rows, HEAD_PAD),
            lat_s, kr_s,
            cache_latent[l], cache_krope[l])
        xs = _post(xs, mod_s, pool_s, o_lat.reshape(db, ss, N_HEADS * kv_lora), w,
                   absorb=True, final=final, tb=tb_s, ts=ss)

        for lst, val in zip(outs, (lat_p, kr_p, nh_p[:, HIST_PAD - POOL_HIST:], lat_s, kr_s,
                                   nh_s[:, HIST_PAD - POOL_HIST:])):
            lst.append(val)
    return (xp, xs) + tuple(jnp.stack(o) for o in outs)
```

```python
import functools
import math

import jax
import jax.numpy as jnp
from jax import lax
from jax.experimental import pallas as pl
from jax.experimental.pallas import tpu as pltpu

F32 = jnp.float32
BF16 = jnp.bfloat16

N_HEADS = 8
NOPE_DIM = 64
ROPE_DIM = 32
V_DIM = 64
POOL_WINDOWS = (2, 4, 8, 16)
POOL_HIST = max(POOL_WINDOWS) - 1
HIST_PAD = POOL_HIST + 1
N_MOD = 6
ROPE_BASE = 10000.0
EPS = 1e-6
NEG_INF = -1e30
SOFTMAX_SCALE = 1.0 / math.sqrt(NOPE_DIM + ROPE_DIM)
LOG2E = math.log2(math.e)

LANES = 128
SUBLANES = 8
VMEM_LIMIT = 56 * 1024 * 1024

HEAD_PAD = LANES


def _params(sem, vmem=VMEM_LIMIT):
    return pltpu.CompilerParams(dimension_semantics=sem, vmem_limit_bytes=vmem)


def _nt_dot(a, b):
    return lax.dot_general(a, b, (((1,), (1,)), ((), ())), preferred_element_type=F32)


def _dot(a, b):
    return jnp.dot(a, b, preferred_element_type=F32)


def _rms(x, g):
    return x * lax.rsqrt(jnp.mean(x * x, axis=-1, keepdims=True) + EPS) * g


def _mod_kernel(c_ref, w_ref, b_ref, o_ref):
    c = c_ref[...]
    sc = c * (1.0 / (1.0 + jnp.exp(-c)))
    o_ref[...] = _dot(sc.astype(BF16), w_ref[...].astype(BF16)) + b_ref[...]


def _modulation(c, w_mod, b_mod):
    n, d = c.shape
    nout = w_mod.shape[1]
    tn = d
    return pl.pallas_call(
        _mod_kernel,
        grid=(nout // tn,),
        in_specs=[
            pl.BlockSpec((n, d), lambda j: (0, 0)),
            pl.BlockSpec((d, tn), lambda j: (0, j)),
            pl.BlockSpec((1, tn), lambda j: (0, j)),
        ],
        out_specs=pl.BlockSpec((n, tn), lambda j: (0, j)),
        out_shape=jax.ShapeDtypeStruct((n, nout), F32),
        compiler_params=_params(("arbitrary",)),
        name="mod",
    )(c, w_mod, b_mod.reshape(1, nout))


def _pre_kernel(*refs, pos0, absorb, pool_w, q_lora, kv_lora):
    (x_ref, mod_ref, hist_ref, tab_ref, win_ref, wq_ref, wkv_ref, wpool_ref,
     gmix_ref, gq_ref, gkv_ref, pscale_ref) = refs[:12]
    if absorb:
        (q_ref, qlat_ref, lat_ref, kr_ref, pool_ref, nh_ref, ext_ref) = refs[12:]
    else:
        (q_ref, kcat_ref, v_ref, lat_ref, kr_ref, pool_ref, nh_ref, ext_ref) = refs[12:]
    si = pl.program_id(1)
    tb, ts, d = x_ref.shape
    r = tb * ts
    hq = N_HEADS * HEAD_PAD

    x = x_ref[...]
    shift1 = mod_ref[:, 0:1, :]
    scale1 = mod_ref[:, 1:2, :]
    h = _rms(x, gmix_ref[...]) * (1.0 + scale1) + shift1
    proj = _dot(h.reshape(r, d).astype(BF16), win_ref[...])
    o1 = pool_w
    o2 = o1 + q_lora
    o3 = o2 + kv_lora
    u = proj[:, :o1].reshape(tb, ts, pool_w)
    cq = proj[:, o1:o2]
    ckv = proj[:, o2:o3]
    kr_a = proj[:, o3:o3 + LANES].reshape(tb, ts, LANES)
    kr_b = proj[:, o3 + LANES:o3 + 2 * LANES].reshape(tb, ts, LANES)

    tab = tab_ref[...]
    t_qa = tab[:, 0 * LANES:1 * LANES]
    t_qb = tab[:, 1 * LANES:2 * LANES]
    t_ka = tab[:, 2 * LANES:3 * LANES]
    t_kb = tab[:, 3 * LANES:4 * LANES]

    qn = _rms(cq, gq_ref[...]).astype(BF16)
    if absorb:
        qab = _dot(qn, wq_ref[...])
        for hd in range(N_HEADS):
            lo = hd * HEAD_PAD
            qa = qab[:, lo:lo + HEAD_PAD].reshape(tb, ts, HEAD_PAD)
            qb = qab[:, hq + lo:hq + lo + HEAD_PAD].reshape(tb, ts, HEAD_PAD)
            qh = (qa * t_qa + qb * t_qb).astype(BF16)
            q_ref[:, :, lo:lo + HEAD_PAD] = qh
            ql = _dot(qh.reshape(r, HEAD_PAD), wkv_ref[hd])
            qlat_ref[:, :, hd * kv_lora:(hd + 1) * kv_lora] = ql.reshape(tb, ts, kv_lora).astype(BF16)
    else:
        qab = _dot(qn, wq_ref[...])
        for hd in range(N_HEADS):
            lo = hd * HEAD_PAD
            qa = qab[:, lo:lo + HEAD_PAD].reshape(tb, ts, HEAD_PAD)
            qb = qab[:, hq + lo:hq + lo + HEAD_PAD].reshape(tb, ts, HEAD_PAD)
            q_ref[:, :, lo:lo + HEAD_PAD] = (qa * t_qa + qb * t_qb).astype(BF16)

    lat = _rms(ckv, gkv_ref[...])
    lat_ref[...] = lat.reshape(tb, ts, kv_lora)
    kr128 = kr_a * t_ka + kr_b * t_kb
    if absorb:
        kr_ref[...] = kr128[:, :, :ROPE_DIM]
    else:
        for i in range(tb):
            kr_ref[i] = jnp.transpose(kr128[i])[:ROPE_DIM, :]
    if not absorb:
        kv = _dot(lat.astype(BF16), wkv_ref[...])
        for hd in range(N_HEADS):
            lo = hd * HEAD_PAD
            kcat_ref[:, :, lo:lo + HEAD_PAD] = (kv[:, lo:lo + HEAD_PAD].reshape(tb, ts, HEAD_PAD) + kr128).astype(BF16)
        v_ref[...] = kv[:, hq:].reshape(tb, ts, N_HEADS * V_DIM).astype(BF16)

    @pl.when(si == 0)
    def _():
        ext_ref[:, 0:HIST_PAD, :] = hist_ref[...]

    ext_ref[:, HIST_PAD:HIST_PAD + ts, :] = u
    gw = pool_w // len(POOL_WINDOWS)
    pos = (pos0 + (si * ts).astype(F32)
           + lax.broadcasted_iota(jnp.int32, (1, ts, gw), 1).astype(F32))
    parts = []
    for g, w in enumerate(POOL_WINDOWS):
        lo = g * gw
        acc = ext_ref[:, HIST_PAD:HIST_PAD + ts, lo:lo + gw]
        for i in range(1, w):
            acc = acc + ext_ref[:, HIST_PAD - i:HIST_PAD - i + ts, lo:lo + gw]
        inv = 1.0 / jnp.minimum(float(w), pos + 1.0)
        parts.append(acc * inv - u[:, :, lo:lo + gw])
    dpool = jnp.concatenate(parts, axis=-1).reshape(r, pool_w).astype(BF16)
    y = _dot(dpool, wpool_ref[...]) * pscale_ref[...]
    pool_ref[...] = y.reshape(tb, ts, pool_w).astype(BF16)
    tail = ext_ref[:, ts:ts + HIST_PAD, :]
    nh_ref[...] = tail
    ext_ref[:, 0:HIST_PAD, :] = tail


def _pre(x, mod, hist16, tab, w, *, pos0, absorb, tb, ts):
    b, s, d = x.shape
    pool_w = hist16.shape[-1]
    q_lora = w["gq"].shape[-1]
    kv_lora = w["gkv"].shape[-1]
    hq = N_HEADS * HEAD_PAD
    grid = (b // tb, s // ts)
    tile = lambda width: pl.BlockSpec((tb, ts, width), lambda i, j: (i, j, 0))
    full = lambda a: pl.BlockSpec(a.shape, lambda i, j: (0,) * a.ndim)
    wkv = w["wabs"] if absorb else w["wkv"]
    in_specs = [
        tile(d),
        pl.BlockSpec((tb, N_MOD, d), lambda i, j: (i, 0, 0)),
        pl.BlockSpec((tb, HIST_PAD, pool_w), lambda i, j: (i, 0, 0)),
        pl.BlockSpec((ts, 4 * LANES), lambda i, j: (j, 0)),
        full(w["win"]), full(w["wq"]), full(wkv), full(w["wpool"]),
        full(w["gmix"]), full(w["gq"]), full(w["gkv"]), full(w["pscale"]),
    ]
    sds = jax.ShapeDtypeStruct
    if absorb:
        out_shape = [sds((b, s, hq), BF16), sds((b, s, N_HEADS * kv_lora), BF16)]
        out_specs = [tile(hq), tile(N_HEADS * kv_lora)]
    else:
        out_shape = [sds((b, s, hq), BF16), sds((b, s, hq), BF16), sds((b, s, N_HEADS * V_DIM), BF16)]
        out_specs = [tile(hq), tile(hq), tile(N_HEADS * V_DIM)]
    if absorb:
        kr_shape, kr_spec = sds((b, s, ROPE_DIM), F32), tile(ROPE_DIM)
    else:
        kr_shape = sds((b, ROPE_DIM, s), F32)
        kr_spec = pl.BlockSpec((tb, ROPE_DIM, ts), lambda i, j: (i, 0, j))
    out_shape += [sds((b, s, kv_lora), F32), kr_shape, sds((b, s, pool_w), BF16),
                  sds((b, HIST_PAD, pool_w), F32)]
    out_specs += [tile(kv_lora), kr_spec, tile(pool_w),
                  pl.BlockSpec((tb, HIST_PAD, pool_w), lambda i, j: (i, 0, 0))]
    kern = functools.partial(_pre_kernel, pos0=float(pos0), absorb=absorb, pool_w=pool_w,
                             q_lora=q_lora, kv_lora=kv_lora)
    return pl.pallas_call(
        kern,
        grid=grid,
        in_specs=in_specs,
        out_specs=out_specs,
        out_shape=out_shape,
        scratch_shapes=[pltpu.VMEM((tb, HIST_PAD + ts, pool_w), F32)],
        compiler_params=_params(("arbitrary", "arbitrary")),
        name="pre_sample" if absorb else "pre_prompt",
    )(x, mod, hist16, tab, w["win"], w["wq"], wkv, w["wpool"], w["gmix"], w["gq"], w["gkv"], w["pscale"])


def _attn_kernel(q_ref, k_ref, v_ref, o_ref, *, tq):
    s = q_ref.shape[1]
    row = lax.broadcasted_iota(jnp.int32, (tq, tq), 0)
    col = lax.broadcasted_iota(jnp.int32, (tq, tq), 1)
    causal = row >= col
    first_head = lax.broadcasted_iota(jnp.int32, (tq, 2 * V_DIM), 1) < V_DIM
    for qi in range(s // tq):
        q0 = qi * tq
        outs = []
        for hh in range(2):
            hs = slice(hh * HEAD_PAD, (hh + 1) * HEAD_PAD)
            q = q_ref[0, q0:q0 + tq, hs]
            s_d = jnp.where(causal, _nt_dot(q, k_ref[0, q0:q0 + tq, hs]), NEG_INF)
            m = jnp.max(s_d, axis=-1, keepdims=True)
            if qi > 0:
                s_o = _nt_dot(q, k_ref[0, 0:q0, hs])
                m = jnp.maximum(m, jnp.max(s_o, axis=-1, keepdims=True))
                p_o = jnp.exp2(s_o - m)
                l = jnp.sum(p_o, axis=-1, keepdims=True)
                acc = _dot(p_o.astype(BF16), v_ref[0, 0:q0, :])
            p_d = jnp.exp2(s_d - m)
            if qi > 0:
                l = l + jnp.sum(p_d, axis=-1, keepdims=True)
                acc = acc + _dot(p_d.astype(BF16), v_ref[0, q0:q0 + tq, :])
            else:
                l = jnp.sum(p_d, axis=-1, keepdims=True)
                acc = _dot(p_d.astype(BF16), v_ref[0, q0:q0 + tq, :])
            outs.append(acc * (1.0 / l))
        o_ref[0, q0:q0 + tq, :] = jnp.where(first_head, outs[0], outs[1]).astype(BF16)


def _attention(q, kcat, v, *, tq):
    b, s, _ = q.shape
    pairs = N_HEADS // 2
    return pl.pallas_call(
        functools.partial(_attn_kernel, tq=tq),
        grid=(b, pairs),
        in_specs=[
            pl.BlockSpec((1, s, 2 * HEAD_PAD), lambda i, p: (i, 0, p)),
            pl.BlockSpec((1, s, 2 * HEAD_PAD), lambda i, p: (i, 0, p)),
            pl.BlockSpec((1, s, 2 * V_DIM), lambda i, p: (i, 0, p)),
        ],
        out_specs=pl.BlockSpec((1, s, 2 * V_DIM), lambda i, p: (i, 0, p)),
        out_shape=jax.ShapeDtypeStruct((b, s, N_HEADS * V_DIM), BF16),
        compiler_params=_params(("arbitrary", "arbitrary")),
        name="attn_prompt",
    )(q, kcat, v)


def _dec_kernel(pt_ref, qlat_ref, qpad_ref, latn_ref, krn_ref, clat_hbm, ckr_hbm, o_ref,
                latbuf, krbuf, latbf, s_ref, sem, *, n_pages, page, chunk, s_new):
    b = pl.program_id(0)
    nb = pl.num_programs(0)
    slot = b % 2
    t_past = n_pages * page
    t_all = t_past + page
    rows = qlat_ref.shape[1]

    def page_copies(bb, sl, p):
        pg = pt_ref[bb, p]
        dst = pl.ds(pl.multiple_of(p * page, page), page)
        return (pltpu.make_async_copy(clat_hbm.at[pg], latbuf.at[sl, dst], sem.at[0, sl]),
                pltpu.make_async_copy(ckr_hbm.at[pg], krbuf.at[sl, :, dst], sem.at[1, sl]))

    def start_all(bb, sl):
        def body(p, carry):
            c_lat, c_kr = page_copies(bb, sl, p)
            c_lat.start()
            c_kr.start()
            return carry
        lax.fori_loop(0, n_pages, body, 0)

    def wait_all(bb, sl):
        def body(p, carry):
            c_lat, c_kr = page_copies(bb, sl, p)
            c_lat.wait()
            c_kr.wait()
            return carry
        lax.fori_loop(0, n_pages, body, 0)

    @pl.when(b == 0)
    def _():
        start_all(0, 0)

    @pl.when(b + 1 < nb)
    def _():
        start_all(b + 1, 1 - slot)

    latbuf[slot, t_past:t_all, :] = jnp.zeros((page, latbuf.shape[-1]), F32)
    krbuf[slot, :, t_past:t_all] = jnp.zeros((ROPE_DIM, page), F32)
    latbuf[slot, t_past:t_past + s_new, :] = latn_ref[0]
    krbuf[slot, :, t_past:t_past + s_new] = krn_ref[0]

    q1 = qlat_ref[0]
    q2 = qpad_ref[0][:, :ROPE_DIM]

    wait_all(b, slot)

    for lo in list(range(0, t_past, chunk)) + [t_past]:
        n = min(chunk, t_all - lo)
        lat_c = latbuf[slot, lo:lo + n, :].astype(BF16)
        latbf[lo:lo + n, :] = lat_c
        s_ref[:, lo:lo + n] = _nt_dot(q1, lat_c) + _dot(q2, krbuf[slot, :, lo:lo + n].astype(BF16))

    tok = lax.broadcasted_iota(jnp.int32, (rows, page), 0) // N_HEADS
    col = lax.broadcasted_iota(jnp.int32, (rows, page), 1)
    s_ref[:, t_past:t_all] = jnp.where(col <= tok, s_ref[:, t_past:t_all], NEG_INF)

    s_all = s_ref[...]
    m = jnp.max(s_all, axis=-1, keepdims=True)
    p = jnp.exp2(s_all - m)
    l = jnp.sum(p, axis=-1, keepdims=True)
    acc = _dot(p.astype(BF16), latbf[...])
    o_ref[0] = (acc * (1.0 / l)).astype(BF16)


def _decode_attention(page_table, qlat, qpad, lat_new, kr_new, cache_lat, cache_kr):
    db, rows, kv_lora = qlat.shape
    n_pages = page_table.shape[1]
    page = cache_lat.shape[1]
    s_new = lat_new.shape[1]
    t_past = n_pages * page
    chunk = min(2048, t_past)
    kern = functools.partial(_dec_kernel, n_pages=n_pages, page=page, chunk=chunk, s_new=s_new)
    grid_spec = pltpu.PrefetchScalarGridSpec(
        num_scalar_prefetch=1,
        grid=(db,),
        in_specs=[
            pl.BlockSpec((1, rows, kv_lora), lambda i, pt: (i, 0, 0)),
            pl.BlockSpec((1, rows, HEAD_PAD), lambda i, pt: (i, 0, 0)),
            pl.BlockSpec((1, s_new, kv_lora), lambda i, pt: (i, 0, 0)),
            pl.BlockSpec((1, ROPE_DIM, s_new), lambda i, pt: (i, 0, 0)),
            pl.BlockSpec(memory_space=pl.ANY),
            pl.BlockSpec(memory_space=pl.ANY),
        ],
        out_specs=pl.BlockSpec((1, rows, kv_lora), lambda i, pt: (i, 0, 0)),
        scratch_shapes=[
            pltpu.VMEM((2, t_past + page, kv_lora), F32),
            pltpu.VMEM((2, ROPE_DIM, t_past + page), F32),
            pltpu.VMEM((t_past + page, kv_lora), BF16),
            pltpu.VMEM((rows, t_past + page), F32),
            pltpu.SemaphoreType.DMA((2, 2)),
        ],
    )
    return pl.pallas_call(
        kern,
        grid_spec=grid_spec,
        out_shape=jax.ShapeDtypeStruct((db, rows, kv_lora), BF16),
        compiler_params=_params(("arbitrary",)),
        name="attn_sample",
    )(page_table, qlat, qpad, lat_new, kr_new, cache_lat, cache_kr)


def _post_kernel(x_ref, mod_ref, pool_ref, att_ref, wout_ref, wuv_ref, wup_ref, wdown_ref,
                 gmlp_ref, gfin_ref, y_ref, *, absorb, final, pool_w, kv_lora, ff_chunk):
    tb, ts, d = x_ref.shape
    r = tb * ts
    x = x_ref[...]
    gate1 = mod_ref[:, 2:3, :]
    shift2 = mod_ref[:, 3:4, :]
    scale2 = mod_ref[:, 4:5, :]
    gate2 = mod_ref[:, 5:6, :]

    mix = _dot(pool_ref[...].reshape(r, pool_w), wout_ref[0:pool_w, :])
    if absorb:
        for p in range(N_HEADS // 2):
            o_pair = None
            for hh in range(2):
                hd = 2 * p + hh
                o_lat = att_ref[:, :, hd * kv_lora:(hd + 1) * kv_lora].reshape(r, kv_lora)
                t = _dot(o_lat, wuv_ref[hd])
                o_pair = t if o_pair is None else o_pair + t
            lo = pool_w + p * 2 * V_DIM
            mix = mix + _dot(o_pair.astype(BF16), wout_ref[lo:lo + 2 * V_DIM, :])
    else:
        mix = mix + _dot(att_ref[...].reshape(r, N_HEADS * V_DIM), wout_ref[pool_w:, :])
    x1 = x + gate1 * mix.reshape(tb, ts, d)

    h2 = (_rms(x1, gmlp_ref[...]) * (1.0 + scale2) + shift2).reshape(r, d).astype(BF16)
    d_ff = wup_ref.shape[1]
    mlp = None
    for c in range(d_ff // ff_chunk):
        cs = slice(c * ff_chunk, (c + 1) * ff_chunk)
        hid = jnp.maximum(_dot(h2, wup_ref[:, cs]), 0.0)
        t = _dot((hid * hid).astype(BF16), wdown_ref[cs, :])
        mlp = t if mlp is None else mlp + t
    x2 = x1 + gate2 * mlp.reshape(tb, ts, d)
    y_ref[...] = _rms(x2, gfin_ref[...]) if final else x2


def _post(x, mod, pool, att, w, *, absorb, final, tb, ts):
    b, s, d = x.shape
    pool_w = pool.shape[-1]
    kv_lora = w["gkv"].shape[-1]
    tile = lambda width: pl.BlockSpec((tb, ts, width), lambda i, j: (i, j, 0))
    const = lambda a: pl.BlockSpec(a.shape, lambda i, j: (0,) * a.ndim, pipeline_mode=pl.Buffered(1))
    kern = functools.partial(_post_kernel, absorb=absorb, final=final, pool_w=pool_w, kv_lora=kv_lora,
                             ff_chunk=min(1024, w["wup"].shape[1]))
    return pl.pallas_call(
        kern,
        grid=(b // tb, s // ts),
        in_specs=[
            tile(d),
            pl.BlockSpec((tb, N_MOD, d), lambda i, j: (i, 0, 0)),
            tile(pool_w),
            tile(att.shape[-1]),
            const(w["wout"]), const(w["wuvp"]), const(w["wup"]), const(w["wdown"]),
            const(w["gmlp"]), const(w["gfin"]),
        ],
        out_specs=tile(d),
        out_shape=jax.ShapeDtypeStruct((b, s, d), F32),
        compiler_params=_params(("arbitrary", "arbitrary")),
        name="post_sample" if absorb else "post_prompt",
    )(x, mod, pool, att, w["wout"], w["wuvp"], w["wup"], w["wdown"], w["gmlp"], w["gfin"])


def _rot_half(wr):
    half = ROPE_DIM // 2
    return jnp.concatenate([-wr[..., half:], wr[..., :half]], axis=-1)


def _layer_weights(w_in, g_mix, g_q, w_uq, g_kv, w_uk, w_uv, w_pool, pool_scale, w_out, g_mlp,
                   w_up, w_down, g_final):
    d, _ = w_in.shape
    q_lora = g_q.shape[-1]
    kv_lora = g_kv.shape[-1]
    n_groups, gw, _ = w_pool.shape
    pool_w = n_groups * gw
    o3 = pool_w + q_lora + kv_lora
    zpad = lambda rows, cols: jnp.zeros((rows, cols), F32)

    w_kr = w_in[:, o3:]
    win = jnp.concatenate([w_in[:, :o3], w_kr, zpad(d, LANES - ROPE_DIM),
                           _rot_half(w_kr), zpad(d, LANES - ROPE_DIM)], axis=1)

    q_rope = w_uq[:, :, NOPE_DIM:]
    q_nope = w_uq[:, :, :NOPE_DIM]
    hz = jnp.zeros((q_lora, N_HEADS, HEAD_PAD - ROPE_DIM - NOPE_DIM), F32)
    wq_a = jnp.concatenate([q_rope, q_nope, hz], axis=-1).reshape(q_lora, N_HEADS * HEAD_PAD)
    wq_b = jnp.concatenate([_rot_half(q_rope), jnp.zeros((q_lora, N_HEADS, HEAD_PAD - ROPE_DIM), F32)],
                           axis=-1).reshape(q_lora, N_HEADS * HEAD_PAD)
    wq = jnp.concatenate([wq_a, wq_b], axis=1)

    kz_lo = jnp.zeros((kv_lora, N_HEADS, ROPE_DIM), F32)
    kz_hi = jnp.zeros((kv_lora, N_HEADS, HEAD_PAD - ROPE_DIM - NOPE_DIM), F32)
    wk = jnp.concatenate([kz_lo, w_uk, kz_hi], axis=-1).reshape(kv_lora, N_HEADS * HEAD_PAD)
    wkv = jnp.concatenate([wk, w_uv.reshape(kv_lora, N_HEADS * V_DIM)], axis=1)
    wabs = jnp.transpose(jnp.concatenate([kz_lo, w_uk, kz_hi], axis=-1), (1, 2, 0))

    uv = jnp.transpose(w_uv, (1, 0, 2))
    uz = jnp.zeros_like(uv)
    even = (jnp.arange(N_HEADS) % 2 == 0)[:, None, None]
    wuvp = jnp.where(even, jnp.concatenate([uv, uz], axis=-1), jnp.concatenate([uz, uv], axis=-1))

    wpool = jax.scipy.linalg.block_diag(*[w_pool[g] for g in range(n_groups)])

    bf = lambda a: a.astype(BF16)
    return dict(
        win=bf(win), wq=bf(wq), wkv=bf(wkv), wabs=bf(wabs), wuvp=bf(wuvp), wpool=bf(wpool),
        wout=bf(w_out), wup=bf(w_up), wdown=bf(w_down),
        gmix=g_mix.reshape(1, -1), gq=g_q.reshape(1, -1), gkv=g_kv.reshape(1, -1),
        pscale=pool_scale.reshape(1, -1), gmlp=g_mlp.reshape(1, -1), gfin=g_final.reshape(1, -1),
    )


def _rope_tables(positions):
    inv = ROPE_BASE ** (-jnp.arange(0, ROPE_DIM, 2, dtype=F32) / ROPE_DIM)
    ang = positions[:, None] * inv[None, :]
    cos2 = jnp.tile(jnp.cos(ang), (1, 2))
    sin2 = jnp.tile(jnp.sin(ang), (1, 2))
    n = positions.shape[0]
    qs = SOFTMAX_SCALE * LOG2E
    t_qa = jnp.concatenate([cos2, jnp.ones((n, NOPE_DIM), F32),
                            jnp.zeros((n, HEAD_PAD - ROPE_DIM - NOPE_DIM), F32)], axis=1) * qs
    zr = jnp.zeros((n, HEAD_PAD - ROPE_DIM), F32)
    t_qb = jnp.concatenate([sin2, zr], axis=1) * qs
    t_ka = jnp.concatenate([cos2, zr], axis=1)
    t_kb = jnp.concatenate([sin2, zr], axis=1)
    return jnp.concatenate([t_qa, t_qb, t_ka, t_kb], axis=1)


def _seq_tile(s, target):
    t = min(s, target)
    while s % t:
        t //= 2
    return t


def kernel(x_prompt, x_sample, cache_latent, cache_krope, state_pool, page_table, c_prompt, c_sample,
           w_mod, b_mod, g_mix, w_in, g_q, w_uq, g_kv, w_uk, w_uv, w_pool, pool_scale, w_out,
           g_mlp, w_up, w_down, g_final):
    bp, sp, d = x_prompt.shape
    db, ss, _ = x_sample.shape
    depth = w_mod.shape[0]
    n_pages = page_table.shape[1]
    page = cache_latent.shape[2]
    past = n_pages * page
    pool_w = state_pool.shape[-1]
    kv_lora = g_kv.shape[-1]

    tab_p = _rope_tables(jnp.arange(sp, dtype=F32))
    tab_s = _rope_tables(jnp.arange(ss, dtype=F32) + float(past))
    ts_p = _seq_tile(sp, 512)
    tb_s = _seq_tile(db, max(1, 256 // ss))

    xp, xs = x_prompt, x_sample
    outs = [[] for _ in range(6)]
    for l in range(depth):
        w = _layer_weights(w_in[l], g_mix[l], g_q[l], w_uq[l], g_kv[l], w_uk[l], w_uv[l], w_pool[l],
                           pool_scale[l], w_out[l], g_mlp[l], w_up[l], w_down[l], g_final)
        mod = _modulation(jnp.concatenate([c_prompt, c_sample], axis=0), w_mod[l], b_mod[l])
        mod = mod.reshape(bp + db, N_MOD, d)
        mod_p, mod_s = mod[:bp], mod[bp:]

        hist_p = jnp.zeros((bp, HIST_PAD, pool_w), F32)
        q, kcat, v, lat_p, kr_p, pool_p, nh_p = _pre(
            xp, mod_p, hist_p, tab_p, w, pos0=0.0, absorb=False, tb=1, ts=ts_p)
        att_p = _attention(q, kcat, v, tq=_seq_tile(sp, 512))
        final = l == depth - 1
        xp = _post(xp, mod_p, pool_p, att_p, w, absorb=False, final=final, tb=1, ts=ts_p)

        hist_s = jnp.pad(state_pool[l], ((0, 0), (HIST_PAD - POOL_HIST, 0), (0, 0)))
        qs, qlat, lat_s, kr_s, pool_s, nh_s = _pre(
            xs, mod_s, hist_s, tab_s, w, pos0=float(past), absorb=True, tb=tb_s, ts=ss)
        rows = ss * N_HEADS
        o_lat = _decode_attention(
            page_table,
            qlat.reshape(db, rows, kv_lora),
            qs.reshape(db, rows, HEAD_PAD),
            lat_s, jnp.swapaxes(kr_s, 1, 2),
            cache_latent[l], jnp.swapaxes(cache_krope[l], 1, 2))
        xs = _post(xs, mod_s, pool_s, o_lat.reshape(db, ss, N_HEADS * kv_lora), w,
                   absorb=True, final=final, tb=tb_s, ts=ss)

        for lst, val in zip(outs, (lat_p, jnp.swapaxes(kr_p, 1, 2), nh_p[:, HIST_PAD - POOL_HIST:],
                                   lat_s, kr_s, nh_s[:, HIST_PAD - POOL_HIST:])):
            lst.append(val)
    return (xp, xs) + tuple(jnp.stack(o) for o in outs)
```

```python
import functools
import math

import jax
import jax.numpy as jnp
from jax import lax
from jax.experimental import pallas as pl
from jax.experimental.pallas import tpu as pltpu

F32 = jnp.float32
BF16 = jnp.bfloat16

N_HEADS = 8
NOPE_DIM = 64
ROPE_DIM = 32
V_DIM = 64
POOL_WINDOWS = (2, 4, 8, 16)
POOL_HIST = max(POOL_WINDOWS) - 1
HIST_PAD = POOL_HIST + 1
N_MOD = 6
ROPE_BASE = 10000.0
EPS = 1e-6
NEG_INF = -1e30
SOFTMAX_SCALE = 1.0 / math.sqrt(NOPE_DIM + ROPE_DIM)
LOG2E = math.log2(math.e)

LANES = 128
SUBLANES = 8
VMEM_LIMIT = 56 * 1024 * 1024

HEAD_PAD = LANES


def _params(sem, vmem=VMEM_LIMIT):
    return pltpu.CompilerParams(dimension_semantics=sem, vmem_limit_bytes=vmem)


def _nt_dot(a, b):
    return lax.dot_general(a, b, (((1,), (1,)), ((), ())), preferred_element_type=F32)


def _dot(a, b):
    return jnp.dot(a, b, preferred_element_type=F32)


def _rms(x, g):
    return x * lax.rsqrt(jnp.mean(x * x, axis=-1, keepdims=True) + EPS) * g


def _mod_kernel(c_ref, w_ref, b_ref, o_ref):
    c = c_ref[...]
    sc = c * (1.0 / (1.0 + jnp.exp(-c)))
    o_ref[...] = _dot(sc.astype(BF16), w_ref[...].astype(BF16)) + b_ref[...]


def _modulation(c, w_mod, b_mod):
    n, d = c.shape
    nout = w_mod.shape[1]
    tn = d
    return pl.pallas_call(
        _mod_kernel,
        grid=(nout // tn,),
        in_specs=[
            pl.BlockSpec((n, d), lambda j: (0, 0)),
            pl.BlockSpec((d, tn), lambda j: (0, j)),
            pl.BlockSpec((1, tn), lambda j: (0, j)),
        ],
        out_specs=pl.BlockSpec((n, tn), lambda j: (0, j)),
        out_shape=jax.ShapeDtypeStruct((n, nout), F32),
        compiler_params=_params(("arbitrary",)),
        name="mod",
    )(c, w_mod, b_mod.reshape(1, nout))


def _pre_kernel(*refs, pos0, absorb, pool_w, q_lora, kv_lora):
    (x_ref, mod_ref, hist_ref, tab_ref, win_ref, wq_ref, wkv_ref, wpool_ref,
     gmix_ref, gq_ref, gkv_ref, pscale_ref) = refs[:12]
    if absorb:
        (q_ref, qlat_ref, lat_ref, kr_ref, pool_ref, nh_ref, ext_ref) = refs[12:]
    else:
        (q_ref, kcat_ref, v_ref, lat_ref, kr_ref, pool_ref, nh_ref, ext_ref) = refs[12:]
    si = pl.program_id(1)
    tb, ts, d = x_ref.shape
    r = tb * ts
    hq = N_HEADS * HEAD_PAD

    x = x_ref[...]
    shift1 = mod_ref[:, 0:1, :]
    scale1 = mod_ref[:, 1:2, :]
    h = _rms(x, gmix_ref[...]) * (1.0 + scale1) + shift1
    proj = _dot(h.reshape(r, d).astype(BF16), win_ref[...])
    o1 = pool_w
    o2 = o1 + q_lora
    o3 = o2 + kv_lora
    u = proj[:, :o1].reshape(tb, ts, pool_w)
    cq = proj[:, o1:o2]
    ckv = proj[:, o2:o3]
    kr_a = proj[:, o3:o3 + LANES].reshape(tb, ts, LANES)
    kr_b = proj[:, o3 + LANES:o3 + 2 * LANES].reshape(tb, ts, LANES)

    tab = tab_ref[...]
    t_qa = tab[:, 0 * LANES:1 * LANES]
    t_qb = tab[:, 1 * LANES:2 * LANES]
    t_ka = tab[:, 2 * LANES:3 * LANES]
    t_kb = tab[:, 3 * LANES:4 * LANES]

    qn = _rms(cq, gq_ref[...]).astype(BF16)
    if absorb:
        qab = _dot(qn, wq_ref[...])
        for hd in range(N_HEADS):
            lo = hd * HEAD_PAD
            qa = qab[:, lo:lo + HEAD_PAD].reshape(tb, ts, HEAD_PAD)
            qb = qab[:, hq + lo:hq + lo + HEAD_PAD].reshape(tb, ts, HEAD_PAD)
            qh = (qa * t_qa + qb * t_qb).astype(BF16)
            q_ref[:, :, lo:lo + HEAD_PAD] = qh
            ql = _dot(qh.reshape(r, HEAD_PAD), wkv_ref[hd])
            qlat_ref[:, :, hd * kv_lora:(hd + 1) * kv_lora] = ql.reshape(tb, ts, kv_lora).astype(BF16)
    else:
        qab = _dot(qn, wq_ref[...])
        for hd in range(N_HEADS):
            lo = hd * HEAD_PAD
            qa = qab[:, lo:lo + HEAD_PAD].reshape(tb, ts, HEAD_PAD)
            qb = qab[:, hq + lo:hq + lo + HEAD_PAD].reshape(tb, ts, HEAD_PAD)
            q_ref[:, :, lo:lo + HEAD_PAD] = (qa * t_qa + qb * t_qb).astype(BF16)

    lat = _rms(ckv, gkv_ref[...])
    lat_ref[...] = lat.reshape(tb, ts, kv_lora)
    kr128 = kr_a * t_ka + kr_b * t_kb
    if absorb:
        kr_ref[...] = kr128[:, :, :ROPE_DIM]
    else:
        for i in range(tb):
            kr_ref[i] = jnp.transpose(kr128[i])[:ROPE_DIM, :]
    if not absorb:
        kv = _dot(lat.astype(BF16), wkv_ref[...])
        for hd in range(N_HEADS):
            lo = hd * HEAD_PAD
            kcat_ref[:, :, lo:lo + HEAD_PAD] = (kv[:, lo:lo + HEAD_PAD].reshape(tb, ts, HEAD_PAD) + kr128).astype(BF16)
        v_ref[...] = kv[:, hq:].reshape(tb, ts, N_HEADS * V_DIM).astype(BF16)

    @pl.when(si == 0)
    def _():
        ext_ref[:, 0:HIST_PAD, :] = hist_ref[...]

    ext_ref[:, HIST_PAD:HIST_PAD + ts, :] = u
    gw = pool_w // len(POOL_WINDOWS)
    pos = (pos0 + (si * ts).astype(F32)
           + lax.broadcasted_iota(jnp.int32, (1, ts, gw), 1).astype(F32))
    parts = []
    for g, w in enumerate(POOL_WINDOWS):
        lo = g * gw
        acc = ext_ref[:, HIST_PAD:HIST_PAD + ts, lo:lo + gw]
        for i in range(1, w):
            acc = acc + ext_ref[:, HIST_PAD - i:HIST_PAD - i + ts, lo:lo + gw]
        inv = 1.0 / jnp.minimum(float(w), pos + 1.0)
        parts.append(acc * inv - u[:, :, lo:lo + gw])
    dpool = jnp.concatenate(parts, axis=-1).reshape(r, pool_w).astype(BF16)
    y = _dot(dpool, wpool_ref[...]) * pscale_ref[...]
    pool_ref[...] = y.reshape(tb, ts, pool_w).astype(BF16)
    tail = ext_ref[:, ts:ts + HIST_PAD, :]
    nh_ref[...] = tail
    ext_ref[:, 0:HIST_PAD, :] = tail


def _pre(x, mod, hist16, tab, w, *, pos0, absorb, tb, ts):
    b, s, d = x.shape
    pool_w = hist16.shape[-1]
    q_lora = w["gq"].shape[-1]
    kv_lora = w["gkv"].shape[-1]
    hq = N_HEADS * HEAD_PAD
    grid = (b // tb, s // ts)
    tile = lambda width: pl.BlockSpec((tb, ts, width), lambda i, j: (i, j, 0))
    full = lambda a: pl.BlockSpec(a.shape, lambda i, j: (0,) * a.ndim)
    wkv = w["wabs"] if absorb else w["wkv"]
    in_specs = [
        tile(d),
        pl.BlockSpec((tb, N_MOD, d), lambda i, j: (i, 0, 0)),
        pl.BlockSpec((tb, HIST_PAD, pool_w), lambda i, j: (i, 0, 0)),
        pl.BlockSpec((ts, 4 * LANES), lambda i, j: (j, 0)),
        full(w["win"]), full(w["wq"]), full(wkv), full(w["wpool"]),
        full(w["gmix"]), full(w["gq"]), full(w["gkv"]), full(w["pscale"]),
    ]
    sds = jax.ShapeDtypeStruct
    if absorb:
        out_shape = [sds((b, s, hq), BF16), sds((b, s, N_HEADS * kv_lora), BF16)]
        out_specs = [tile(hq), tile(N_HEADS * kv_lora)]
    else:
        out_shape = [sds((b, s, hq), BF16), sds((b, s, hq), BF16), sds((b, s, N_HEADS * V_DIM), BF16)]
        out_specs = [tile(hq), tile(hq), tile(N_HEADS * V_DIM)]
    if absorb:
        kr_shape, kr_spec = sds((b, s, ROPE_DIM), F32), tile(ROPE_DIM)
    else:
        kr_shape = sds((b, ROPE_DIM, s), F32)
        kr_spec = pl.BlockSpec((tb, ROPE_DIM, ts), lambda i, j: (i, 0, j))
    out_shape += [sds((b, s, kv_lora), F32), kr_shape, sds((b, s, pool_w), BF16),
                  sds((b, HIST_PAD, pool_w), F32)]
    out_specs += [tile(kv_lora), kr_spec, tile(pool_w),
                  pl.BlockSpec((tb, HIST_PAD, pool_w), lambda i, j: (i, 0, 0))]
    kern = functools.partial(_pre_kernel, pos0=float(pos0), absorb=absorb, pool_w=pool_w,
                             q_lora=q_lora, kv_lora=kv_lora)
    return pl.pallas_call(
        kern,
        grid=grid,
        in_specs=in_specs,
        out_specs=out_specs,
        out_shape=out_shape,
        scratch_shapes=[pltpu.VMEM((tb, HIST_PAD + ts, pool_w), F32)],
        compiler_params=_params(("arbitrary", "arbitrary")),
        name="pre_sample" if absorb else "pre_prompt",
    )(x, mod, hist16, tab, w["win"], w["wq"], wkv, w["wpool"], w["gmix"], w["gq"], w["gkv"], w["pscale"])


def _attn_kernel(q_ref, k_ref, v_ref, o_ref, *, tq):
    s = q_ref.shape[1]
    row = lax.broadcasted_iota(jnp.int32, (tq, tq), 0)
    col = lax.broadcasted_iota(jnp.int32, (tq, tq), 1)
    causal = row >= col
    first_head = lax.broadcasted_iota(jnp.int32, (tq, 2 * V_DIM), 1) < V_DIM
    for qi in range(s // tq):
        q0 = qi * tq
        outs = []
        for hh in range(2):
            hs = slice(hh * HEAD_PAD, (hh + 1) * HEAD_PAD)
            q = q_ref[0, q0:q0 + tq, hs]
            s_d = jnp.where(causal, _nt_dot(q, k_ref[0, q0:q0 + tq, hs]), NEG_INF)
            m = jnp.max(s_d, axis=-1, keepdims=True)
            if qi > 0:
                s_o = _nt_dot(q, k_ref[0, 0:q0, hs])
                m = jnp.maximum(m, jnp.max(s_o, axis=-1, keepdims=True))
                p_o = jnp.exp2(s_o - m)
                l = jnp.sum(p_o, axis=-1, keepdims=True)
                acc = _dot(p_o.astype(BF16), v_ref[0, 0:q0, :])
            p_d = jnp.exp2(s_d - m)
            if qi > 0:
                l = l + jnp.sum(p_d, axis=-1, keepdims=True)
                acc = acc + _dot(p_d.astype(BF16), v_ref[0, q0:q0 + tq, :])
            else:
                l = jnp.sum(p_d, axis=-1, keepdims=True)
                acc = _dot(p_d.astype(BF16), v_ref[0, q0:q0 + tq, :])
            outs.append(acc * (1.0 / l))
        o_ref[0, q0:q0 + tq, :] = jnp.where(first_head, outs[0], outs[1]).astype(BF16)


def _attention(q, kcat, v, *, tq):
    b, s, _ = q.shape
    pairs = N_HEADS // 2
    return pl.pallas_call(
        functools.partial(_attn_kernel, tq=tq),
        grid=(b, pairs),
        in_specs=[
            pl.BlockSpec((1, s, 2 * HEAD_PAD), lambda i, p: (i, 0, p)),
            pl.BlockSpec((1, s, 2 * HEAD_PAD), lambda i, p: (i, 0, p)),
            pl.BlockSpec((1, s, 2 * V_DIM), lambda i, p: (i, 0, p)),
        ],
        out_specs=pl.BlockSpec((1, s, 2 * V_DIM), lambda i, p: (i, 0, p)),
        out_shape=jax.ShapeDtypeStruct((b, s, N_HEADS * V_DIM), BF16),
        compiler_params=_params(("arbitrary", "arbitrary")),
        name="attn_prompt",
    )(q, kcat, v)


def _dec_kernel(pt_ref, qlat_ref, qpad_ref, latn_ref, krn_ref, clat_hbm, ckr_hbm, o_ref,
                latbuf, krbuf, latbf, s_ref, sem, *, n_pages, page, chunk, s_new):
    g = pl.program_id(0)
    ng = pl.num_programs(0)
    t_past = n_pages * page
    t_all = t_past + page
    rows = qlat_ref.shape[1]

    def page_copies(bb, sl, p):
        pg = pt_ref[bb, p]
        dst = pl.ds(p * page, page)
        return (pltpu.make_async_copy(clat_hbm.at[pg], latbuf.at[sl, dst], sem.at[0, sl]),
                pltpu.make_async_copy(ckr_hbm.at[pg], krbuf.at[sl, :, dst], sem.at[1, sl]))

    def start_all(bb, sl):
        for p in range(n_pages):
            for c in page_copies(bb, sl, p):
                c.start()

    def wait_all(bb, sl):
        for p in range(n_pages):
            for c in page_copies(bb, sl, p):
                c.wait()

    def attend(sl):
        latbuf[sl, t_past:t_all, :] = jnp.zeros((page, latbuf.shape[-1]), F32)
        krbuf[sl, :, t_past:t_all] = jnp.zeros((ROPE_DIM, page), F32)
        latbuf[sl, t_past:t_past + s_new, :] = latn_ref[sl]
        krbuf[sl, :, t_past:t_past + s_new] = krn_ref[sl]
        q1 = qlat_ref[sl]
        q2 = qpad_ref[sl][:, :ROPE_DIM]

        for lo in list(range(0, t_past, chunk)) + [t_past]:
            n = min(chunk, t_all - lo)
            lat_c = latbuf[sl, lo:lo + n, :].astype(BF16)
            latbf[lo:lo + n, :] = lat_c
            s_ref[:, lo:lo + n] = _nt_dot(q1, lat_c) + _dot(q2, krbuf[sl, :, lo:lo + n].astype(BF16))

        tok = lax.broadcasted_iota(jnp.int32, (rows, page), 0) // N_HEADS
        col = lax.broadcasted_iota(jnp.int32, (rows, page), 1)
        s_ref[:, t_past:t_all] = jnp.where(col <= tok, s_ref[:, t_past:t_all], NEG_INF)

        s_all = s_ref[...]
        m = jnp.max(s_all, axis=-1, keepdims=True)
        p = jnp.exp2(s_all - m)
        inv_l = 1.0 / jnp.sum(p, axis=-1, keepdims=True)
        p = p.astype(BF16)
        half = (t_all // 2) // page * page
        o_a = _dot(p[:, :half], latbf[0:half, :]) * inv_l
        o_b = _dot(p[:, half:], latbf[half:t_all, :]) * inv_l
        o_ref[sl] = (o_a + o_b).astype(BF16)

    b0 = g * 2

    @pl.when(g == 0)
    def _():
        start_all(0, 0)

    start_all(b0 + 1, 1)
    wait_all(b0, 0)
    attend(0)

    @pl.when(g + 1 < ng)
    def _():
        start_all(b0 + 2, 0)

    wait_all(b0 + 1, 1)
    attend(1)


def _decode_attention(page_table, qlat, qpad, lat_new, kr_new, cache_lat, cache_kr):
    db, rows, kv_lora = qlat.shape
    n_pages = page_table.shape[1]
    page = cache_lat.shape[1]
    s_new = lat_new.shape[1]
    t_past = n_pages * page
    chunk = min(2048, t_past)
    assert db % 2 == 0, "the decode kernel alternates two buffer slots within a step"
    kern = functools.partial(_dec_kernel, n_pages=n_pages, page=page, chunk=chunk, s_new=s_new)
    grid_spec = pltpu.PrefetchScalarGridSpec(
        num_scalar_prefetch=1,
        grid=(db // 2,),
        in_specs=[
            pl.BlockSpec((2, rows, kv_lora), lambda i, pt: (i, 0, 0)),
            pl.BlockSpec((2, rows, HEAD_PAD), lambda i, pt: (i, 0, 0)),
            pl.BlockSpec((2, s_new, kv_lora), lambda i, pt: (i, 0, 0)),
            pl.BlockSpec((2, ROPE_DIM, s_new), lambda i, pt: (i, 0, 0)),
            pl.BlockSpec(memory_space=pl.ANY),
            pl.BlockSpec(memory_space=pl.ANY),
        ],
        out_specs=pl.BlockSpec((2, rows, kv_lora), lambda i, pt: (i, 0, 0)),
        scratch_shapes=[
            pltpu.VMEM((2, t_past + page, kv_lora), F32),
            pltpu.VMEM((2, ROPE_DIM, t_past + page), F32),
            pltpu.VMEM((t_past + page, kv_lora), BF16),
            pltpu.VMEM((rows, t_past + page), F32),
            pltpu.SemaphoreType.DMA((2, 2)),
        ],
    )
    return pl.pallas_call(
        kern,
        grid_spec=grid_spec,
        out_shape=jax.ShapeDtypeStruct((db, rows, kv_lora), BF16),
        compiler_params=_params(("arbitrary",)),
        name="attn_sample",
    )(page_table, qlat, qpad, lat_new, kr_new, cache_lat, cache_kr)


def _post_kernel(x_ref, mod_ref, pool_ref, att_ref, wout_ref, wuv_ref, wup_ref, wdown_ref,
                 gmlp_ref, gfin_ref, y_ref, *, absorb, final, pool_w, kv_lora, ff_chunk):
    tb, ts, d = x_ref.shape
    r = tb * ts
    x = x_ref[...]
    gate1 = mod_ref[:, 2:3, :]
    shift2 = mod_ref[:, 3:4, :]
    scale2 = mod_ref[:, 4:5, :]
    gate2 = mod_ref[:, 5:6, :]

    mix = _dot(pool_ref[...].reshape(r, pool_w), wout_ref[0:pool_w, :])
    if absorb:
        for p in range(N_HEADS // 2):
            o_pair = None
            for hh in range(2):
                hd = 2 * p + hh
                o_lat = att_ref[:, :, hd * kv_lora:(hd + 1) * kv_lora].reshape(r, kv_lora)
                t = _dot(o_lat, wuv_ref[hd])
                o_pair = t if o_pair is None else o_pair + t
            lo = pool_w + p * 2 * V_DIM
            mix = mix + _dot(o_pair.astype(BF16), wout_ref[lo:lo + 2 * V_DIM, :])
    else:
        mix = mix + _dot(att_ref[...].reshape(r, N_HEADS * V_DIM), wout_ref[pool_w:, :])
    x1 = x + gate1 * mix.reshape(tb, ts, d)

    h2 = (_rms(x1, gmlp_ref[...]) * (1.0 + scale2) + shift2).reshape(r, d).astype(BF16)
    d_ff = wup_ref.shape[1]
    mlp = None
    for c in range(d_ff // ff_chunk):
        cs = slice(c * ff_chunk, (c + 1) * ff_chunk)
        hid = jnp.maximum(_dot(h2, wup_ref[:, cs]), 0.0)
        t = _dot((hid * hid).astype(BF16), wdown_ref[cs, :])
        mlp = t if mlp is None else mlp + t
    x2 = x1 + gate2 * mlp.reshape(tb, ts, d)
    y_ref[...] = _rms(x2, gfin_ref[...]) if final else x2


def _post(x, mod, pool, att, w, *, absorb, final, tb, ts):
    b, s, d = x.shape
    pool_w = pool.shape[-1]
    kv_lora = w["gkv"].shape[-1]
    tile = lambda width: pl.BlockSpec((tb, ts, width), lambda i, j: (i, j, 0))
    const = lambda a: pl.BlockSpec(a.shape, lambda i, j: (0,) * a.ndim, pipeline_mode=pl.Buffered(1))
    kern = functools.partial(_post_kernel, absorb=absorb, final=final, pool_w=pool_w, kv_lora=kv_lora,
                             ff_chunk=min(1024, w["wup"].shape[1]))
    return pl.pallas_call(
        kern,
        grid=(b // tb, s // ts),
        in_specs=[
            tile(d),
            pl.BlockSpec((tb, N_MOD, d), lambda i, j: (i, 0, 0)),
            tile(pool_w),
            tile(att.shape[-1]),
            const(w["wout"]), const(w["wuvp"]), const(w["wup"]), const(w["wdown"]),
            const(w["gmlp"]), const(w["gfin"]),
        ],
        out_specs=tile(d),
        out_shape=jax.ShapeDtypeStruct((b, s, d), F32),
        compiler_params=_params(("arbitrary", "arbitrary")),
        name="post_sample" if absorb else "post_prompt",
    )(x, mod, pool, att, w["wout"], w["wuvp"], w["wup"], w["wdown"], w["gmlp"], w["gfin"])


def _rot_half(wr):
    half = ROPE_DIM // 2
    return jnp.concatenate([-wr[..., half:], wr[..., :half]], axis=-1)


def _layer_weights(w_in, g_mix, g_q, w_uq, g_kv, w_uk, w_uv, w_pool, pool_scale, w_out, g_mlp,
                   w_up, w_down, g_final):
    d, _ = w_in.shape
    q_lora = g_q.shape[-1]
    kv_lora = g_kv.shape[-1]
    n_groups, gw, _ = w_pool.shape
    pool_w = n_groups * gw
    o3 = pool_w + q_lora + kv_lora
    zpad = lambda rows, cols: jnp.zeros((rows, cols), F32)

    w_kr = w_in[:, o3:]
    win = jnp.concatenate([w_in[:, :o3], w_kr, zpad(d, LANES - ROPE_DIM),
                           _rot_half(w_kr), zpad(d, LANES - ROPE_DIM)], axis=1)

    q_rope = w_uq[:, :, NOPE_DIM:]
    q_nope = w_uq[:, :, :NOPE_DIM]
    hz = jnp.zeros((q_lora, N_HEADS, HEAD_PAD - ROPE_DIM - NOPE_DIM), F32)
    wq_a = jnp.concatenate([q_rope, q_nope, hz], axis=-1).reshape(q_lora, N_HEADS * HEAD_PAD)
    wq_b = jnp.concatenate([_rot_half(q_rope), jnp.zeros((q_lora, N_HEADS, HEAD_PAD - ROPE_DIM), F32)],
                           axis=-1).reshape(q_lora, N_HEADS * HEAD_PAD)
    wq = jnp.concatenate([wq_a, wq_b], axis=1)

    kz_lo = jnp.zeros((kv_lora, N_HEADS, ROPE_DIM), F32)
    kz_hi = jnp.zeros((kv_lora, N_HEADS, HEAD_PAD - ROPE_DIM - NOPE_DIM), F32)
    wk = jnp.concatenate([kz_lo, w_uk, kz_hi], axis=-1).reshape(kv_lora, N_HEADS * HEAD_PAD)
    wkv = jnp.concatenate([wk, w_uv.reshape(kv_lora, N_HEADS * V_DIM)], axis=1)
    wabs = jnp.transpose(jnp.concatenate([kz_lo, w_uk, kz_hi], axis=-1), (1, 2, 0))

    uv = jnp.transpose(w_uv, (1, 0, 2))
    uz = jnp.zeros_like(uv)
    even = (jnp.arange(N_HEADS) % 2 == 0)[:, None, None]
    wuvp = jnp.where(even, jnp.concatenate([uv, uz], axis=-1), jnp.concatenate([uz, uv], axis=-1))

    wpool = jax.scipy.linalg.block_diag(*[w_pool[g] for g in range(n_groups)])

    bf = lambda a: a.astype(BF16)
    return dict(
        win=bf(win), wq=bf(wq), wkv=bf(wkv), wabs=bf(wabs), wuvp=bf(wuvp), wpool=bf(wpool),
        wout=bf(w_out), wup=bf(w_up), wdown=bf(w_down),
        gmix=g_mix.reshape(1, -1), gq=g_q.reshape(1, -1), gkv=g_kv.reshape(1, -1),
        pscale=pool_scale.reshape(1, -1), gmlp=g_mlp.reshape(1, -1), gfin=g_final.reshape(1, -1),
    )


def _rope_tables(positions):
    inv = ROPE_BASE ** (-jnp.arange(0, ROPE_DIM, 2, dtype=F32) / ROPE_DIM)
    ang = positions[:, None] * inv[None, :]
    cos2 = jnp.tile(jnp.cos(ang), (1, 2))
    sin2 = jnp.tile(jnp.sin(ang), (1, 2))
    n = positions.shape[0]
    qs = SOFTMAX_SCALE * LOG2E
    t_qa = jnp.concatenate([cos2, jnp.ones((n, NOPE_DIM), F32),
                            jnp.zeros((n, HEAD_PAD - ROPE_DIM - NOPE_DIM), F32)], axis=1) * qs
    zr = jnp.zeros((n, HEAD_PAD - ROPE_DIM), F32)
    t_qb = jnp.concatenate([sin2, zr], axis=1) * qs
    t_ka = jnp.concatenate([cos2, zr], axis=1)
    t_kb = jnp.concatenate([sin2, zr], axis=1)
    return jnp.concatenate([t_qa, t_qb, t_ka, t_kb], axis=1)


def _seq_tile(s, target):
    t = min(s, target)
    while s % t:
        t //= 2
    return t


def kernel(x_prompt, x_sample, cache_latent, cache_krope, state_pool, page_table, c_prompt, c_sample,
           w_mod, b_mod, g_mix, w_in, g_q, w_uq, g_kv, w_uk, w_uv, w_pool, pool_scale, w_out,
           g_mlp, w_up, w_down, g_final):
    bp, sp, d = x_prompt.shape
    db, ss, _ = x_sample.shape
    depth = w_mod.shape[0]
    n_pages = page_table.shape[1]
    page = cache_latent.shape[2]
    past = n_pages * page
    pool_w = state_pool.shape[-1]
    kv_lora = g_kv.shape[-1]

    tab_p = _rope_tables(jnp.arange(sp, dtype=F32))
    tab_s = _rope_tables(jnp.arange(ss, dtype=F32) + float(past))
    ts_p = _seq_tile(sp, 512)
    tb_s = _seq_tile(db, max(1, 256 // ss))

    xp, xs = x_prompt, x_sample
    outs = [[] for _ in range(6)]
    for l in range(depth):
        w = _layer_weights(w_in[l], g_mix[l], g_q[l], w_uq[l], g_kv[l], w_uk[l], w_uv[l], w_pool[l],
                           pool_scale[l], w_out[l], g_mlp[l], w_up[l], w_down[l], g_final)
        mod = _modulation(jnp.concatenate([c_prompt, c_sample], axis=0), w_mod[l], b_mod[l])
        mod = mod.reshape(bp + db, N_MOD, d)
        mod_p, mod_s = mod[:bp], mod[bp:]

        hist_p = jnp.zeros((bp, HIST_PAD, pool_w), F32)
        q, kcat, v, lat_p, kr_p, pool_p, nh_p = _pre(
            xp, mod_p, hist_p, tab_p, w, pos0=0.0, absorb=False, tb=1, ts=ts_p)
        att_p = _attention(q, kcat, v, tq=_seq_tile(sp, 512))
        final = l == depth - 1
        xp = _post(xp, mod_p, pool_p, att_p, w, absorb=False, final=final, tb=1, ts=ts_p)

        hist_s = jnp.pad(state_pool[l], ((0, 0), (HIST_PAD - POOL_HIST, 0), (0, 0)))
        qs, qlat, lat_s, kr_s, pool_s, nh_s = _pre(
            xs, mod_s, hist_s, tab_s, w, pos0=float(past), absorb=True, tb=tb_s, ts=ss)
        rows = ss * N_HEADS
        o_lat = _decode_attention(
            page_table,
            qlat.reshape(db, rows, kv_lora),
            qs.reshape(db, rows, HEAD_PAD),
            lat_s, jnp.swapaxes(kr_s, 1, 2),
            cache_latent[l], jnp.swapaxes(cache_krope[l], 1, 2))
        xs = _post(xs, mod_s, pool_s, o_lat.reshape(db, ss, N_HEADS * kv_lora), w,
                   absorb=True, final=final, tb=tb_s, ts=ss)

        for lst, val in zip(outs, (lat_p, jnp.swapaxes(kr_p, 1, 2), nh_p[:, HIST_PAD - POOL_HIST:],
                                   lat_s, kr_s, nh_s[:, HIST_PAD - POOL_HIST:])):
            lst.append(val)
    return (xp, xs) + tuple(jnp.stack(o) for o in outs)
```

```python
import functools
import math

import jax
import jax.numpy as jnp
from jax import lax
from jax.experimental import pallas as pl
from jax.experimental.pallas import tpu as pltpu

F32 = jnp.float32
BF16 = jnp.bfloat16

N_HEADS = 8
NOPE_DIM = 64
ROPE_DIM = 32
V_DIM = 64
POOL_WINDOWS = (2, 4, 8, 16)
POOL_HIST = max(POOL_WINDOWS) - 1
HIST_PAD = POOL_HIST + 1
N_MOD = 6
ROPE_BASE = 10000.0
EPS = 1e-6
NEG_INF = -1e30
SOFTMAX_SCALE = 1.0 / math.sqrt(NOPE_DIM + ROPE_DIM)
LOG2E = math.log2(math.e)

LANES = 128
SUBLANES = 8
VMEM_LIMIT = 56 * 1024 * 1024

HEAD_PAD = LANES


def _params(sem, vmem=VMEM_LIMIT):
    return pltpu.CompilerParams(dimension_semantics=sem, vmem_limit_bytes=vmem)


def _nt_dot(a, b):
    return lax.dot_general(a, b, (((1,), (1,)), ((), ())), preferred_element_type=F32)


def _dot(a, b):
    return jnp.dot(a, b, preferred_element_type=F32)


def _rms(x, g):
    return x * lax.rsqrt(jnp.mean(x * x, axis=-1, keepdims=True) + EPS) * g


def _mod_kernel(c_ref, w_ref, b_ref, o_ref):
    c = c_ref[...]
    sc = c * (1.0 / (1.0 + jnp.exp(-c)))
    o_ref[...] = _dot(sc.astype(BF16), w_ref[...].astype(BF16)) + b_ref[...]


def _modulation(c, w_mod, b_mod):
    n, d = c.shape
    nout = w_mod.shape[1]
    tn = d
    return pl.pallas_call(
        _mod_kernel,
        grid=(nout // tn,),
        in_specs=[
            pl.BlockSpec((n, d), lambda j: (0, 0)),
            pl.BlockSpec((d, tn), lambda j: (0, j)),
            pl.BlockSpec((1, tn), lambda j: (0, j)),
        ],
        out_specs=pl.BlockSpec((n, tn), lambda j: (0, j)),
        out_shape=jax.ShapeDtypeStruct((n, nout), F32),
        compiler_params=_params(("arbitrary",)),
        name="mod",
    )(c, w_mod, b_mod.reshape(1, nout))


def _pre_kernel(*refs, pos0, absorb, pool_w, q_lora, kv_lora, n_split):
    (x_ref, mod_ref, hist_ref, tab_ref, win_ref, wq_ref, wkv_ref, wpool_ref,
     gmix_ref, gq_ref, gkv_ref, pscale_ref) = refs[:12]
    if absorb:
        (q_ref, qlat_ref, lat_ref, kr_ref, pool_ref, nh_ref, ext_ref) = refs[12:]
    else:
        (q_ref, kcat_ref, v_ref, lat_ref, kr_ref, pool_ref, nh_ref, ext_ref) = refs[12:]
    si = pl.program_id(1)
    tb, ts, d = x_ref.shape
    hq = N_HEADS * HEAD_PAD
    shift1 = mod_ref[:, 0:1, :]
    scale1 = mod_ref[:, 1:2, :]
    o1 = pool_w
    o2 = o1 + q_lora
    o3 = o2 + kv_lora
    gw = pool_w // len(POOL_WINDOWS)

    @pl.when(si == 0)
    def _():
        ext_ref[:, 0:HIST_PAD, :] = hist_ref[...]

    def rows_block(r0, n):
        rs = slice(r0, r0 + n)
        r = tb * n
        h = _rms(x_ref[:, rs, :], gmix_ref[...]) * (1.0 + scale1) + shift1
        proj = _dot(h.reshape(r, d).astype(BF16), win_ref[...])
        u = proj[:, :o1].reshape(tb, n, pool_w)
        cq = proj[:, o1:o2]
        ckv = proj[:, o2:o3]
        kr_a = proj[:, o3:o3 + LANES].reshape(tb, n, LANES)
        kr_b = proj[:, o3 + LANES:o3 + 2 * LANES].reshape(tb, n, LANES)

        t_qa = tab_ref[rs, 0 * LANES:1 * LANES]
        t_qb = tab_ref[rs, 1 * LANES:2 * LANES]
        t_ka = tab_ref[rs, 2 * LANES:3 * LANES]
        t_kb = tab_ref[rs, 3 * LANES:4 * LANES]

        qab = _dot(_rms(cq, gq_ref[...]).astype(BF16), wq_ref[...])
        for hd in range(N_HEADS):
            lo = hd * HEAD_PAD
            qa = qab[:, lo:lo + HEAD_PAD].reshape(tb, n, HEAD_PAD)
            qb = qab[:, hq + lo:hq + lo + HEAD_PAD].reshape(tb, n, HEAD_PAD)
            qh = (qa * t_qa + qb * t_qb).astype(BF16)
            q_ref[:, rs, lo:lo + HEAD_PAD] = qh
            if absorb:
                ql = _dot(qh.reshape(r, HEAD_PAD), wkv_ref[hd])
                qlat_ref[:, rs, hd * kv_lora:(hd + 1) * kv_lora] = ql.reshape(tb, n, kv_lora).astype(BF16)

        lat = _rms(ckv, gkv_ref[...])
        lat_ref[:, rs, :] = lat.reshape(tb, n, kv_lora)
        kr128 = kr_a * t_ka + kr_b * t_kb
        if absorb:
            kr_ref[:, rs, :] = kr128[:, :, :ROPE_DIM]
        else:
            for i in range(tb):
                kr_ref[i, :, rs] = jnp.transpose(kr128[i])[:ROPE_DIM, :]
            kv = _dot(lat.astype(BF16), wkv_ref[...])
            for hd in range(N_HEADS):
                lo = hd * HEAD_PAD
                kcat_ref[:, rs, lo:lo + HEAD_PAD] = (
                    kv[:, lo:lo + HEAD_PAD].reshape(tb, n, HEAD_PAD) + kr128).astype(BF16)
            v_ref[:, rs, :] = kv[:, hq:].reshape(tb, n, N_HEADS * V_DIM).astype(BF16)

        e0 = HIST_PAD + r0
        ext_ref[:, e0:e0 + n, :] = u
        pos = (pos0 + (si * ts + r0).astype(F32)
               + lax.broadcasted_iota(jnp.int32, (1, n, gw), 1).astype(F32))
        parts = []
        for g, w in enumerate(POOL_WINDOWS):
            lo = g * gw
            acc = ext_ref[:, e0:e0 + n, lo:lo + gw]
            for i in range(1, w):
                acc = acc + ext_ref[:, e0 - i:e0 - i + n, lo:lo + gw]
            inv = 1.0 / jnp.minimum(float(w), pos + 1.0)
            parts.append(acc * inv - u[:, :, lo:lo + gw])
        dpool = jnp.concatenate(parts, axis=-1).reshape(r, pool_w).astype(BF16)
        y = _dot(dpool, wpool_ref[...]) * pscale_ref[...]
        pool_ref[:, rs, :] = y.reshape(tb, n, pool_w).astype(BF16)

    for blk in range(n_split):
        rows_block(blk * (ts // n_split), ts // n_split)

    tail = ext_ref[:, ts:ts + HIST_PAD, :]
    nh_ref[...] = tail
    ext_ref[:, 0:HIST_PAD, :] = tail


def _pre(x, mod, hist16, tab, w, *, pos0, absorb, tb, ts):
    b, s, d = x.shape
    pool_w = hist16.shape[-1]
    q_lora = w["gq"].shape[-1]
    kv_lora = w["gkv"].shape[-1]
    hq = N_HEADS * HEAD_PAD
    grid = (b // tb, s // ts)
    tile = lambda width: pl.BlockSpec((tb, ts, width), lambda i, j: (i, j, 0))
    full = lambda a: pl.BlockSpec(a.shape, lambda i, j: (0,) * a.ndim)
    wkv = w["wabs"] if absorb else w["wkv"]
    in_specs = [
        tile(d),
        pl.BlockSpec((tb, N_MOD, d), lambda i, j: (i, 0, 0)),
        pl.BlockSpec((tb, HIST_PAD, pool_w), lambda i, j: (i, 0, 0)),
        pl.BlockSpec((ts, 4 * LANES), lambda i, j: (j, 0)),
        full(w["win"]), full(w["wq"]), full(wkv), full(w["wpool"]),
        full(w["gmix"]), full(w["gq"]), full(w["gkv"]), full(w["pscale"]),
    ]
    sds = jax.ShapeDtypeStruct
    if absorb:
        out_shape = [sds((b, s, hq), BF16), sds((b, s, N_HEADS * kv_lora), BF16)]
        out_specs = [tile(hq), tile(N_HEADS * kv_lora)]
    else:
        out_shape = [sds((b, s, hq), BF16), sds((b, s, hq), BF16), sds((b, s, N_HEADS * V_DIM), BF16)]
        out_specs = [tile(hq), tile(hq), tile(N_HEADS * V_DIM)]
    if absorb:
        kr_shape, kr_spec = sds((b, s, ROPE_DIM), F32), tile(ROPE_DIM)
    else:
        kr_shape = sds((b, ROPE_DIM, s), F32)
        kr_spec = pl.BlockSpec((tb, ROPE_DIM, ts), lambda i, j: (i, 0, j))
    out_shape += [sds((b, s, kv_lora), F32), kr_shape, sds((b, s, pool_w), BF16),
                  sds((b, HIST_PAD, pool_w), F32)]
    out_specs += [tile(kv_lora), kr_spec, tile(pool_w),
                  pl.BlockSpec((tb, HIST_PAD, pool_w), lambda i, j: (i, 0, 0))]
    n_split = 2 if ts % (2 * LANES) == 0 else 1
    kern = functools.partial(_pre_kernel, pos0=float(pos0), absorb=absorb, pool_w=pool_w,
                             q_lora=q_lora, kv_lora=kv_lora, n_split=n_split)
    return pl.pallas_call(
        kern,
        grid=grid,
        in_specs=in_specs,
        out_specs=out_specs,
        out_shape=out_shape,
        scratch_shapes=[pltpu.VMEM((tb, HIST_PAD + ts, pool_w), F32)],
        compiler_params=_params(("arbitrary", "arbitrary")),
        name="pre_sample" if absorb else "pre_prompt",
    )(x, mod, hist16, tab, w["win"], w["wq"], wkv, w["wpool"], w["gmix"], w["gq"], w["gkv"], w["pscale"])


def _attn_kernel(q_ref, k_ref, v_ref, o_ref, *, tq):
    s = q_ref.shape[1]
    th = tq // 2
    first_head = lax.broadcasted_iota(jnp.int32, (tq, 2 * V_DIM), 1) < V_DIM
    rowmax = lambda a: jnp.max(a, axis=-1, keepdims=True)
    rowsum = lambda a: jnp.sum(a, axis=-1, keepdims=True)
    for qi in range(s // tq):
        q0 = qi * tq
        outs = []
        for hh in range(2):
            hs = slice(hh * HEAD_PAD, (hh + 1) * HEAD_PAD)
            s_d, m = [], []
            for j in range(2):
                nk = (j + 1) * th
                r0 = q0 + j * th
                row = lax.broadcasted_iota(jnp.int32, (th, nk), 0) + j * th
                col = lax.broadcasted_iota(jnp.int32, (th, nk), 1)
                sd = jnp.where(row >= col, _nt_dot(q_ref[0, r0:r0 + th, hs], k_ref[0, q0:q0 + nk, hs]),
                               NEG_INF)
                s_d.append(sd)
                m.append(rowmax(sd))
            if qi > 0:
                s_o = _nt_dot(q_ref[0, q0:q0 + tq, hs], k_ref[0, 0:q0, hs])
                m = [jnp.maximum(m[j], rowmax(s_o[j * th:(j + 1) * th])) for j in range(2)]
                p_o = jnp.concatenate([jnp.exp2(s_o[j * th:(j + 1) * th] - m[j]) for j in range(2)], axis=0)
                l_o = rowsum(p_o)
                acc_o = _dot(p_o.astype(BF16), v_ref[0, 0:q0, :])
            acc, l = [], []
            for j in range(2):
                p_d = jnp.exp2(s_d[j] - m[j])
                l.append(rowsum(p_d))
                acc.append(_dot(p_d.astype(BF16), v_ref[0, q0:q0 + (j + 1) * th, :]))
            acc = jnp.concatenate(acc, axis=0)
            l = jnp.concatenate(l, axis=0)
            if qi > 0:
                acc = acc + acc_o
                l = l + l_o
            outs.append(acc * (1.0 / l))
        o_ref[0, q0:q0 + tq, :] = jnp.where(first_head, outs[0], outs[1]).astype(BF16)


def _attention(q, kcat, v, *, tq):
    b, s, _ = q.shape
    pairs = N_HEADS // 2
    return pl.pallas_call(
        functools.partial(_attn_kernel, tq=tq),
        grid=(b, pairs),
        in_specs=[
            pl.BlockSpec((1, s, 2 * HEAD_PAD), lambda i, p: (i, 0, p)),
            pl.BlockSpec((1, s, 2 * HEAD_PAD), lambda i, p: (i, 0, p)),
            pl.BlockSpec((1, s, 2 * V_DIM), lambda i, p: (i, 0, p)),
        ],
        out_specs=pl.BlockSpec((1, s, 2 * V_DIM), lambda i, p: (i, 0, p)),
        out_shape=jax.ShapeDtypeStruct((b, s, N_HEADS * V_DIM), BF16),
        compiler_params=_params(("arbitrary", "arbitrary")),
        name="attn_prompt",
    )(q, kcat, v)


def _dec_kernel(pt_ref, qlat_ref, qpad_ref, latn_ref, krn_ref, clat_hbm, ckr_hbm, o_ref,
                latbuf, krbuf, latbf, s_ref, sem, *, n_pages, page, chunk, s_new):
    g = pl.program_id(0)
    ng = pl.num_programs(0)
    t_past = n_pages * page
    t_all = t_past + page
    rows = qlat_ref.shape[1]

    def page_copies(bb, sl, p):
        pg = pt_ref[bb, p]
        dst = pl.ds(p * page, page)
        return (pltpu.make_async_copy(clat_hbm.at[pg], latbuf.at[sl, dst], sem.at[0, sl]),
                pltpu.make_async_copy(ckr_hbm.at[pg], krbuf.at[sl, :, dst], sem.at[1, sl]))

    def start_pages(bb, sl, p_lo, p_hi):
        for p in range(p_lo, p_hi):
            for c in page_copies(bb, sl, p):
                c.start()

    def wait_all(bb, sl):
        for p in range(n_pages):
            for c in page_copies(bb, sl, p):
                c.wait()

    past_chunks = list(range(0, t_past, chunk))

    def attend(sl, next_b):
        latbuf[sl, t_past:t_all, :] = jnp.zeros((page, latbuf.shape[-1]), F32)
        krbuf[sl, :, t_past:t_all] = jnp.zeros((ROPE_DIM, page), F32)
        latbuf[sl, t_past:t_past + s_new, :] = latn_ref[sl]
        krbuf[sl, :, t_past:t_past + s_new] = krn_ref[sl]
        q1 = qlat_ref[sl]
        q2 = qpad_ref[sl][:, :ROPE_DIM]

        for ci, lo in enumerate(past_chunks + [t_past]):
            n = min(chunk, t_all - lo)
            lat_c = latbuf[sl, lo:lo + n, :].astype(BF16)
            latbf[lo:lo + n, :] = lat_c
            s_ref[:, lo:lo + n] = _nt_dot(q1, lat_c) + _dot(q2, krbuf[sl, :, lo:lo + n].astype(BF16))
            if ci < len(past_chunks):
                start_pages(next_b, 1 - sl, ci * n_pages // len(past_chunks),
                            (ci + 1) * n_pages // len(past_chunks))

        tok = lax.broadcasted_iota(jnp.int32, (rows, page), 0) // N_HEADS
        col = lax.broadcasted_iota(jnp.int32, (rows, page), 1)
        s_ref[:, t_past:t_all] = jnp.where(col <= tok, s_ref[:, t_past:t_all], NEG_INF)

        s_all = s_ref[...]
        m = jnp.max(s_all, axis=-1, keepdims=True)
        p = jnp.exp2(s_all - m)
        inv_l = 1.0 / jnp.sum(p, axis=-1, keepdims=True)
        p = p.astype(BF16)
        half = (t_all // 2) // page * page
        o_a = _dot(p[:, :half], latbf[0:half, :]) * inv_l
        o_b = _dot(p[:, half:], latbf[half:t_all, :]) * inv_l
        o_ref[sl] = (o_a + o_b).astype(BF16)

    b0 = g * 2
    last_b0 = (ng - 1) * 2

    @pl.when(g == 0)
    def _():
        start_pages(0, 0, 0, n_pages)

    wait_all(b0, 0)
    attend(0, b0 + 1)
    wait_all(b0 + 1, 1)
    attend(1, jnp.minimum(b0 + 2, last_b0))

    @pl.when(g == ng - 1)
    def _():
        wait_all(last_b0, 0)


def _decode_attention(page_table, qlat, qpad, lat_new, kr_new, cache_lat, cache_kr):
    db, rows, kv_lora = qlat.shape
    n_pages = page_table.shape[1]
    page = cache_lat.shape[1]
    s_new = lat_new.shape[1]
    t_past = n_pages * page
    chunk = min(2048, t_past)
    assert db % 2 == 0, "the decode kernel alternates two buffer slots within a step"
    kern = functools.partial(_dec_kernel, n_pages=n_pages, page=page, chunk=chunk, s_new=s_new)
    grid_spec = pltpu.PrefetchScalarGridSpec(
        num_scalar_prefetch=1,
        grid=(db // 2,),
        in_specs=[
            pl.BlockSpec((2, rows, kv_lora), lambda i, pt: (i, 0, 0)),
            pl.BlockSpec((2, rows, HEAD_PAD), lambda i, pt: (i, 0, 0)),
            pl.BlockSpec((2, s_new, kv_lora), lambda i, pt: (i, 0, 0)),
            pl.BlockSpec((2, ROPE_DIM, s_new), lambda i, pt: (i, 0, 0)),
            pl.BlockSpec(memory_space=pl.ANY),
            pl.BlockSpec(memory_space=pl.ANY),
        ],
        out_specs=pl.BlockSpec((2, rows, kv_lora), lambda i, pt: (i, 0, 0)),
        scratch_shapes=[
            pltpu.VMEM((2, t_past + page, kv_lora), F32),
            pltpu.VMEM((2, ROPE_DIM, t_past + page), F32),
            pltpu.VMEM((t_past + page, kv_lora), BF16),
            pltpu.VMEM((rows, t_past + page), F32),
            pltpu.SemaphoreType.DMA((2, 2)),
        ],
    )
    return pl.pallas_call(
        kern,
        grid_spec=grid_spec,
        out_shape=jax.ShapeDtypeStruct((db, rows, kv_lora), BF16),
        compiler_params=_params(("arbitrary",)),
        name="attn_sample",
    )(page_table, qlat, qpad, lat_new, kr_new, cache_lat, cache_kr)


def _post_kernel(x_ref, mod_ref, pool_ref, att_ref, wout_ref, wuv_ref, wup_ref, wdown_ref,
                 gmlp_ref, gfin_ref, y_ref, *, absorb, final, pool_w, kv_lora, ff_chunk):
    tb, ts, d = x_ref.shape
    r = tb * ts
    x = x_ref[...]
    gate1 = mod_ref[:, 2:3, :]
    shift2 = mod_ref[:, 3:4, :]
    scale2 = mod_ref[:, 4:5, :]
    gate2 = mod_ref[:, 5:6, :]

    mix = _dot(pool_ref[...].reshape(r, pool_w), wout_ref[0:pool_w, :])
    if absorb:
        for p in range(N_HEADS // 2):
            o_pair = None
            for hh in range(2):
                hd = 2 * p + hh
                o_lat = att_ref[:, :, hd * kv_lora:(hd + 1) * kv_lora].reshape(r, kv_lora)
                t = _dot(o_lat, wuv_ref[hd])
                o_pair = t if o_pair is None else o_pair + t
            lo = pool_w + p * 2 * V_DIM
            mix = mix + _dot(o_pair.astype(BF16), wout_ref[lo:lo + 2 * V_DIM, :])
    else:
        mix = mix + _dot(att_ref[...].reshape(r, N_HEADS * V_DIM), wout_ref[pool_w:, :])
    x1 = x + gate1 * mix.reshape(tb, ts, d)

    h2 = (_rms(x1, gmlp_ref[...]) * (1.0 + scale2) + shift2).reshape(r, d).astype(BF16)
    d_ff = wup_ref.shape[1]
    mlp = None
    for c in range(d_ff // ff_chunk):
        cs = slice(c * ff_chunk, (c + 1) * ff_chunk)
        hid = jnp.maximum(_dot(h2, wup_ref[:, cs]), 0.0)
        t = _dot((hid * hid).astype(BF16), wdown_ref[cs, :])
        mlp = t if mlp is None else mlp + t
    x2 = x1 + gate2 * mlp.reshape(tb, ts, d)
    y_ref[...] = _rms(x2, gfin_ref[...]) if final else x2


def _post(x, mod, pool, att, w, *, absorb, final, tb, ts):
    b, s, d = x.shape
    pool_w = pool.shape[-1]
    kv_lora = w["gkv"].shape[-1]
    tile = lambda width: pl.BlockSpec((tb, ts, width), lambda i, j: (i, j, 0))
    const = lambda a: pl.BlockSpec(a.shape, lambda i, j: (0,) * a.ndim, pipeline_mode=pl.Buffered(1))
    kern = functools.partial(_post_kernel, absorb=absorb, final=final, pool_w=pool_w, kv_lora=kv_lora,
                             ff_chunk=min(1024, w["wup"].shape[1]))
    return pl.pallas_call(
        kern,
        grid=(b // tb, s // ts),
        in_specs=[
            tile(d),
            pl.BlockSpec((tb, N_MOD, d), lambda i, j: (i, 0, 0)),
            tile(pool_w),
            tile(att.shape[-1]),
            const(w["wout"]), const(w["wuvp"]), const(w["wup"]), const(w["wdown"]),
            const(w["gmlp"]), const(w["gfin"]),
        ],
        out_specs=tile(d),
        out_shape=jax.ShapeDtypeStruct((b, s, d), F32),
        compiler_params=_params(("arbitrary", "arbitrary")),
        name="post_sample" if absorb else "post_prompt",
    )(x, mod, pool, att, w["wout"], w["wuvp"], w["wup"], w["wdown"], w["gmlp"], w["gfin"])


def _rot_half(wr):
    half = ROPE_DIM // 2
    return jnp.concatenate([-wr[..., half:], wr[..., :half]], axis=-1)


def _layer_weights(w_in, g_mix, g_q, w_uq, g_kv, w_uk, w_uv, w_pool, pool_scale, w_out, g_mlp,
                   w_up, w_down, g_final):
    d, _ = w_in.shape
    q_lora = g_q.shape[-1]
    kv_lora = g_kv.shape[-1]
    n_groups, gw, _ = w_pool.shape
    pool_w = n_groups * gw
    o3 = pool_w + q_lora + kv_lora
    zpad = lambda rows, cols: jnp.zeros((rows, cols), F32)

    w_kr = w_in[:, o3:]
    win = jnp.concatenate([w_in[:, :o3], w_kr, zpad(d, LANES - ROPE_DIM),
                           _rot_half(w_kr), zpad(d, LANES - ROPE_DIM)], axis=1)

    q_rope = w_uq[:, :, NOPE_DIM:]
    q_nope = w_uq[:, :, :NOPE_DIM]
    hz = jnp.zeros((q_lora, N_HEADS, HEAD_PAD - ROPE_DIM - NOPE_DIM), F32)
    wq_a = jnp.concatenate([q_rope, q_nope, hz], axis=-1).reshape(q_lora, N_HEADS * HEAD_PAD)
    wq_b = jnp.concatenate([_rot_half(q_rope), jnp.zeros((q_lora, N_HEADS, HEAD_PAD - ROPE_DIM), F32)],
                           axis=-1).reshape(q_lora, N_HEADS * HEAD_PAD)
    wq = jnp.concatenate([wq_a, wq_b], axis=1)

    kz_lo = jnp.zeros((kv_lora, N_HEADS, ROPE_DIM), F32)
    kz_hi = jnp.zeros((kv_lora, N_HEADS, HEAD_PAD - ROPE_DIM - NOPE_DIM), F32)
    wk = jnp.concatenate([kz_lo, w_uk, kz_hi], axis=-1).reshape(kv_lora, N_HEADS * HEAD_PAD)
    wkv = jnp.concatenate([wk, w_uv.reshape(kv_lora, N_HEADS * V_DIM)], axis=1)
    wabs = jnp.transpose(jnp.concatenate([kz_lo, w_uk, kz_hi], axis=-1), (1, 2, 0))

    uv = jnp.transpose(w_uv, (1, 0, 2))
    uz = jnp.zeros_like(uv)
    even = (jnp.arange(N_HEADS) % 2 == 0)[:, None, None]
    wuvp = jnp.where(even, jnp.concatenate([uv, uz], axis=-1), jnp.concatenate([uz, uv], axis=-1))

    wpool = jax.scipy.linalg.block_diag(*[w_pool[g] for g in range(n_groups)])

    bf = lambda a: a.astype(BF16)
    return dict(
        win=bf(win), wq=bf(wq), wkv=bf(wkv), wabs=bf(wabs), wuvp=bf(wuvp), wpool=bf(wpool),
        wout=bf(w_out), wup=bf(w_up), wdown=bf(w_down),
        gmix=g_mix.reshape(1, -1), gq=g_q.reshape(1, -1), gkv=g_kv.reshape(1, -1),
        pscale=pool_scale.reshape(1, -1), gmlp=g_mlp.reshape(1, -1), gfin=g_final.reshape(1, -1),
    )


def _rope_tables(positions):
    inv = ROPE_BASE ** (-jnp.arange(0, ROPE_DIM, 2, dtype=F32) / ROPE_DIM)
    ang = positions[:, None] * inv[None, :]
    cos2 = jnp.tile(jnp.cos(ang), (1, 2))
    sin2 = jnp.tile(jnp.sin(ang), (1, 2))
    n = positions.shape[0]
    qs = SOFTMAX_SCALE * LOG2E
    t_qa = jnp.concatenate([cos2, jnp.ones((n, NOPE_DIM), F32),
                            jnp.zeros((n, HEAD_PAD - ROPE_DIM - NOPE_DIM), F32)], axis=1) * qs
    zr = jnp.zeros((n, HEAD_PAD - ROPE_DIM), F32)
    t_qb = jnp.concatenate([sin2, zr], axis=1) * qs
    t_ka = jnp.concatenate([cos2, zr], axis=1)
    t_kb = jnp.concatenate([sin2, zr], axis=1)
    return jnp.concatenate([t_qa, t_qb, t_ka, t_kb], axis=1)


def _seq_tile(s, target):
    t = min(s, target)
    while s % t:
        t //= 2
    return t


def kernel(x_prompt, x_sample, cache_latent, cache_krope, state_pool, page_table, c_prompt, c_sample,
           w_mod, b_mod, g_mix, w_in, g_q, w_uq, g_kv, w_uk, w_uv, w_pool, pool_scale, w_out,
           g_mlp, w_up, w_down, g_final):
    bp, sp, d = x_prompt.shape
    db, ss, _ = x_sample.shape
    depth = w_mod.shape[0]
    n_pages = page_table.shape[1]
    page = cache_latent.shape[2]
    past = n_pages * page
    pool_w = state_pool.shape[-1]
    kv_lora = g_kv.shape[-1]

    tab_p = _rope_tables(jnp.arange(sp, dtype=F32))
    tab_s = _rope_tables(jnp.arange(ss, dtype=F32) + float(past))
    ts_p = _seq_tile(sp, 512)
    tb_s = _seq_tile(db, max(1, 256 // ss))

    xp, xs = x_prompt, x_sample
    outs = [[] for _ in range(6)]
    for l in range(depth):
        w = _layer_weights(w_in[l], g_mix[l], g_q[l], w_uq[l], g_kv[l], w_uk[l], w_uv[l], w_pool[l],
                           pool_scale[l], w_out[l], g_mlp[l], w_up[l], w_down[l], g_final)
        mod = _modulation(jnp.concatenate([c_prompt, c_sample], axis=0), w_mod[l], b_mod[l])
        mod = mod.reshape(bp + db, N_MOD, d)
        mod_p, mod_s = mod[:bp], mod[bp:]

        hist_p = jnp.zeros((bp, HIST_PAD, pool_w), F32)
        q, kcat, v, lat_p, kr_p, pool_p, nh_p = _pre(
            xp, mod_p, hist_p, tab_p, w, pos0=0.0, absorb=False, tb=1, ts=ts_p)
        att_p = _attention(q, kcat, v, tq=_seq_tile(sp, 512))
        final = l == depth - 1
        xp = _post(xp, mod_p, pool_p, att_p, w, absorb=False, final=final, tb=1, ts=ts_p)

        hist_s = jnp.pad(state_pool[l], ((0, 0), (HIST_PAD - POOL_HIST, 0), (0, 0)))
        qs, qlat, lat_s, kr_s, pool_s, nh_s = _pre(
            xs, mod_s, hist_s, tab_s, w, pos0=float(past), absorb=True, tb=tb_s, ts=ss)
        rows = ss * N_HEADS
        o_lat = _decode_attention(
            page_table,
            qlat.reshape(db, rows, kv_lora),
            qs.reshape(db, rows, HEAD_PAD),
            lat_s, jnp.swapaxes(kr_s, 1, 2),
            cache_latent[l], jnp.swapaxes(cache_krope[l], 1, 2))
        xs = _post(xs, mod_s, pool_s, o_lat.reshape(db, ss, N_HEADS * kv_lora), w,
                   absorb=True, final=final, tb=tb_s, ts=ss)

        for lst, val in zip(outs, (lat_p, jnp.swapaxes(kr_p, 1, 2), nh_p[:, HIST_PAD - POOL_HIST:],
                                   lat_s, kr_s, nh_s[:, HIST_PAD - POOL_HIST:])):
            lst.append(val)
    return (xp, xs) + tuple(jnp.stack(o) for o in outs)
```

```python
import functools
import math

import jax
import jax.numpy as jnp
from jax import lax
from jax.experimental import pallas as pl
from jax.experimental.pallas import tpu as pltpu

F32 = jnp.float32
BF16 = jnp.bfloat16

N_HEADS = 8
NOPE_DIM = 64
ROPE_DIM = 32
V_DIM = 64
POOL_WINDOWS = (2, 4, 8, 16)
POOL_HIST = max(POOL_WINDOWS) - 1
HIST_PAD = POOL_HIST + 1
N_MOD = 6
ROPE_BASE = 10000.0
EPS = 1e-6
NEG_INF = -1e30
SOFTMAX_SCALE = 1.0 / math.sqrt(NOPE_DIM + ROPE_DIM)
LOG2E = math.log2(math.e)

LANES = 128
SUBLANES = 8
VMEM_LIMIT = 56 * 1024 * 1024

HEAD_PAD = LANES


def _params(sem, vmem=VMEM_LIMIT):
    return pltpu.CompilerParams(dimension_semantics=sem, vmem_limit_bytes=vmem)


def _nt_dot(a, b):
    return lax.dot_general(a, b, (((1,), (1,)), ((), ())), preferred_element_type=F32)


def _dot(a, b):
    return jnp.dot(a, b, preferred_element_type=F32)


def _rms(x, g):
    return x * lax.rsqrt(jnp.mean(x * x, axis=-1, keepdims=True) + EPS) * g


def _mod_kernel(c_ref, w_ref, b_ref, o_ref):
    c = c_ref[...]
    sc = c * (1.0 / (1.0 + jnp.exp(-c)))
    o_ref[...] = _dot(sc.astype(BF16), w_ref[...].astype(BF16)) + b_ref[...]


def _modulation(c, w_mod, b_mod):
    n, d = c.shape
    nout = w_mod.shape[1]
    tn = d
    return pl.pallas_call(
        _mod_kernel,
        grid=(nout // tn,),
        in_specs=[
            pl.BlockSpec((n, d), lambda j: (0, 0)),
            pl.BlockSpec((d, tn), lambda j: (0, j)),
            pl.BlockSpec((1, tn), lambda j: (0, j)),
        ],
        out_specs=pl.BlockSpec((n, tn), lambda j: (0, j)),
        out_shape=jax.ShapeDtypeStruct((n, nout), F32),
        compiler_params=_params(("arbitrary",)),
        name="mod",
    )(c, w_mod, b_mod.reshape(1, nout))


def _pre_kernel(*refs, pos0, absorb, pool_w, q_lora, kv_lora, n_split):
    (x_ref, mod_ref, hist_ref, tab_ref, win_ref, wq_ref, wkv_ref, wpool_ref,
     gmix_ref, gq_ref, gkv_ref, pscale_ref) = refs[:12]
    if absorb:
        (q_ref, qlat_ref, lat_ref, kr_ref, pool_ref, nh_ref, ext_ref) = refs[12:]
    else:
        (q_ref, kcat_ref, v_ref, lat_ref, kr_ref, pool_ref, nh_ref, ext_ref) = refs[12:]
    si = pl.program_id(1)
    tb, ts, d = x_ref.shape
    hq = N_HEADS * HEAD_PAD
    shift1 = mod_ref[:, 0:1, :]
    scale1 = mod_ref[:, 1:2, :]
    o1 = pool_w
    o2 = o1 + q_lora
    o3 = o2 + kv_lora
    gw = pool_w // len(POOL_WINDOWS)

    @pl.when(si == 0)
    def _():
        ext_ref[:, 0:HIST_PAD, :] = hist_ref[...]

    def rows_block(r0, n):
        rs = slice(r0, r0 + n)
        r = tb * n
        h = _rms(x_ref[:, rs, :], gmix_ref[...]) * (1.0 + scale1) + shift1
        proj = _dot(h.reshape(r, d).astype(BF16), win_ref[...])
        u = proj[:, :o1].reshape(tb, n, pool_w)
        cq = proj[:, o1:o2]
        ckv = proj[:, o2:o3]
        kr_a = proj[:, o3:o3 + LANES].reshape(tb, n, LANES)
        kr_b = proj[:, o3 + LANES:o3 + 2 * LANES].reshape(tb, n, LANES)

        t_qa = tab_ref[rs, 0 * LANES:1 * LANES]
        t_qb = tab_ref[rs, 1 * LANES:2 * LANES]
        t_ka = tab_ref[rs, 2 * LANES:3 * LANES]
        t_kb = tab_ref[rs, 3 * LANES:4 * LANES]

        qab = _dot(_rms(cq, gq_ref[...]).astype(BF16), wq_ref[...])
        for hd in range(N_HEADS):
            lo = hd * HEAD_PAD
            qa = qab[:, lo:lo + HEAD_PAD].reshape(tb, n, HEAD_PAD)
            qb = qab[:, hq + lo:hq + lo + HEAD_PAD].reshape(tb, n, HEAD_PAD)
            qh = (qa * t_qa + qb * t_qb).astype(BF16)
            q_ref[:, rs, lo:lo + HEAD_PAD] = qh
            if absorb:
                ql = _dot(qh.reshape(r, HEAD_PAD), wkv_ref[hd])
                qlat_ref[:, rs, hd * kv_lora:(hd + 1) * kv_lora] = ql.reshape(tb, n, kv_lora).astype(BF16)

        lat = _rms(ckv, gkv_ref[...])
        lat_ref[:, rs, :] = lat.reshape(tb, n, kv_lora)
        kr128 = kr_a * t_ka + kr_b * t_kb
        if absorb:
            kr_ref[:, rs, :] = kr128[:, :, :ROPE_DIM]
        else:
            for i in range(tb):
                kr_ref[i, :, rs] = jnp.transpose(kr128[i])[:ROPE_DIM, :]
            kv = _dot(lat.astype(BF16), wkv_ref[...])
            for hd in range(N_HEADS):
                lo = hd * HEAD_PAD
                kcat_ref[:, rs, lo:lo + HEAD_PAD] = (
                    kv[:, lo:lo + HEAD_PAD].reshape(tb, n, HEAD_PAD) + kr128).astype(BF16)
            v_ref[:, rs, :] = kv[:, hq:].reshape(tb, n, N_HEADS * V_DIM).astype(BF16)

        e0 = HIST_PAD + r0
        ext_ref[:, e0:e0 + n, :] = u
        pos = (pos0 + (si * ts + r0).astype(F32)
               + lax.broadcasted_iota(jnp.int32, (1, n, gw), 1).astype(F32))
        parts = []
        for g, w in enumerate(POOL_WINDOWS):
            lo = g * gw
            acc = ext_ref[:, e0:e0 + n, lo:lo + gw]
            for i in range(1, w):
                acc = acc + ext_ref[:, e0 - i:e0 - i + n, lo:lo + gw]
            inv = 1.0 / jnp.minimum(float(w), pos + 1.0)
            parts.append(acc * inv - u[:, :, lo:lo + gw])
        dpool = jnp.concatenate(parts, axis=-1).reshape(r, pool_w).astype(BF16)
        y = _dot(dpool, wpool_ref[...]) * pscale_ref[...]
        pool_ref[:, rs, :] = y.reshape(tb, n, pool_w).astype(BF16)

    for blk in range(n_split):
        rows_block(blk * (ts // n_split), ts // n_split)

    tail = ext_ref[:, ts:ts + HIST_PAD, :]
    nh_ref[...] = tail
    ext_ref[:, 0:HIST_PAD, :] = tail


def _pre(x, mod, hist16, tab, w, *, pos0, absorb, tb, ts):
    b, s, d = x.shape
    pool_w = hist16.shape[-1]
    q_lora = w["gq"].shape[-1]
    kv_lora = w["gkv"].shape[-1]
    hq = N_HEADS * HEAD_PAD
    grid = (b // tb, s // ts)
    tile = lambda width: pl.BlockSpec((tb, ts, width), lambda i, j: (i, j, 0))
    full = lambda a: pl.BlockSpec(a.shape, lambda i, j: (0,) * a.ndim)
    wkv = w["wabs"] if absorb else w["wkv"]
    in_specs = [
        tile(d),
        pl.BlockSpec((tb, N_MOD, d), lambda i, j: (i, 0, 0)),
        pl.BlockSpec((tb, HIST_PAD, pool_w), lambda i, j: (i, 0, 0)),
        pl.BlockSpec((ts, 4 * LANES), lambda i, j: (j, 0)),
        full(w["win"]), full(w["wq"]), full(wkv), full(w["wpool"]),
        full(w["gmix"]), full(w["gq"]), full(w["gkv"]), full(w["pscale"]),
    ]
    sds = jax.ShapeDtypeStruct
    if absorb:
        out_shape = [sds((b, s, hq), BF16), sds((b, s, N_HEADS * kv_lora), BF16)]
        out_specs = [tile(hq), tile(N_HEADS * kv_lora)]
    else:
        out_shape = [sds((b, s, hq), BF16), sds((b, s, hq), BF16), sds((b, s, N_HEADS * V_DIM), BF16)]
        out_specs = [tile(hq), tile(hq), tile(N_HEADS * V_DIM)]
    if absorb:
        kr_shape, kr_spec = sds((b, s, ROPE_DIM), F32), tile(ROPE_DIM)
    else:
        kr_shape = sds((b, ROPE_DIM, s), F32)
        kr_spec = pl.BlockSpec((tb, ROPE_DIM, ts), lambda i, j: (i, 0, j))
    out_shape += [sds((b, s, kv_lora), F32), kr_shape, sds((b, s, pool_w), BF16),
                  sds((b, HIST_PAD, pool_w), F32)]
    out_specs += [tile(kv_lora), kr_spec, tile(pool_w),
                  pl.BlockSpec((tb, HIST_PAD, pool_w), lambda i, j: (i, 0, 0))]
    n_split = 2 if ts % (2 * LANES) == 0 else 1
    kern = functools.partial(_pre_kernel, pos0=float(pos0), absorb=absorb, pool_w=pool_w,
                             q_lora=q_lora, kv_lora=kv_lora, n_split=n_split)
    return pl.pallas_call(
        kern,
        grid=grid,
        in_specs=in_specs,
        out_specs=out_specs,
        out_shape=out_shape,
        scratch_shapes=[pltpu.VMEM((tb, HIST_PAD + ts, pool_w), F32)],
        compiler_params=_params(("arbitrary", "arbitrary")),
        name="pre_sample" if absorb else "pre_prompt",
    )(x, mod, hist16, tab, w["win"], w["wq"], wkv, w["wpool"], w["gmix"], w["gq"], w["gkv"], w["pscale"])


def _attn_kernel(q_ref, k_ref, v_ref, o_ref, *, tq):
    s = q_ref.shape[1]
    th = tq // 2
    first_head = lax.broadcasted_iota(jnp.int32, (tq, 2 * V_DIM), 1) < V_DIM
    rowmax = lambda a: jnp.max(a, axis=-1, keepdims=True)
    rowsum = lambda a: jnp.sum(a, axis=-1, keepdims=True)
    for qi in range(s // tq):
        q0 = qi * tq
        outs = []
        for hh in range(2):
            hs = slice(hh * HEAD_PAD, (hh + 1) * HEAD_PAD)
            s_d, m = [], []
            for j in range(2):
                nk = (j + 1) * th
                r0 = q0 + j * th
                row = lax.broadcasted_iota(jnp.int32, (th, nk), 0) + j * th
                col = lax.broadcasted_iota(jnp.int32, (th, nk), 1)
                sd = jnp.where(row >= col, _nt_dot(q_ref[0, r0:r0 + th, hs], k_ref[0, q0:q0 + nk, hs]),
                               NEG_INF)
                s_d.append(sd)
                m.append(rowmax(sd))
            if qi > 0:
                s_o = _nt_dot(q_ref[0, q0:q0 + tq, hs], k_ref[0, 0:q0, hs])
                m = [jnp.maximum(m[j], rowmax(s_o[j * th:(j + 1) * th])) for j in range(2)]
                p_o = jnp.concatenate([jnp.exp2(s_o[j * th:(j + 1) * th] - m[j]) for j in range(2)], axis=0)
                l_o = rowsum(p_o)
                acc_o = _dot(p_o.astype(BF16), v_ref[0, 0:q0, :])
            acc, l = [], []
            for j in range(2):
                p_d = jnp.exp2(s_d[j] - m[j])
                l.append(rowsum(p_d))
                acc.append(_dot(p_d.astype(BF16), v_ref[0, q0:q0 + (j + 1) * th, :]))
            acc = jnp.concatenate(acc, axis=0)
            l = jnp.concatenate(l, axis=0)
            if qi > 0:
                acc = acc + acc_o
                l = l + l_o
            outs.append(acc * (1.0 / l))
        o_ref[0, q0:q0 + tq, :] = jnp.where(first_head, outs[0], outs[1]).astype(BF16)


def _attention(q, kcat, v, *, tq):
    b, s, _ = q.shape
    pairs = N_HEADS // 2
    return pl.pallas_call(
        functools.partial(_attn_kernel, tq=tq),
        grid=(b, pairs),
        in_specs=[
            pl.BlockSpec((1, s, 2 * HEAD_PAD), lambda i, p: (i, 0, p)),
            pl.BlockSpec((1, s, 2 * HEAD_PAD), lambda i, p: (i, 0, p)),
            pl.BlockSpec((1, s, 2 * V_DIM), lambda i, p: (i, 0, p)),
        ],
        out_specs=pl.BlockSpec((1, s, 2 * V_DIM), lambda i, p: (i, 0, p)),
        out_shape=jax.ShapeDtypeStruct((b, s, N_HEADS * V_DIM), BF16),
        compiler_params=_params(("arbitrary", "arbitrary")),
        name="attn_prompt",
    )(q, kcat, v)


def _dec_kernel(pt_ref, qlat_ref, qpad_ref, latn_ref, krn_ref, clat_hbm, ckr_hbm, o_ref,
                latbuf, krbuf, latbf, s_ref, sem, *, n_pages, page, chunk, s_new):
    g = pl.program_id(0)
    ng = pl.num_programs(0)
    n_slots = latbuf.shape[0]
    ahead = n_slots // 2
    n_batches = ng * n_slots
    t_past = n_pages * page
    t_all = t_past + page
    rows = qlat_ref.shape[1]

    def page_copies(bb, sl, p):
        pg = pt_ref[bb, p]
        dst = pl.ds(p * page, page)
        return (pltpu.make_async_copy(clat_hbm.at[pg], latbuf.at[sl, dst], sem.at[0, sl]),
                pltpu.make_async_copy(ckr_hbm.at[pg], krbuf.at[sl, :, dst], sem.at[1, sl]))

    def start_pages(bb, sl, p_lo, p_hi):
        for p in range(p_lo, p_hi):
            for c in page_copies(bb, sl, p):
                c.start(priority=p % 2)

    def wait_all(bb, sl):
        for p in range(n_pages):
            for c in page_copies(bb, sl, p):
                c.wait()

    past_chunks = list(range(0, t_past, chunk))

    def attend(sl, next_b):
        latbuf[sl, t_past:t_all, :] = jnp.zeros((page, latbuf.shape[-1]), F32)
        krbuf[sl, :, t_past:t_all] = jnp.zeros((ROPE_DIM, page), F32)
        latbuf[sl, t_past:t_past + s_new, :] = latn_ref[sl]
        krbuf[sl, :, t_past:t_past + s_new] = krn_ref[sl]
        q1 = qlat_ref[sl]
        q2 = qpad_ref[sl][:, :ROPE_DIM]

        for ci, lo in enumerate(past_chunks + [t_past]):
            n = min(chunk, t_all - lo)
            lat_c = latbuf[sl, lo:lo + n, :].astype(BF16)
            latbf[lo:lo + n, :] = lat_c
            s_ref[:, lo:lo + n] = _nt_dot(q1, lat_c) + _dot(q2, krbuf[sl, :, lo:lo + n].astype(BF16))
            if ci < len(past_chunks):
                start_pages(next_b, (sl + ahead) % n_slots, ci * n_pages // len(past_chunks),
                            (ci + 1) * n_pages // len(past_chunks))

        tok = lax.broadcasted_iota(jnp.int32, (rows, page), 0) // N_HEADS
        col = lax.broadcasted_iota(jnp.int32, (rows, page), 1)
        s_ref[:, t_past:t_all] = jnp.where(col <= tok, s_ref[:, t_past:t_all], NEG_INF)

        s_all = s_ref[...]
        m = jnp.max(s_all, axis=-1, keepdims=True)
        p = jnp.exp2(s_all - m)
        inv_l = 1.0 / jnp.sum(p, axis=-1, keepdims=True)
        p = p.astype(BF16)
        half = (t_all // 2) // page * page
        o_a = _dot(p[:, :half], latbf[0:half, :]) * inv_l
        o_b = _dot(p[:, half:], latbf[half:t_all, :]) * inv_l
        o_ref[sl] = (o_a + o_b).astype(BF16)

    b0 = g * n_slots

    @pl.when(g == 0)
    def _():
        for j in range(ahead):
            start_pages(j, j, 0, n_pages)

    for j in range(n_slots):
        wait_all(b0 + j, j)
        attend(j, jnp.minimum(b0 + j + ahead, n_batches - 1))

    @pl.when(g == ng - 1)
    def _():
        for j in range(ahead):
            wait_all(n_batches - 1, j)


def _decode_attention(page_table, qlat, qpad, lat_new, kr_new, cache_lat, cache_kr):
    db, rows, kv_lora = qlat.shape
    n_pages = page_table.shape[1]
    page = cache_lat.shape[1]
    s_new = lat_new.shape[1]
    t_past = n_pages * page
    chunk = min(2048, t_past)
    n_slots = 4
    assert db % n_slots == 0, "the decode kernel walks a ring of buffer slots within a step"
    kern = functools.partial(_dec_kernel, n_pages=n_pages, page=page, chunk=chunk, s_new=s_new)
    grid_spec = pltpu.PrefetchScalarGridSpec(
        num_scalar_prefetch=1,
        grid=(db // n_slots,),
        in_specs=[
            pl.BlockSpec((n_slots, rows, kv_lora), lambda i, pt: (i, 0, 0)),
            pl.BlockSpec((n_slots, rows, HEAD_PAD), lambda i, pt: (i, 0, 0)),
            pl.BlockSpec((n_slots, s_new, kv_lora), lambda i, pt: (i, 0, 0)),
            pl.BlockSpec((n_slots, ROPE_DIM, s_new), lambda i, pt: (i, 0, 0)),
            pl.BlockSpec(memory_space=pl.ANY),
            pl.BlockSpec(memory_space=pl.ANY),
        ],
        out_specs=pl.BlockSpec((n_slots, rows, kv_lora), lambda i, pt: (i, 0, 0)),
        scratch_shapes=[
            pltpu.VMEM((n_slots, t_past + page, kv_lora), F32),
            pltpu.VMEM((n_slots, ROPE_DIM, t_past + page), F32),
            pltpu.VMEM((t_past + page, kv_lora), BF16),
            pltpu.VMEM((rows, t_past + page), F32),
            pltpu.SemaphoreType.DMA((2, n_slots)),
        ],
    )
    return pl.pallas_call(
        kern,
        grid_spec=grid_spec,
        out_shape=jax.ShapeDtypeStruct((db, rows, kv_lora), BF16),
        compiler_params=_params(("arbitrary",)),
        name="attn_sample",
    )(page_table, qlat, qpad, lat_new, kr_new, cache_lat, cache_kr)


def _post_kernel(x_ref, mod_ref, pool_ref, att_ref, wout_ref, wuv_ref, wup_ref, wdown_ref,
                 gmlp_ref, gfin_ref, y_ref, *, absorb, final, pool_w, kv_lora, ff_chunk):
    tb, ts, d = x_ref.shape
    r = tb * ts
    x = x_ref[...]
    gate1 = mod_ref[:, 2:3, :]
    shift2 = mod_ref[:, 3:4, :]
    scale2 = mod_ref[:, 4:5, :]
    gate2 = mod_ref[:, 5:6, :]

    mix = _dot(pool_ref[...].reshape(r, pool_w), wout_ref[0:pool_w, :])
    if absorb:
        for p in range(N_HEADS // 2):
            o_pair = None
            for hh in range(2):
                hd = 2 * p + hh
                o_lat = att_ref[:, :, hd * kv_lora:(hd + 1) * kv_lora].reshape(r, kv_lora)
                t = _dot(o_lat, wuv_ref[hd])
                o_pair = t if o_pair is None else o_pair + t
            lo = pool_w + p * 2 * V_DIM
            mix = mix + _dot(o_pair.astype(BF16), wout_ref[lo:lo + 2 * V_DIM, :])
    else:
        mix = mix + _dot(att_ref[...].reshape(r, N_HEADS * V_DIM), wout_ref[pool_w:, :])
    x1 = x + gate1 * mix.reshape(tb, ts, d)

    h2 = (_rms(x1, gmlp_ref[...]) * (1.0 + scale2) + shift2).reshape(r, d).astype(BF16)
    d_ff = wup_ref.shape[1]
    mlp = None
    for c in range(d_ff // ff_chunk):
        cs = slice(c * ff_chunk, (c + 1) * ff_chunk)
        hid = jnp.maximum(_dot(h2, wup_ref[:, cs]), 0.0)
        t = _dot((hid * hid).astype(BF16), wdown_ref[cs, :])
        mlp = t if mlp is None else mlp + t
    x2 = x1 + gate2 * mlp.reshape(tb, ts, d)
    y_ref[...] = _rms(x2, gfin_ref[...]) if final else x2


def _post(x, mod, pool, att, w, *, absorb, final, tb, ts):
    b, s, d = x.shape
    pool_w = pool.shape[-1]
    kv_lora = w["gkv"].shape[-1]
    tile = lambda width: pl.BlockSpec((tb, ts, width), lambda i, j: (i, j, 0))
    const = lambda a: pl.BlockSpec(a.shape, lambda i, j: (0,) * a.ndim, pipeline_mode=pl.Buffered(1))
    kern = functools.partial(_post_kernel, absorb=absorb, final=final, pool_w=pool_w, kv_lora=kv_lora,
                             ff_chunk=min(1024, w["wup"].shape[1]))
    return pl.pallas_call(
        kern,
        grid=(b // tb, s // ts),
        in_specs=[
            tile(d),
            pl.BlockSpec((tb, N_MOD, d), lambda i, j: (i, 0, 0)),
            tile(pool_w),
            tile(att.shape[-1]),
            const(w["wout"]), const(w["wuvp"]), const(w["wup"]), const(w["wdown"]),
            const(w["gmlp"]), const(w["gfin"]),
        ],
        out_specs=tile(d),
        out_shape=jax.ShapeDtypeStruct((b, s, d), F32),
        compiler_params=_params(("arbitrary", "arbitrary")),
        name="post_sample" if absorb else "post_prompt",
    )(x, mod, pool, att, w["wout"], w["wuvp"], w["wup"], w["wdown"], w["gmlp"], w["gfin"])


def _rot_half(wr):
    half = ROPE_DIM // 2
    return jnp.concatenate([-wr[..., half:], wr[..., :half]], axis=-1)


def _layer_weights(w_in, g_mix, g_q, w_uq, g_kv, w_uk, w_uv, w_pool, pool_scale, w_out, g_mlp,
                   w_up, w_down, g_final):
    d, _ = w_in.shape
    q_lora = g_q.shape[-1]
    kv_lora = g_kv.shape[-1]
    n_groups, gw, _ = w_pool.shape
    pool_w = n_groups * gw
    o3 = pool_w + q_lora + kv_lora
    zpad = lambda rows, cols: jnp.zeros((rows, cols), F32)

    w_kr = w_in[:, o3:]
    win = jnp.concatenate([w_in[:, :o3], w_kr, zpad(d, LANES - ROPE_DIM),
                           _rot_half(w_kr), zpad(d, LANES - ROPE_DIM)], axis=1)

    q_rope = w_uq[:, :, NOPE_DIM:]
    q_nope = w_uq[:, :, :NOPE_DIM]
    hz = jnp.zeros((q_lora, N_HEADS, HEAD_PAD - ROPE_DIM - NOPE_DIM), F32)
    wq_a = jnp.concatenate([q_rope, q_nope, hz], axis=-1).reshape(q_lora, N_HEADS * HEAD_PAD)
    wq_b = jnp.concatenate([_rot_half(q_rope), jnp.zeros((q_lora, N_HEADS, HEAD_PAD - ROPE_DIM), F32)],
                           axis=-1).reshape(q_lora, N_HEADS * HEAD_PAD)
    wq = jnp.concatenate([wq_a, wq_b], axis=1)

    kz_lo = jnp.zeros((kv_lora, N_HEADS, ROPE_DIM), F32)
    kz_hi = jnp.zeros((kv_lora, N_HEADS, HEAD_PAD - ROPE_DIM - NOPE_DIM), F32)
    wk = jnp.concatenate([kz_lo, w_uk, kz_hi], axis=-1).reshape(kv_lora, N_HEADS * HEAD_PAD)
    wkv = jnp.concatenate([wk, w_uv.reshape(kv_lora, N_HEADS * V_DIM)], axis=1)
    wabs = jnp.transpose(jnp.concatenate([kz_lo, w_uk, kz_hi], axis=-1), (1, 2, 0))

    uv = jnp.transpose(w_uv, (1, 0, 2))
    uz = jnp.zeros_like(uv)
    even = (jnp.arange(N_HEADS) % 2 == 0)[:, None, None]
    wuvp = jnp.where(even, jnp.concatenate([uv, uz], axis=-1), jnp.concatenate([uz, uv], axis=-1))

    wpool = jax.scipy.linalg.block_diag(*[w_pool[g] for g in range(n_groups)])

    bf = lambda a: a.astype(BF16)
    return dict(
        win=bf(win), wq=bf(wq), wkv=bf(wkv), wabs=bf(wabs), wuvp=bf(wuvp), wpool=bf(wpool),
        wout=bf(w_out), wup=bf(w_up), wdown=bf(w_down),
        gmix=g_mix.reshape(1, -1), gq=g_q.reshape(1, -1), gkv=g_kv.reshape(1, -1),
        pscale=pool_scale.reshape(1, -1), gmlp=g_mlp.reshape(1, -1), gfin=g_final.reshape(1, -1),
    )


def _rope_tables(positions):
    inv = ROPE_BASE ** (-jnp.arange(0, ROPE_DIM, 2, dtype=F32) / ROPE_DIM)
    ang = positions[:, None] * inv[None, :]
    cos2 = jnp.tile(jnp.cos(ang), (1, 2))
    sin2 = jnp.tile(jnp.sin(ang), (1, 2))
    n = positions.shape[0]
    qs = SOFTMAX_SCALE * LOG2E
    t_qa = jnp.concatenate([cos2, jnp.ones((n, NOPE_DIM), F32),
                            jnp.zeros((n, HEAD_PAD - ROPE_DIM - NOPE_DIM), F32)], axis=1) * qs
    zr = jnp.zeros((n, HEAD_PAD - ROPE_DIM), F32)
    t_qb = jnp.concatenate([sin2, zr], axis=1) * qs
    t_ka = jnp.concatenate([cos2, zr], axis=1)
    t_kb = jnp.concatenate([sin2, zr], axis=1)
    return jnp.concatenate([t_qa, t_qb, t_ka, t_kb], axis=1)


def _seq_tile(s, target):
    t = min(s, target)
    while s % t:
        t //= 2
    return t


def kernel(x_prompt, x_sample, cache_latent, cache_krope, state_pool, page_table, c_prompt, c_sample,
           w_mod, b_mod, g_mix, w_in, g_q, w_uq, g_kv, w_uk, w_uv, w_pool, pool_scale, w_out,
           g_mlp, w_up, w_down, g_final):
    bp, sp, d = x_prompt.shape
    db, ss, _ = x_sample.shape
    depth = w_mod.shape[0]
    n_pages = page_table.shape[1]
    page = cache_latent.shape[2]
    past = n_pages * page
    pool_w = state_pool.shape[-1]
    kv_lora = g_kv.shape[-1]

    tab_p = _rope_tables(jnp.arange(sp, dtype=F32))
    tab_s = _rope_tables(jnp.arange(ss, dtype=F32) + float(past))
    ts_p = _seq_tile(sp, 512)
    tb_s = _seq_tile(db, max(1, 256 // ss))

    xp, xs = x_prompt, x_sample
    outs = [[] for _ in range(6)]
    for l in range(depth):
        w = _layer_weights(w_in[l], g_mix[l], g_q[l], w_uq[l], g_kv[l], w_uk[l], w_uv[l], w_pool[l],
                           pool_scale[l], w_out[l], g_mlp[l], w_up[l], w_down[l], g_final)
        mod = _modulation(jnp.concatenate([c_prompt, c_sample], axis=0), w_mod[l], b_mod[l])
        mod = mod.reshape(bp + db, N_MOD, d)
        mod_p, mod_s = mod[:bp], mod[bp:]

        hist_p = jnp.zeros((bp, HIST_PAD, pool_w), F32)
        q, kcat, v, lat_p, kr_p, pool_p, nh_p = _pre(
            xp, mod_p, hist_p, tab_p, w, pos0=0.0, absorb=False, tb=1, ts=ts_p)
        att_p = _attention(q, kcat, v, tq=_seq_tile(sp, 512))
        final = l == depth - 1
        xp = _post(xp, mod_p, pool_p, att_p, w, absorb=False, final=final, tb=1, ts=ts_p)

        hist_s = jnp.pad(state_pool[l], ((0, 0), (HIST_PAD - POOL_HIST, 0), (0, 0)))
        qs, qlat, lat_s, kr_s, pool_s, nh_s = _pre(
            xs, mod_s, hist_s, tab_s, w, pos0=float(past), absorb=True, tb=tb_s, ts=ss)
        rows = ss * N_HEADS
        o_lat = _decode_attention(
            page_table,
            qlat.reshape(db, rows, kv_lora),
            qs.reshape(db, rows, HEAD_PAD),
            lat_s, jnp.swapaxes(kr_s, 1, 2),
            cache_latent[l], jnp.swapaxes(cache_krope[l], 1, 2))
        xs = _post(xs, mod_s, pool_s, o_lat.reshape(db, ss, N_HEADS * kv_lora), w,
                   absorb=True, final=final, tb=tb_s, ts=ss)

        for lst, val in zip(outs, (lat_p, jnp.swapaxes(kr_p, 1, 2), nh_p[:, HIST_PAD - POOL_HIST:],
                                   lat_s, kr_s, nh_s[:, HIST_PAD - POOL_HIST:])):
            lst.append(val)
    return (xp, xs) + tuple(jnp.stack(o) for o in outs)
```

```python
import functools
import math

import jax
import jax.numpy as jnp
from jax import lax
from jax.experimental import pallas as pl
from jax.experimental.pallas import tpu as pltpu

F32 = jnp.float32
BF16 = jnp.bfloat16

N_HEADS = 8
NOPE_DIM = 64
ROPE_DIM = 32
V_DIM = 64
POOL_WINDOWS = (2, 4, 8, 16)
POOL_HIST = max(POOL_WINDOWS) - 1
HIST_PAD = POOL_HIST + 1
N_MOD = 6
ROPE_BASE = 10000.0
EPS = 1e-6
NEG_INF = -1e30
SOFTMAX_SCALE = 1.0 / math.sqrt(NOPE_DIM + ROPE_DIM)
LOG2E = math.log2(math.e)

LANES = 128
SUBLANES = 8
VMEM_LIMIT = 56 * 1024 * 1024
VMEM_LIMIT_MERGED = 62 * 1024 * 1024

HEAD_PAD = LANES


def _params(sem, vmem=VMEM_LIMIT):
    return pltpu.CompilerParams(dimension_semantics=sem, vmem_limit_bytes=vmem)


def _nt_dot(a, b):
    return lax.dot_general(a, b, (((1,), (1,)), ((), ())), preferred_element_type=F32)


def _dot(a, b):
    return jnp.dot(a, b, preferred_element_type=F32)


def _rms(x, g):
    return x * lax.rsqrt(jnp.mean(x * x, axis=-1, keepdims=True) + EPS) * g


def _mod_kernel(c_ref, w_ref, b_ref, o_ref):
    c = c_ref[...]
    sc = c * (1.0 / (1.0 + jnp.exp(-c)))
    o_ref[...] = _dot(sc.astype(BF16), w_ref[...].astype(BF16)) + b_ref[...]


def _modulation(c, w_mod, b_mod):
    n, d = c.shape
    nout = w_mod.shape[1]
    tn = d
    return pl.pallas_call(
        _mod_kernel,
        grid=(nout // tn,),
        in_specs=[
            pl.BlockSpec((n, d), lambda j: (0, 0)),
            pl.BlockSpec((d, tn), lambda j: (0, j)),
            pl.BlockSpec((1, tn), lambda j: (0, j)),
        ],
        out_specs=pl.BlockSpec((n, tn), lambda j: (0, j)),
        out_shape=jax.ShapeDtypeStruct((n, nout), F32),
        compiler_params=_params(("arbitrary",)),
        name="mod",
    )(c, w_mod, b_mod.reshape(1, nout))


def _pre_kernel(*refs, pos0, absorb, pool_w, q_lora, kv_lora, n_split):
    (x_ref, mod_ref, hist_ref, tab_ref, win_ref, wq_ref, wkv_ref, wpool_ref,
     gmix_ref, gq_ref, gkv_ref, pscale_ref) = refs[:12]
    if absorb:
        (q_ref, qlat_ref, lat_ref, kr_ref, pool_ref, nh_ref, ext_ref) = refs[12:]
    else:
        (q_ref, kcat_ref, v_ref, lat_ref, kr_ref, pool_ref, nh_ref, ext_ref) = refs[12:]
    si = pl.program_id(1)
    tb, ts, d = x_ref.shape
    hq = N_HEADS * HEAD_PAD
    shift1 = mod_ref[:, 0:1, :]
    scale1 = mod_ref[:, 1:2, :]
    o1 = pool_w
    o2 = o1 + q_lora
    o3 = o2 + kv_lora
    gw = pool_w // len(POOL_WINDOWS)

    @pl.when(si == 0)
    def _():
        ext_ref[:, 0:HIST_PAD, :] = hist_ref[...]

    def rows_block(r0, n):
        rs = slice(r0, r0 + n)
        r = tb * n
        h = _rms(x_ref[:, rs, :], gmix_ref[...]) * (1.0 + scale1) + shift1
        proj = _dot(h.reshape(r, d).astype(BF16), win_ref[...])
        u = proj[:, :o1].reshape(tb, n, pool_w)
        cq = proj[:, o1:o2]
        ckv = proj[:, o2:o3]
        kr_a = proj[:, o3:o3 + LANES].reshape(tb, n, LANES)
        kr_b = proj[:, o3 + LANES:o3 + 2 * LANES].reshape(tb, n, LANES)

        t_qa = tab_ref[rs, 0 * LANES:1 * LANES]
        t_qb = tab_ref[rs, 1 * LANES:2 * LANES]
        t_ka = tab_ref[rs, 2 * LANES:3 * LANES]
        t_kb = tab_ref[rs, 3 * LANES:4 * LANES]

        qab = _dot(_rms(cq, gq_ref[...]).astype(BF16), wq_ref[...])
        for hd in range(N_HEADS):
            lo = hd * HEAD_PAD
            qa = qab[:, lo:lo + HEAD_PAD].reshape(tb, n, HEAD_PAD)
            qb = qab[:, hq + lo:hq + lo + HEAD_PAD].reshape(tb, n, HEAD_PAD)
            qh = (qa * t_qa + qb * t_qb).astype(BF16)
            q_ref[:, rs, lo:lo + HEAD_PAD] = qh
            if absorb:
                ql = _dot(qh.reshape(r, HEAD_PAD), wkv_ref[hd])
                qlat_ref[:, rs, hd * kv_lora:(hd + 1) * kv_lora] = ql.reshape(tb, n, kv_lora).astype(BF16)

        lat = _rms(ckv, gkv_ref[...])
        lat_ref[:, rs, :] = lat.reshape(tb, n, kv_lora)
        kr128 = kr_a * t_ka + kr_b * t_kb
        if absorb:
            kr_ref[:, rs, :] = kr128[:, :, :ROPE_DIM]
        else:
            for i in range(tb):
                kr_ref[i, :, rs] = jnp.transpose(kr128[i])[:ROPE_DIM, :]
            kv = _dot(lat.astype(BF16), wkv_ref[...])
            for hd in range(N_HEADS):
                lo = hd * HEAD_PAD
                kcat_ref[:, rs, lo:lo + HEAD_PAD] = (
                    kv[:, lo:lo + HEAD_PAD].reshape(tb, n, HEAD_PAD) + kr128).astype(BF16)
            v_ref[:, rs, :] = kv[:, hq:].reshape(tb, n, N_HEADS * V_DIM).astype(BF16)

        e0 = HIST_PAD + r0
        ext_ref[:, e0:e0 + n, :] = u
        pos = (pos0 + (si * ts + r0).astype(F32)
               + lax.broadcasted_iota(jnp.int32, (1, n, gw), 1).astype(F32))
        parts = []
        for g, w in enumerate(POOL_WINDOWS):
            lo = g * gw
            acc = ext_ref[:, e0:e0 + n, lo:lo + gw]
            for i in range(1, w):
                acc = acc + ext_ref[:, e0 - i:e0 - i + n, lo:lo + gw]
            inv = 1.0 / jnp.minimum(float(w), pos + 1.0)
            parts.append(acc * inv - u[:, :, lo:lo + gw])
        dpool = jnp.concatenate(parts, axis=-1).reshape(r, pool_w).astype(BF16)
        y = _dot(dpool, wpool_ref[...]) * pscale_ref[...]
        pool_ref[:, rs, :] = y.reshape(tb, n, pool_w).astype(BF16)

    for blk in range(n_split):
        rows_block(blk * (ts // n_split), ts // n_split)

    tail = ext_ref[:, ts:ts + HIST_PAD, :]
    nh_ref[...] = tail
    ext_ref[:, 0:HIST_PAD, :] = tail


def _pre(x, mod, hist16, tab, w, *, pos0, absorb, tb, ts):
    b, s, d = x.shape
    pool_w = hist16.shape[-1]
    q_lora = w["gq"].shape[-1]
    kv_lora = w["gkv"].shape[-1]
    hq = N_HEADS * HEAD_PAD
    grid = (b // tb, s // ts)
    tile = lambda width: pl.BlockSpec((tb, ts, width), lambda i, j: (i, j, 0))
    full = lambda a: pl.BlockSpec(a.shape, lambda i, j: (0,) * a.ndim)
    wkv = w["wabs"] if absorb else w["wkv"]
    in_specs = [
        tile(d),
        pl.BlockSpec((tb, N_MOD, d), lambda i, j: (i, 0, 0)),
        pl.BlockSpec((tb, HIST_PAD, pool_w), lambda i, j: (i, 0, 0)),
        pl.BlockSpec((ts, 4 * LANES), lambda i, j: (j, 0)),
        full(w["win"]), full(w["wq"]), full(wkv), full(w["wpool"]),
        full(w["gmix"]), full(w["gq"]), full(w["gkv"]), full(w["pscale"]),
    ]
    sds = jax.ShapeDtypeStruct
    if absorb:
        out_shape = [sds((b, s, hq), BF16), sds((b, s, N_HEADS * kv_lora), BF16)]
        out_specs = [tile(hq), tile(N_HEADS * kv_lora)]
    else:
        out_shape = [sds((b, s, hq), BF16), sds((b, s, hq), BF16), sds((b, s, N_HEADS * V_DIM), BF16)]
        out_specs = [tile(hq), tile(hq), tile(N_HEADS * V_DIM)]
    if absorb:
        kr_shape, kr_spec = sds((b, s, ROPE_DIM), F32), tile(ROPE_DIM)
    else:
        kr_shape = sds((b, ROPE_DIM, s), F32)
        kr_spec = pl.BlockSpec((tb, ROPE_DIM, ts), lambda i, j: (i, 0, j))
    out_shape += [sds((b, s, kv_lora), F32), kr_shape, sds((b, s, pool_w), BF16),
                  sds((b, HIST_PAD, pool_w), F32)]
    out_specs += [tile(kv_lora), kr_spec, tile(pool_w),
                  pl.BlockSpec((tb, HIST_PAD, pool_w), lambda i, j: (i, 0, 0))]
    n_split = 2 if ts % (2 * LANES) == 0 else 1
    kern = functools.partial(_pre_kernel, pos0=float(pos0), absorb=absorb, pool_w=pool_w,
                             q_lora=q_lora, kv_lora=kv_lora, n_split=n_split)
    return pl.pallas_call(
        kern,
        grid=grid,
        in_specs=in_specs,
        out_specs=out_specs,
        out_shape=out_shape,
        scratch_shapes=[pltpu.VMEM((tb, HIST_PAD + ts, pool_w), F32)],
        compiler_params=_params(("arbitrary", "arbitrary")),
        name="pre_sample" if absorb else "pre_prompt",
    )(x, mod, hist16, tab, w["win"], w["wq"], wkv, w["wpool"], w["gmix"], w["gq"], w["gkv"], w["pscale"])


def _attn_kernel(q_ref, k_ref, v_ref, o_ref, *, tq):
    s = q_ref.shape[1]
    th = tq // 2
    first_head = lax.broadcasted_iota(jnp.int32, (tq, 2 * V_DIM), 1) < V_DIM
    rowmax = lambda a: jnp.max(a, axis=-1, keepdims=True)
    rowsum = lambda a: jnp.sum(a, axis=-1, keepdims=True)
    for qi in range(s // tq):
        q0 = qi * tq
        outs = []
        for hh in range(2):
            hs = slice(hh * HEAD_PAD, (hh + 1) * HEAD_PAD)
            s_d, m = [], []
            for j in range(2):
                nk = (j + 1) * th
                r0 = q0 + j * th
                row = lax.broadcasted_iota(jnp.int32, (th, nk), 0) + j * th
                col = lax.broadcasted_iota(jnp.int32, (th, nk), 1)
                sd = jnp.where(row >= col, _nt_dot(q_ref[0, r0:r0 + th, hs], k_ref[0, q0:q0 + nk, hs]),
                               NEG_INF)
                s_d.append(sd)
                m.append(rowmax(sd))
            if qi > 0:
                s_o = _nt_dot(q_ref[0, q0:q0 + tq, hs], k_ref[0, 0:q0, hs])
                m = [jnp.maximum(m[j], rowmax(s_o[j * th:(j + 1) * th])) for j in range(2)]
                p_o = jnp.concatenate([jnp.exp2(s_o[j * th:(j + 1) * th] - m[j]) for j in range(2)], axis=0)
                l_o = rowsum(p_o)
                acc_o = _dot(p_o.astype(BF16), v_ref[0, 0:q0, :])
            acc, l = [], []
            for j in range(2):
                p_d = jnp.exp2(s_d[j] - m[j])
                l.append(rowsum(p_d))
                acc.append(_dot(p_d.astype(BF16), v_ref[0, q0:q0 + (j + 1) * th, :]))
            acc = jnp.concatenate(acc, axis=0)
            l = jnp.concatenate(l, axis=0)
            if qi > 0:
                acc = acc + acc_o
                l = l + l_o
            outs.append(acc * (1.0 / l))
        o_ref[0, q0:q0 + tq, :] = jnp.where(first_head, outs[0], outs[1]).astype(BF16)


def _attention(q, kcat, v, *, tq):
    b, s, _ = q.shape
    pairs = N_HEADS // 2
    return pl.pallas_call(
        functools.partial(_attn_kernel, tq=tq),
        grid=(b, pairs),
        in_specs=[
            pl.BlockSpec((1, s, 2 * HEAD_PAD), lambda i, p: (i, 0, p)),
            pl.BlockSpec((1, s, 2 * HEAD_PAD), lambda i, p: (i, 0, p)),
            pl.BlockSpec((1, s, 2 * V_DIM), lambda i, p: (i, 0, p)),
        ],
        out_specs=pl.BlockSpec((1, s, 2 * V_DIM), lambda i, p: (i, 0, p)),
        out_shape=jax.ShapeDtypeStruct((b, s, N_HEADS * V_DIM), BF16),
        compiler_params=_params(("arbitrary", "arbitrary")),
        name="attn_prompt",
    )(q, kcat, v)


def _decode_ops(pt_ref, qlat_ref, qpad_ref, latn_ref, krn_ref, clat_hbm, ckr_hbm, o_ref,
                latbuf, krbuf, s_ref, sem, *, n_pages, page, chunk, s_new):
    t_past = n_pages * page
    t_all = t_past + page
    rows = qlat_ref.shape[1]
    spans = [(lo, min(chunk, t_all - lo)) for lo in list(range(0, t_past, chunk)) + [t_past]]

    def page_copies(bb, sl, p):
        pg = pt_ref[bb, p]
        dst = pl.ds(p * page, page)
        return (pltpu.make_async_copy(clat_hbm.at[pg], latbuf.at[sl, dst], sem.at[0, sl]),
                pltpu.make_async_copy(ckr_hbm.at[pg], krbuf.at[sl, :, dst], sem.at[1, sl]))

    def start_pages(bb, sl):
        for p in range(n_pages):
            for c in page_copies(bb, sl, p):
                c.start(priority=p % 2)

    def wait_pages(bb, sl):
        for p in range(n_pages):
            for c in page_copies(bb, sl, p):
                c.wait()

    def attend_stages(sl):
        val = {}

        def new_page():
            latbuf[sl, t_past:t_all, :] = jnp.zeros((page, latbuf.shape[-1]), F32)
            krbuf[sl, :, t_past:t_all] = jnp.zeros((ROPE_DIM, page), F32)
            latbuf[sl, t_past:t_past + s_new, :] = latn_ref[sl]
            krbuf[sl, :, t_past:t_past + s_new] = krn_ref[sl]

        def scores(lo, n):
            def run():
                lat_c = latbuf[sl, lo:lo + n, :].astype(BF16)
                s_ref[:, lo:lo + n] = (_nt_dot(qlat_ref[sl], lat_c)
                                       + _dot(qpad_ref[sl][:, :ROPE_DIM], krbuf[sl, :, lo:lo + n].astype(BF16)))
            return run

        def softmax():
            tok = lax.broadcasted_iota(jnp.int32, (rows, page), 0) // N_HEADS
            col = lax.broadcasted_iota(jnp.int32, (rows, page), 1)
            s_ref[:, t_past:t_all] = jnp.where(col <= tok, s_ref[:, t_past:t_all], NEG_INF)
            s_all = s_ref[...]
            m = jnp.max(s_all, axis=-1, keepdims=True)
            p = jnp.exp2(s_all - m)
            val["inv_l"] = 1.0 / jnp.sum(p, axis=-1, keepdims=True)
            val["p"] = p.astype(BF16)
            val["out"] = None

        def values(lo, n, last):
            def run():
                t = _dot(val["p"][:, lo:lo + n], latbuf[sl, lo:lo + n, :].astype(BF16)) * val["inv_l"]
                val["out"] = t if val["out"] is None else val["out"] + t
                if last:
                    o_ref[sl] = val["out"].astype(BF16)
            return run

        return ([new_page] + [scores(lo, n) for lo, n in spans] + [softmax]
                + [values(lo, n, i == len(spans) - 1) for i, (lo, n) in enumerate(spans)])

    return start_pages, wait_pages, attend_stages


def _post_dec_kernel(pt_ref, x_ref, mod_ref, pool_ref, att_ref, wout_ref, wuv_ref, wup_ref, wdown_ref,
                     gmlp_ref, gfin_ref, qlat_ref, qpad_ref, latn_ref, krn_ref, clat_hbm, ckr_hbm,
                     y_ref, o_ref, latbuf, krbuf, s_ref, sem, *, final, pool_w, kv_lora, ff_chunk,
                     n_pages, page, chunk, s_new):
    step = pl.program_id(0) * pl.num_programs(1) + pl.program_id(1)
    n_steps = pl.num_programs(0) * pl.num_programs(1)
    n_slots = latbuf.shape[0]
    start_pages, wait_pages, attend_stages = _decode_ops(
        pt_ref, qlat_ref, qpad_ref, latn_ref, krn_ref, clat_hbm, ckr_hbm, o_ref, latbuf, krbuf, s_ref, sem,
        n_pages=n_pages, page=page, chunk=chunk, s_new=s_new)
    tile_stages = _post_stages(x_ref, mod_ref, pool_ref, att_ref, wout_ref, wuv_ref, wup_ref, wdown_ref,
                               gmlp_ref, gfin_ref, y_ref, absorb=False, final=final, pool_w=pool_w,
                               kv_lora=kv_lora, ff_chunk=ff_chunk)
    b0 = step * n_slots
    last_b0 = (n_steps - 1) * n_slots

    @pl.when(step == 0)
    def _():
        for j in range(n_slots):
            start_pages(j, j)

    attn_stages = []
    for j in range(n_slots):
        attn_stages.append(functools.partial(wait_pages, b0 + j, j))
        attn_stages += attend_stages(j)
        attn_stages.append(functools.partial(start_pages, jnp.minimum(b0 + n_slots + j, last_b0 + j), j))

    done = 0
    for i, stage in enumerate(tile_stages):
        upto = (i + 1) * len(attn_stages) // len(tile_stages)
        for a in attn_stages[done:upto]:
            a()
        done = upto
        stage()

    @pl.when(step == n_steps - 1)
    def _():
        for j in range(n_slots):
            wait_pages(last_b0 + j, j)


def _post_prompt_with_decode(x, mod, pool, att, w, page_table, qlat, qpad, lat_new, kr_new,
                             cache_lat, cache_kr, *, final, ts):
    b, s, d = x.shape
    pool_w = pool.shape[-1]
    db, rows, kv_lora = qlat.shape
    n_pages = page_table.shape[1]
    page = cache_lat.shape[1]
    s_new = lat_new.shape[1]
    t_past = n_pages * page
    nj = s // ts
    n_steps = b * nj
    assert db % n_steps == 0, "sample batches are spread evenly over the prompt tiles"
    n_slots = db // n_steps
    tile = lambda width: pl.BlockSpec((1, ts, width), lambda i, j, pt: (i, j, 0))
    const = lambda a: pl.BlockSpec(a.shape, lambda i, j, pt: (0,) * a.ndim, pipeline_mode=pl.Buffered(1))
    per_step = lambda *shape: pl.BlockSpec((n_slots,) + shape, lambda i, j, pt: (i * nj + j, 0, 0))
    kern = functools.partial(_post_dec_kernel, final=final, pool_w=pool_w, kv_lora=kv_lora,
                             ff_chunk=min(1024, w["wup"].shape[1]), n_pages=n_pages, page=page,
                             chunk=min(2048, t_past), s_new=s_new)
    grid_spec = pltpu.PrefetchScalarGridSpec(
        num_scalar_prefetch=1,
        grid=(b, nj),
        in_specs=[
            tile(d),
            pl.BlockSpec((1, N_MOD, d), lambda i, j, pt: (i, 0, 0)),
            tile(pool_w),
            tile(att.shape[-1]),
            const(w["wout"]), const(w["wuvp"]), const(w["wup"]), const(w["wdown"]),
            const(w["gmlp"]), const(w["gfin"]),
            per_step(rows, kv_lora), per_step(rows, HEAD_PAD), per_step(s_new, kv_lora),
            per_step(ROPE_DIM, s_new),
            pl.BlockSpec(memory_space=pl.ANY),
            pl.BlockSpec(memory_space=pl.ANY),
        ],
        out_specs=[tile(d), per_step(rows, kv_lora)],
        scratch_shapes=[
            pltpu.VMEM((n_slots, t_past + page, kv_lora), F32),
            pltpu.VMEM((n_slots, ROPE_DIM, t_past + page), F32),
            pltpu.VMEM((rows, t_past + page), F32),
            pltpu.SemaphoreType.DMA((2, n_slots)),
        ],
    )
    return pl.pallas_call(
        kern,
        grid_spec=grid_spec,
        out_shape=[jax.ShapeDtypeStruct((b, s, d), F32), jax.ShapeDtypeStruct((db, rows, kv_lora), BF16)],
        compiler_params=_params(("arbitrary", "arbitrary"), vmem=VMEM_LIMIT_MERGED),
        name="post_prompt_attn_sample",
    )(page_table, x, mod, pool, att, w["wout"], w["wuvp"], w["wup"], w["wdown"], w["gmlp"], w["gfin"],
      qlat, qpad, lat_new, kr_new, cache_lat, cache_kr)


def _post_kernel(x_ref, mod_ref, pool_ref, att_ref, wout_ref, wuv_ref, wup_ref, wdown_ref,
                 gmlp_ref, gfin_ref, y_ref, *, absorb, final, pool_w, kv_lora, ff_chunk):
    for stage in _post_stages(x_ref, mod_ref, pool_ref, att_ref, wout_ref, wuv_ref, wup_ref, wdown_ref,
                              gmlp_ref, gfin_ref, y_ref, absorb=absorb, final=final, pool_w=pool_w,
                              kv_lora=kv_lora, ff_chunk=ff_chunk):
        stage()


def _post_stages(x_ref, mod_ref, pool_ref, att_ref, wout_ref, wuv_ref, wup_ref, wdown_ref,
                 gmlp_ref, gfin_ref, y_ref, *, absorb, final, pool_w, kv_lora, ff_chunk):
    tb, ts, d = x_ref.shape
    r = tb * ts
    d_ff = wup_ref.shape[1]
    n_chunks = d_ff // ff_chunk
    val = {}

    def project():
        mix = _dot(pool_ref[...].reshape(r, pool_w), wout_ref[0:pool_w, :])
        if absorb:
            for p in range(N_HEADS // 2):
                o_pair = None
                for hh in range(2):
                    hd = 2 * p + hh
                    o_lat = att_ref[:, :, hd * kv_lora:(hd + 1) * kv_lora].reshape(r, kv_lora)
                    t = _dot(o_lat, wuv_ref[hd])
                    o_pair = t if o_pair is None else o_pair + t
                lo = pool_w + p * 2 * V_DIM
                mix = mix + _dot(o_pair.astype(BF16), wout_ref[lo:lo + 2 * V_DIM, :])
        else:
            mix = mix + _dot(att_ref[...].reshape(r, N_HEADS * V_DIM), wout_ref[pool_w:, :])
        x1 = x_ref[...] + mod_ref[:, 2:3, :] * mix.reshape(tb, ts, d)
        val["x1"] = x1
        val["h2"] = (_rms(x1, gmlp_ref[...]) * (1.0 + mod_ref[:, 4:5, :]) + mod_ref[:, 3:4, :]
                     ).reshape(r, d).astype(BF16)
        val["mlp"] = None

    def mlp_up(c):
        def run():
            hid = jnp.maximum(_dot(val["h2"], wup_ref[:, c * ff_chunk:(c + 1) * ff_chunk]), 0.0)
            val["hid"] = (hid * hid).astype(BF16)
        return run

    def mlp_down(c):
        def run():
            t = _dot(val["hid"], wdown_ref[c * ff_chunk:(c + 1) * ff_chunk, :])
            val["mlp"] = t if val["mlp"] is None else val["mlp"] + t
            if c == n_chunks - 1:
                x2 = val["x1"] + mod_ref[:, 5:6, :] * val["mlp"].reshape(tb, ts, d)
                y_ref[...] = _rms(x2, gfin_ref[...]) if final else x2
        return run

    stages = [project]
    for c in range(n_chunks):
        stages += [mlp_up(c), mlp_down(c)]
    return stages


def _post(x, mod, pool, att, w, *, absorb, final, tb, ts):
    b, s, d = x.shape
    pool_w = pool.shape[-1]
    kv_lora = w["gkv"].shape[-1]
    tile = lambda width: pl.BlockSpec((tb, ts, width), lambda i, j: (i, j, 0))
    const = lambda a: pl.BlockSpec(a.shape, lambda i, j: (0,) * a.ndim, pipeline_mode=pl.Buffered(1))
    kern = functools.partial(_post_kernel, absorb=absorb, final=final, pool_w=pool_w, kv_lora=kv_lora,
                             ff_chunk=min(1024, w["wup"].shape[1]))
    return pl.pallas_call(
        kern,
        grid=(b // tb, s // ts),
        in_specs=[
            tile(d),
            pl.BlockSpec((tb, N_MOD, d), lambda i, j: (i, 0, 0)),
            tile(pool_w),
            tile(att.shape[-1]),
            const(w["wout"]), const(w["wuvp"]), const(w["wup"]), const(w["wdown"]),
            const(w["gmlp"]), const(w["gfin"]),
        ],
        out_specs=tile(d),
        out_shape=jax.ShapeDtypeStruct((b, s, d), F32),
        compiler_params=_params(("arbitrary", "arbitrary")),
        name="post_sample" if absorb else "post_prompt",
    )(x, mod, pool, att, w["wout"], w["wuvp"], w["wup"], w["wdown"], w["gmlp"], w["gfin"])


def _rot_half(wr):
    half = ROPE_DIM // 2
    return jnp.concatenate([-wr[..., half:], wr[..., :half]], axis=-1)


def _layer_weights(w_in, g_mix, g_q, w_uq, g_kv, w_uk, w_uv, w_pool, pool_scale, w_out, g_mlp,
                   w_up, w_down, g_final):
    d, _ = w_in.shape
    q_lora = g_q.shape[-1]
    kv_lora = g_kv.shape[-1]
    n_groups, gw, _ = w_pool.shape
    pool_w = n_groups * gw
    o3 = pool_w + q_lora + kv_lora
    zpad = lambda rows, cols: jnp.zeros((rows, cols), F32)

    w_kr = w_in[:, o3:]
    win = jnp.concatenate([w_in[:, :o3], w_kr, zpad(d, LANES - ROPE_DIM),
                           _rot_half(w_kr), zpad(d, LANES - ROPE_DIM)], axis=1)

    q_rope = w_uq[:, :, NOPE_DIM:]
    q_nope = w_uq[:, :, :NOPE_DIM]
    hz = jnp.zeros((q_lora, N_HEADS, HEAD_PAD - ROPE_DIM - NOPE_DIM), F32)
    wq_a = jnp.concatenate([q_rope, q_nope, hz], axis=-1).reshape(q_lora, N_HEADS * HEAD_PAD)
    wq_b = jnp.concatenate([_rot_half(q_rope), jnp.zeros((q_lora, N_HEADS, HEAD_PAD - ROPE_DIM), F32)],
                           axis=-1).reshape(q_lora, N_HEADS * HEAD_PAD)
    wq = jnp.concatenate([wq_a, wq_b], axis=1)

    kz_lo = jnp.zeros((kv_lora, N_HEADS, ROPE_DIM), F32)
    kz_hi = jnp.zeros((kv_lora, N_HEADS, HEAD_PAD - ROPE_DIM - NOPE_DIM), F32)
    wk = jnp.concatenate([kz_lo, w_uk, kz_hi], axis=-1).reshape(kv_lora, N_HEADS * HEAD_PAD)
    wkv = jnp.concatenate([wk, w_uv.reshape(kv_lora, N_HEADS * V_DIM)], axis=1)
    wabs = jnp.transpose(jnp.concatenate([kz_lo, w_uk, kz_hi], axis=-1), (1, 2, 0))

    uv = jnp.transpose(w_uv, (1, 0, 2))
    uz = jnp.zeros_like(uv)
    even = (jnp.arange(N_HEADS) % 2 == 0)[:, None, None]
    wuvp = jnp.where(even, jnp.concatenate([uv, uz], axis=-1), jnp.concatenate([uz, uv], axis=-1))

    wpool = jax.scipy.linalg.block_diag(*[w_pool[g] for g in range(n_groups)])

    bf = lambda a: a.astype(BF16)
    return dict(
        win=bf(win), wq=bf(wq), wkv=bf(wkv), wabs=bf(wabs), wuvp=bf(wuvp), wpool=bf(wpool),
        wout=bf(w_out), wup=bf(w_up), wdown=bf(w_down),
        gmix=g_mix.reshape(1, -1), gq=g_q.reshape(1, -1), gkv=g_kv.reshape(1, -1),
        pscale=pool_scale.reshape(1, -1), gmlp=g_mlp.reshape(1, -1), gfin=g_final.reshape(1, -1),
    )


def _rope_tables(positions):
    inv = ROPE_BASE ** (-jnp.arange(0, ROPE_DIM, 2, dtype=F32) / ROPE_DIM)
    ang = positions[:, None] * inv[None, :]
    cos2 = jnp.tile(jnp.cos(ang), (1, 2))
    sin2 = jnp.tile(jnp.sin(ang), (1, 2))
    n = positions.shape[0]
    qs = SOFTMAX_SCALE * LOG2E
    t_qa = jnp.concatenate([cos2, jnp.ones((n, NOPE_DIM), F32),
                            jnp.zeros((n, HEAD_PAD - ROPE_DIM - NOPE_DIM), F32)], axis=1) * qs
    zr = jnp.zeros((n, HEAD_PAD - ROPE_DIM), F32)
    t_qb = jnp.concatenate([sin2, zr], axis=1) * qs
    t_ka = jnp.concatenate([cos2, zr], axis=1)
    t_kb = jnp.concatenate([sin2, zr], axis=1)
    return jnp.concatenate([t_qa, t_qb, t_ka, t_kb], axis=1)


def _seq_tile(s, target):
    t = min(s, target)
    while s % t:
        t //= 2
    return t


def kernel(x_prompt, x_sample, cache_latent, cache_krope, state_pool, page_table, c_prompt, c_sample,
           w_mod, b_mod, g_mix, w_in, g_q, w_uq, g_kv, w_uk, w_uv, w_pool, pool_scale, w_out,
           g_mlp, w_up, w_down, g_final):
    bp, sp, d = x_prompt.shape
    db, ss, _ = x_sample.shape
    depth = w_mod.shape[0]
    n_pages = page_table.shape[1]
    page = cache_latent.shape[2]
    past = n_pages * page
    pool_w = state_pool.shape[-1]
    kv_lora = g_kv.shape[-1]

    tab_p = _rope_tables(jnp.arange(sp, dtype=F32))
    tab_s = _rope_tables(jnp.arange(ss, dtype=F32) + float(past))
    ts_p = _seq_tile(sp, 512)
    tb_s = _seq_tile(db, max(1, 256 // ss))

    xp, xs = x_prompt, x_sample
    outs = [[] for _ in range(6)]
    for l in range(depth):
        w = _layer_weights(w_in[l], g_mix[l], g_q[l], w_uq[l], g_kv[l], w_uk[l], w_uv[l], w_pool[l],
                           pool_scale[l], w_out[l], g_mlp[l], w_up[l], w_down[l], g_final)
        mod = _modulation(jnp.concatenate([c_prompt, c_sample], axis=0), w_mod[l], b_mod[l])
        mod = mod.reshape(bp + db, N_MOD, d)
        mod_p, mod_s = mod[:bp], mod[bp:]

        hist_p = jnp.zeros((bp, HIST_PAD, pool_w), F32)
        q, kcat, v, lat_p, kr_p, pool_p, nh_p = _pre(
            xp, mod_p, hist_p, tab_p, w, pos0=0.0, absorb=False, tb=1, ts=ts_p)
        att_p = _attention(q, kcat, v, tq=_seq_tile(sp, 512))
        final = l == depth - 1

        hist_s = jnp.pad(state_pool[l], ((0, 0), (HIST_PAD - POOL_HIST, 0), (0, 0)))
        qs, qlat, lat_s, kr_s, pool_s, nh_s = _pre(
            xs, mod_s, hist_s, tab_s, w, pos0=float(past), absorb=True, tb=tb_s, ts=ss)
        rows = ss * N_HEADS
        xp, o_lat = _post_prompt_with_decode(
            xp, mod_p, pool_p, att_p, w, page_table,
            qlat.reshape(db, rows, kv_lora),
            qs.reshape(db, rows, HEAD_PAD),
            lat_s, jnp.swapaxes(kr_s, 1, 2),
            cache_latent[l], jnp.swapaxes(cache_krope[l], 1, 2), final=final, ts=ts_p)
        xs = _post(xs, mod_s, pool_s, o_lat.reshape(db, ss, N_HEADS * kv_lora), w,
                   absorb=True, final=final, tb=tb_s, ts=ss)

        for lst, val in zip(outs, (lat_p, jnp.swapaxes(kr_p, 1, 2), nh_p[:, HIST_PAD - POOL_HIST:],
                                   lat_s, kr_s, nh_s[:, HIST_PAD - POOL_HIST:])):
            lst.append(val)
    return (xp, xs) + tuple(jnp.stack(o) for o in outs)
```

```python
import functools
import math

import jax
import jax.numpy as jnp
from jax import lax
from jax.experimental import pallas as pl
from jax.experimental.pallas import tpu as pltpu

F32 = jnp.float32
BF16 = jnp.bfloat16

N_HEADS = 8
NOPE_DIM = 64
ROPE_DIM = 32
V_DIM = 64
POOL_WINDOWS = (2, 4, 8, 16)
POOL_HIST = max(POOL_WINDOWS) - 1
HIST_PAD = POOL_HIST + 1
N_MOD = 6
ROPE_BASE = 10000.0
EPS = 1e-6
NEG_INF = -1e30
SOFTMAX_SCALE = 1.0 / math.sqrt(NOPE_DIM + ROPE_DIM)
LOG2E = math.log2(math.e)

LANES = 128
SUBLANES = 8
VMEM_LIMIT = 56 * 1024 * 1024
VMEM_LIMIT_MERGED = 62 * 1024 * 1024

HEAD_PAD = LANES


def _params(sem, vmem=VMEM_LIMIT):
    return pltpu.CompilerParams(dimension_semantics=sem, vmem_limit_bytes=vmem)


def _nt_dot(a, b):
    return lax.dot_general(a, b, (((1,), (1,)), ((), ())), preferred_element_type=F32)


def _dot(a, b):
    return jnp.dot(a, b, preferred_element_type=F32)


def _rms(x, g):
    return x * lax.rsqrt(jnp.mean(x * x, axis=-1, keepdims=True) + EPS) * g


def _mod_kernel(c_ref, w_ref, b_ref, o_ref):
    c = c_ref[...]
    sc = c * (1.0 / (1.0 + jnp.exp(-c)))
    o_ref[...] = _dot(sc.astype(BF16), w_ref[...].astype(BF16)) + b_ref[...]


def _modulation(c, w_mod, b_mod):
    n, d = c.shape
    nout = w_mod.shape[1]
    tn = d
    return pl.pallas_call(
        _mod_kernel,
        grid=(nout // tn,),
        in_specs=[
            pl.BlockSpec((n, d), lambda j: (0, 0)),
            pl.BlockSpec((d, tn), lambda j: (0, j)),
            pl.BlockSpec((1, tn), lambda j: (0, j)),
        ],
        out_specs=pl.BlockSpec((n, tn), lambda j: (0, j)),
        out_shape=jax.ShapeDtypeStruct((n, nout), F32),
        compiler_params=_params(("arbitrary",)),
        name="mod",
    )(c, w_mod, b_mod.reshape(1, nout))


def _pre_kernel(*refs, pos0, absorb, pool_w, q_lora, kv_lora, n_split):
    (x_ref, mod_ref, hist_ref, tab_ref, win_ref, wq_ref, wkv_ref, wpool_ref,
     gmix_ref, gq_ref, gkv_ref, pscale_ref) = refs[:12]
    if absorb:
        (q_ref, qlat_ref, lat_ref, kr_ref, pool_ref, nh_ref, ext_ref) = refs[12:]
    else:
        (q_ref, kcat_ref, v_ref, lat_ref, kr_ref, pool_ref, nh_ref, ext_ref) = refs[12:]
    si = pl.program_id(1)
    tb, ts, d = x_ref.shape
    hq = N_HEADS * HEAD_PAD
    shift1 = mod_ref[:, 0:1, :]
    scale1 = mod_ref[:, 1:2, :]
    o1 = pool_w
    o2 = o1 + q_lora
    o3 = o2 + kv_lora
    gw = pool_w // len(POOL_WINDOWS)

    @pl.when(si == 0)
    def _():
        ext_ref[:, 0:HIST_PAD, :] = hist_ref[...]

    def rows_block(r0, n):
        rs = slice(r0, r0 + n)
        r = tb * n
        e0 = HIST_PAD + r0
        val = {}

        def project():
            h = _rms(x_ref[:, rs, :], gmix_ref[...]) * (1.0 + scale1) + shift1
            proj = _dot(h.reshape(r, d).astype(BF16), win_ref[...])
            val["u"] = proj[:, :o1].reshape(tb, n, pool_w)
            ext_ref[:, e0:e0 + n, :] = val["u"]
            val["cq"] = proj[:, o1:o2]
            val["ckv"] = proj[:, o2:o3]
            val["kr_a"] = proj[:, o3:o3 + LANES].reshape(tb, n, LANES)
            val["kr_b"] = proj[:, o3 + LANES:o3 + 2 * LANES].reshape(tb, n, LANES)

        def queries():
            t_qa = tab_ref[rs, 0 * LANES:1 * LANES]
            t_qb = tab_ref[rs, 1 * LANES:2 * LANES]
            qab = _dot(_rms(val["cq"], gq_ref[...]).astype(BF16), wq_ref[...])
            for hd in range(N_HEADS):
                lo = hd * HEAD_PAD
                qa = qab[:, lo:lo + HEAD_PAD].reshape(tb, n, HEAD_PAD)
                qb = qab[:, hq + lo:hq + lo + HEAD_PAD].reshape(tb, n, HEAD_PAD)
                qh = (qa * t_qa + qb * t_qb).astype(BF16)
                q_ref[:, rs, lo:lo + HEAD_PAD] = qh
                if absorb:
                    ql = _dot(qh.reshape(r, HEAD_PAD), wkv_ref[hd])
                    qlat_ref[:, rs, hd * kv_lora:(hd + 1) * kv_lora] = ql.reshape(tb, n, kv_lora).astype(BF16)

        def keys():
            lat = _rms(val["ckv"], gkv_ref[...])
            lat_ref[:, rs, :] = lat.reshape(tb, n, kv_lora)
            kr128 = (val["kr_a"] * tab_ref[rs, 2 * LANES:3 * LANES]
                     + val["kr_b"] * tab_ref[rs, 3 * LANES:4 * LANES])
            if absorb:
                kr_ref[:, rs, :] = kr128[:, :, :ROPE_DIM]
            else:
                for i in range(tb):
                    kr_ref[i, :, rs] = jnp.transpose(kr128[i])[:ROPE_DIM, :]
                kv = _dot(lat.astype(BF16), wkv_ref[...])
                for hd in range(N_HEADS):
                    lo = hd * HEAD_PAD
                    kcat_ref[:, rs, lo:lo + HEAD_PAD] = (
                        kv[:, lo:lo + HEAD_PAD].reshape(tb, n, HEAD_PAD) + kr128).astype(BF16)
                v_ref[:, rs, :] = kv[:, hq:].reshape(tb, n, N_HEADS * V_DIM).astype(BF16)

        def pooling():
            pos = (pos0 + (si * ts + r0).astype(F32)
                   + lax.broadcasted_iota(jnp.int32, (1, n, gw), 1).astype(F32))
            parts = []
            for g, w in enumerate(POOL_WINDOWS):
                lo = g * gw
                acc = ext_ref[:, e0:e0 + n, lo:lo + gw]
                for i in range(1, w):
                    acc = acc + ext_ref[:, e0 - i:e0 - i + n, lo:lo + gw]
                inv = 1.0 / jnp.minimum(float(w), pos + 1.0)
                parts.append(acc * inv - val["u"][:, :, lo:lo + gw])
            dpool = jnp.concatenate(parts, axis=-1).reshape(r, pool_w).astype(BF16)
            y = _dot(dpool, wpool_ref[...]) * pscale_ref[...]
            pool_ref[:, rs, :] = y.reshape(tb, n, pool_w).astype(BF16)

        return [project, queries, keys, pooling]

    blocks = [rows_block(blk * (ts // n_split), ts // n_split) for blk in range(n_split)]
    for stage_of_blocks in zip(*blocks):
        for stage in stage_of_blocks:
            stage()

    tail = ext_ref[:, ts:ts + HIST_PAD, :]
    nh_ref[...] = tail
    ext_ref[:, 0:HIST_PAD, :] = tail


def _pre(x, mod, hist16, tab, w, *, pos0, absorb, tb, ts):
    b, s, d = x.shape
    pool_w = hist16.shape[-1]
    q_lora = w["gq"].shape[-1]
    kv_lora = w["gkv"].shape[-1]
    hq = N_HEADS * HEAD_PAD
    grid = (b // tb, s // ts)
    tile = lambda width: pl.BlockSpec((tb, ts, width), lambda i, j: (i, j, 0))
    full = lambda a: pl.BlockSpec(a.shape, lambda i, j: (0,) * a.ndim)
    wkv = w["wabs"] if absorb else w["wkv"]
    in_specs = [
        tile(d),
        pl.BlockSpec((tb, N_MOD, d), lambda i, j: (i, 0, 0)),
        pl.BlockSpec((tb, HIST_PAD, pool_w), lambda i, j: (i, 0, 0)),
        pl.BlockSpec((ts, 4 * LANES), lambda i, j: (j, 0)),
        full(w["win"]), full(w["wq"]), full(wkv), full(w["wpool"]),
        full(w["gmix"]), full(w["gq"]), full(w["gkv"]), full(w["pscale"]),
    ]
    sds = jax.ShapeDtypeStruct
    if absorb:
        out_shape = [sds((b, s, hq), BF16), sds((b, s, N_HEADS * kv_lora), BF16)]
        out_specs = [tile(hq), tile(N_HEADS * kv_lora)]
    else:
        out_shape = [sds((b, s, hq), BF16), sds((b, s, hq), BF16), sds((b, s, N_HEADS * V_DIM), BF16)]
        out_specs = [tile(hq), tile(hq), tile(N_HEADS * V_DIM)]
    if absorb:
        kr_shape, kr_spec = sds((b, s, ROPE_DIM), F32), tile(ROPE_DIM)
    else:
        kr_shape = sds((b, ROPE_DIM, s), F32)
        kr_spec = pl.BlockSpec((tb, ROPE_DIM, ts), lambda i, j: (i, 0, j))
    out_shape += [sds((b, s, kv_lora), F32), kr_shape, sds((b, s, pool_w), BF16),
                  sds((b, HIST_PAD, pool_w), F32)]
    out_specs += [tile(kv_lora), kr_spec, tile(pool_w),
                  pl.BlockSpec((tb, HIST_PAD, pool_w), lambda i, j: (i, 0, 0))]
    n_split = 2 if ts % (2 * LANES) == 0 else 1
    kern = functools.partial(_pre_kernel, pos0=float(pos0), absorb=absorb, pool_w=pool_w,
                             q_lora=q_lora, kv_lora=kv_lora, n_split=n_split)
    return pl.pallas_call(
        kern,
        grid=grid,
        in_specs=in_specs,
        out_specs=out_specs,
        out_shape=out_shape,
        scratch_shapes=[pltpu.VMEM((tb, HIST_PAD + ts, pool_w), F32)],
        compiler_params=_params(("arbitrary", "arbitrary")),
        name="pre_sample" if absorb else "pre_prompt",
    )(x, mod, hist16, tab, w["win"], w["wq"], wkv, w["wpool"], w["gmix"], w["gq"], w["gkv"], w["pscale"])


def _attn_kernel(q_ref, k_ref, v_ref, o_ref, *, tq):
    s = q_ref.shape[1]
    th = tq // 2
    first_head = lax.broadcasted_iota(jnp.int32, (tq, 2 * V_DIM), 1) < V_DIM
    rowmax = lambda a: jnp.max(a, axis=-1, keepdims=True)
    rowsum = lambda a: jnp.sum(a, axis=-1, keepdims=True)
    units = [(qi, hh) for qi in range(s // tq) for hh in range(2)]
    val = [dict() for _ in units]

    def scores(u):
        qi, hh = units[u]
        q0 = qi * tq
        hs = slice(hh * HEAD_PAD, (hh + 1) * HEAD_PAD)
        s_d = []
        for j in range(2):
            nk = (j + 1) * th
            r0 = q0 + j * th
            row = lax.broadcasted_iota(jnp.int32, (th, nk), 0) + j * th
            col = lax.broadcasted_iota(jnp.int32, (th, nk), 1)
            s_d.append(jnp.where(row >= col, _nt_dot(q_ref[0, r0:r0 + th, hs], k_ref[0, q0:q0 + nk, hs]),
                                 NEG_INF))
        val[u]["s_d"] = s_d
        val[u]["s_o"] = _nt_dot(q_ref[0, q0:q0 + tq, hs], k_ref[0, 0:q0, hs]) if qi > 0 else None

    def softmax(u):
        s_d, s_o = val[u]["s_d"], val[u]["s_o"]
        m = [rowmax(s_d[j]) for j in range(2)]
        if s_o is not None:
            m = [jnp.maximum(m[j], rowmax(s_o[j * th:(j + 1) * th])) for j in range(2)]
            p_o = jnp.concatenate([jnp.exp2(s_o[j * th:(j + 1) * th] - m[j]) for j in range(2)], axis=0)
            val[u]["l_o"] = rowsum(p_o)
            val[u]["p_o"] = p_o.astype(BF16)
        p_d = [jnp.exp2(s_d[j] - m[j]) for j in range(2)]
        val[u]["l_d"] = [rowsum(p) for p in p_d]
        val[u]["p_d"] = [p.astype(BF16) for p in p_d]

    def values(u):
        qi, hh = units[u]
        q0 = qi * tq
        acc = jnp.concatenate([_dot(val[u]["p_d"][j], v_ref[0, q0:q0 + (j + 1) * th, :]) for j in range(2)],
                              axis=0)
        l = jnp.concatenate(val[u]["l_d"], axis=0)
        if val[u]["s_o"] is not None:
            acc = acc + _dot(val[u]["p_o"], v_ref[0, 0:q0, :])
            l = l + val[u]["l_o"]
        out = acc * (1.0 / l)
        if hh == 0:
            val[u]["out"] = out
        else:
            o_ref[0, q0:q0 + tq, :] = jnp.where(first_head, val[u - 1]["out"], out).astype(BF16)

    scores(0)
    for u in range(len(units)):
        if u + 1 < len(units):
            scores(u + 1)
        softmax(u)
        values(u)


def _attention(q, kcat, v, *, tq):
    b, s, _ = q.shape
    pairs = N_HEADS // 2
    return pl.pallas_call(
        functools.partial(_attn_kernel, tq=tq),
        grid=(b, pairs),
        in_specs=[
            pl.BlockSpec((1, s, 2 * HEAD_PAD), lambda i, p: (i, 0, p)),
            pl.BlockSpec((1, s, 2 * HEAD_PAD), lambda i, p: (i, 0, p)),
            pl.BlockSpec((1, s, 2 * V_DIM), lambda i, p: (i, 0, p)),
        ],
        out_specs=pl.BlockSpec((1, s, 2 * V_DIM), lambda i, p: (i, 0, p)),
        out_shape=jax.ShapeDtypeStruct((b, s, N_HEADS * V_DIM), BF16),
        compiler_params=_params(("arbitrary", "arbitrary")),
        name="attn_prompt",
    )(q, kcat, v)


def _decode_ops(pt_ref, qlat_ref, qpad_ref, latn_ref, krn_ref, clat_hbm, ckr_hbm, o_ref,
                latbuf, krbuf, s_ref, sem, *, n_pages, page, chunk, s_new):
    t_past = n_pages * page
    t_all = t_past + page
    rows = qlat_ref.shape[1]
    spans = [(lo, min(chunk, t_all - lo)) for lo in list(range(0, t_past, chunk)) + [t_past]]

    def page_copies(bb, sl, p):
        pg = pt_ref[bb, p]
        dst = pl.ds(p * page, page)
        return (pltpu.make_async_copy(clat_hbm.at[pg], latbuf.at[sl, dst], sem.at[0, sl]),
                pltpu.make_async_copy(ckr_hbm.at[pg], krbuf.at[sl, :, dst], sem.at[1, sl]))

    def start_pages(bb, sl):
        for p in range(n_pages):
            for c in page_copies(bb, sl, p):
                c.start(priority=p % 2)

    def wait_pages(bb, sl):
        for p in range(n_pages):
            for c in page_copies(bb, sl, p):
                c.wait()

    def attend_stages(sl):
        val = {}

        def new_page():
            latbuf[sl, t_past:t_all, :] = jnp.zeros((page, latbuf.shape[-1]), F32)
            krbuf[sl, :, t_past:t_all] = jnp.zeros((ROPE_DIM, page), F32)
            latbuf[sl, t_past:t_past + s_new, :] = latn_ref[sl]
            krbuf[sl, :, t_past:t_past + s_new] = krn_ref[sl]

        def scores(lo, n):
            def run():
                lat_c = latbuf[sl, lo:lo + n, :].astype(BF16)
                s_ref[:, lo:lo + n] = (_nt_dot(qlat_ref[sl], lat_c)
                                       + _dot(qpad_ref[sl][:, :ROPE_DIM], krbuf[sl, :, lo:lo + n].astype(BF16)))
            return run

        def softmax():
            tok = lax.broadcasted_iota(jnp.int32, (rows, page), 0) // N_HEADS
            col = lax.broadcasted_iota(jnp.int32, (rows, page), 1)
            s_ref[:, t_past:t_all] = jnp.where(col <= tok, s_ref[:, t_past:t_all], NEG_INF)
            s_all = s_ref[...]
            m = jnp.max(s_all, axis=-1, keepdims=True)
            p = jnp.exp2(s_all - m)
            val["inv_l"] = 1.0 / jnp.sum(p, axis=-1, keepdims=True)
            val["p"] = p.astype(BF16)
            val["out"] = None

        def values(lo, n, last):
            def run():
                t = _dot(val["p"][:, lo:lo + n], latbuf[sl, lo:lo + n, :].astype(BF16)) * val["inv_l"]
                val["out"] = t if val["out"] is None else val["out"] + t
                if last:
                    o_ref[sl] = val["out"].astype(BF16)
            return run

        return ([new_page] + [scores(lo, n) for lo, n in spans] + [softmax]
                + [values(lo, n, i == len(spans) - 1) for i, (lo, n) in enumerate(spans)])

    return start_pages, wait_pages, attend_stages


def _post_dec_kernel(pt_ref, x_ref, mod_ref, pool_ref, att_ref, wout_ref, wuv_ref, wup_ref, wdown_ref,
                     gmlp_ref, gfin_ref, qlat_ref, qpad_ref, latn_ref, krn_ref, clat_hbm, ckr_hbm,
                     y_ref, o_ref, latbuf, krbuf, s_ref, sem, *, final, pool_w, kv_lora, ff_chunk,
                     n_pages, page, chunk, s_new):
    step = pl.program_id(0) * pl.num_programs(1) + pl.program_id(1)
    n_steps = pl.num_programs(0) * pl.num_programs(1)
    n_slots = latbuf.shape[0]
    start_pages, wait_pages, attend_stages = _decode_ops(
        pt_ref, qlat_ref, qpad_ref, latn_ref, krn_ref, clat_hbm, ckr_hbm, o_ref, latbuf, krbuf, s_ref, sem,
        n_pages=n_pages, page=page, chunk=chunk, s_new=s_new)
    tile_stages = _post_stages(x_ref, mod_ref, pool_ref, att_ref, wout_ref, wuv_ref, wup_ref, wdown_ref,
                               gmlp_ref, gfin_ref, y_ref, absorb=False, final=final, pool_w=pool_w,
                               kv_lora=kv_lora, ff_chunk=ff_chunk)
    b0 = step * n_slots
    last_b0 = (n_steps - 1) * n_slots

    @pl.when(step == 0)
    def _():
        for j in range(n_slots):
            start_pages(j, j)

    attn_stages = []
    for j in range(n_slots):
        attn_stages.append(functools.partial(wait_pages, b0 + j, j))
        attn_stages += attend_stages(j)
        attn_stages.append(functools.partial(start_pages, jnp.minimum(b0 + n_slots + j, last_b0 + j), j))

    done = 0
    for i, stage in enumerate(tile_stages):
        upto = (i + 1) * len(attn_stages) // len(tile_stages)
        for a in attn_stages[done:upto]:
            a()
        done = upto
        stage()

    @pl.when(step == n_steps - 1)
    def _():
        for j in range(n_slots):
            wait_pages(last_b0 + j, j)


def _post_prompt_with_decode(x, mod, pool, att, w, page_table, qlat, qpad, lat_new, kr_new,
                             cache_lat, cache_kr, *, final, ts):
    b, s, d = x.shape
    pool_w = pool.shape[-1]
    db, rows, kv_lora = qlat.shape
    n_pages = page_table.shape[1]
    page = cache_lat.shape[1]
    s_new = lat_new.shape[1]
    t_past = n_pages * page
    nj = s // ts
    n_steps = b * nj
    assert db % n_steps == 0, "sample batches are spread evenly over the prompt tiles"
    n_slots = db // n_steps
    tile = lambda width: pl.BlockSpec((1, ts, width), lambda i, j, pt: (i, j, 0))
    const = lambda a: pl.BlockSpec(a.shape, lambda i, j, pt: (0,) * a.ndim, pipeline_mode=pl.Buffered(1))
    per_step = lambda *shape: pl.BlockSpec((n_slots,) + shape, lambda i, j, pt: (i * nj + j, 0, 0))
    kern = functools.partial(_post_dec_kernel, final=final, pool_w=pool_w, kv_lora=kv_lora,
                             ff_chunk=min(1024, w["wup"].shape[1]), n_pages=n_pages, page=page,
                             chunk=min(2048, t_past), s_new=s_new)
    grid_spec = pltpu.PrefetchScalarGridSpec(
        num_scalar_prefetch=1,
        grid=(b, nj),
        in_specs=[
            tile(d),
            pl.BlockSpec((1, N_MOD, d), lambda i, j, pt: (i, 0, 0)),
            tile(pool_w),
            tile(att.shape[-1]),
            const(w["wout"]), const(w["wuvp"]), const(w["wup"]), const(w["wdown"]),
            const(w["gmlp"]), const(w["gfin"]),
            per_step(rows, kv_lora), per_step(rows, HEAD_PAD), per_step(s_new, kv_lora),
            per_step(ROPE_DIM, s_new),
            pl.BlockSpec(memory_space=pl.ANY),
            pl.BlockSpec(memory_space=pl.ANY),
        ],
        out_specs=[tile(d), per_step(rows, kv_lora)],
        scratch_shapes=[
            pltpu.VMEM((n_slots, t_past + page, kv_lora), F32),
            pltpu.VMEM((n_slots, ROPE_DIM, t_past + page), F32),
            pltpu.VMEM((rows, t_past + page), F32),
            pltpu.SemaphoreType.DMA((2, n_slots)),
        ],
    )
    return pl.pallas_call(
        kern,
        grid_spec=grid_spec,
        out_shape=[jax.ShapeDtypeStruct((b, s, d), F32), jax.ShapeDtypeStruct((db, rows, kv_lora), BF16)],
        compiler_params=_params(("arbitrary", "arbitrary"), vmem=VMEM_LIMIT_MERGED),
        name="post_prompt_attn_sample",
    )(page_table, x, mod, pool, att, w["wout"], w["wuvp"], w["wup"], w["wdown"], w["gmlp"], w["gfin"],
      qlat, qpad, lat_new, kr_new, cache_lat, cache_kr)


def _post_kernel(x_ref, mod_ref, pool_ref, att_ref, wout_ref, wuv_ref, wup_ref, wdown_ref,
                 gmlp_ref, gfin_ref, y_ref, *, absorb, final, pool_w, kv_lora, ff_chunk):
    for stage in _post_stages(x_ref, mod_ref, pool_ref, att_ref, wout_ref, wuv_ref, wup_ref, wdown_ref,
                              gmlp_ref, gfin_ref, y_ref, absorb=absorb, final=final, pool_w=pool_w,
                              kv_lora=kv_lora, ff_chunk=ff_chunk):
        stage()


def _post_stages(x_ref, mod_ref, pool_ref, att_ref, wout_ref, wuv_ref, wup_ref, wdown_ref,
                 gmlp_ref, gfin_ref, y_ref, *, absorb, final, pool_w, kv_lora, ff_chunk):
    tb, ts, d = x_ref.shape
    r = tb * ts
    d_ff = wup_ref.shape[1]
    n_chunks = d_ff // ff_chunk
    val = {}

    def project():
        mix = _dot(pool_ref[...].reshape(r, pool_w), wout_ref[0:pool_w, :])
        if absorb:
            for p in range(N_HEADS // 2):
                o_pair = None
                for hh in range(2):
                    hd = 2 * p + hh
                    o_lat = att_ref[:, :, hd * kv_lora:(hd + 1) * kv_lora].reshape(r, kv_lora)
                    t = _dot(o_lat, wuv_ref[hd])
                    o_pair = t if o_pair is None else o_pair + t
                lo = pool_w + p * 2 * V_DIM
                mix = mix + _dot(o_pair.astype(BF16), wout_ref[lo:lo + 2 * V_DIM, :])
        else:
            mix = mix + _dot(att_ref[...].reshape(r, N_HEADS * V_DIM), wout_ref[pool_w:, :])
        x1 = x_ref[...] + mod_ref[:, 2:3, :] * mix.reshape(tb, ts, d)
        val["x1"] = x1
        val["h2"] = (_rms(x1, gmlp_ref[...]) * (1.0 + mod_ref[:, 4:5, :]) + mod_ref[:, 3:4, :]
                     ).reshape(r, d).astype(BF16)
        val["mlp"] = None

    def mlp_up(c):
        def run():
            hid = jnp.maximum(_dot(val["h2"], wup_ref[:, c * ff_chunk:(c + 1) * ff_chunk]), 0.0)
            val["hid"] = (hid * hid).astype(BF16)
        return run

    def mlp_down(c):
        def run():
            t = _dot(val["hid"], wdown_ref[c * ff_chunk:(c + 1) * ff_chunk, :])
            val["mlp"] = t if val["mlp"] is None else val["mlp"] + t
            if c == n_chunks - 1:
                x2 = val["x1"] + mod_ref[:, 5:6, :] * val["mlp"].reshape(tb, ts, d)
                y_ref[...] = _rms(x2, gfin_ref[...]) if final else x2
        return run

    stages = [project]
    for c in range(n_chunks):
        stages += [mlp_up(c), mlp_down(c)]
    return stages


def _post(x, mod, pool, att, w, *, absorb, final, tb, ts):
    b, s, d = x.shape
    pool_w = pool.shape[-1]
    kv_lora = w["gkv"].shape[-1]
    tile = lambda width: pl.BlockSpec((tb, ts, width), lambda i, j: (i, j, 0))
    const = lambda a: pl.BlockSpec(a.shape, lambda i, j: (0,) * a.ndim, pipeline_mode=pl.Buffered(1))
    kern = functools.partial(_post_kernel, absorb=absorb, final=final, pool_w=pool_w, kv_lora=kv_lora,
                             ff_chunk=min(1024, w["wup"].shape[1]))
    return pl.pallas_call(
        kern,
        grid=(b // tb, s // ts),
        in_specs=[
            tile(d),
            pl.BlockSpec((tb, N_MOD, d), lambda i, j: (i, 0, 0)),
            tile(pool_w),
            tile(att.shape[-1]),
            const(w["wout"]), const(w["wuvp"]), const(w["wup"]), const(w["wdown"]),
            const(w["gmlp"]), const(w["gfin"]),
        ],
        out_specs=tile(d),
        out_shape=jax.ShapeDtypeStruct((b, s, d), F32),
        compiler_params=_params(("arbitrary", "arbitrary")),
        name="post_sample" if absorb else "post_prompt",
    )(x, mod, pool, att, w["wout"], w["wuvp"], w["wup"], w["wdown"], w["gmlp"], w["gfin"])


def _rot_half(wr):
    half = ROPE_DIM // 2
    return jnp.concatenate([-wr[..., half:], wr[..., :half]], axis=-1)


def _layer_weights(w_in, g_mix, g_q, w_uq, g_kv, w_uk, w_uv, w_pool, pool_scale, w_out, g_mlp,
                   w_up, w_down, g_final):
    d, _ = w_in.shape
    q_lora = g_q.shape[-1]
    kv_lora = g_kv.shape[-1]
    n_groups, gw, _ = w_pool.shape
    pool_w = n_groups * gw
    o3 = pool_w + q_lora + kv_lora
    zpad = lambda rows, cols: jnp.zeros((rows, cols), F32)

    w_kr = w_in[:, o3:]
    win = jnp.concatenate([w_in[:, :o3], w_kr, zpad(d, LANES - ROPE_DIM),
                           _rot_half(w_kr), zpad(d, LANES - ROPE_DIM)], axis=1)

    q_rope = w_uq[:, :, NOPE_DIM:]
    q_nope = w_uq[:, :, :NOPE_DIM]
    hz = jnp.zeros((q_lora, N_HEADS, HEAD_PAD - ROPE_DIM - NOPE_DIM), F32)
    wq_a = jnp.concatenate([q_rope, q_nope, hz], axis=-1).reshape(q_lora, N_HEADS * HEAD_PAD)
    wq_b = jnp.concatenate([_rot_half(q_rope), jnp.zeros((q_lora, N_HEADS, HEAD_PAD - ROPE_DIM), F32)],
                           axis=-1).reshape(q_lora, N_HEADS * HEAD_PAD)
    wq = jnp.concatenate([wq_a, wq_b], axis=1)

    kz_lo = jnp.zeros((kv_lora, N_HEADS, ROPE_DIM), F32)
    kz_hi = jnp.zeros((kv_lora, N_HEADS, HEAD_PAD - ROPE_DIM - NOPE_DIM), F32)
    wk = jnp.concatenate([kz_lo, w_uk, kz_hi], axis=-1).reshape(kv_lora, N_HEADS * HEAD_PAD)
    wkv = jnp.concatenate([wk, w_uv.reshape(kv_lora, N_HEADS * V_DIM)], axis=1)
    wabs = jnp.transpose(jnp.concatenate([kz_lo, w_uk, kz_hi], axis=-1), (1, 2, 0))

    uv = jnp.transpose(w_uv, (1, 0, 2))
    uz = jnp.zeros_like(uv)
    even = (jnp.arange(N_HEADS) % 2 == 0)[:, None, None]
    wuvp = jnp.where(even, jnp.concatenate([uv, uz], axis=-1), jnp.concatenate([uz, uv], axis=-1))

    wpool = jax.scipy.linalg.block_diag(*[w_pool[g] for g in range(n_groups)])

    bf = lambda a: a.astype(BF16)
    return dict(
        win=bf(win), wq=bf(wq), wkv=bf(wkv), wabs=bf(wabs), wuvp=bf(wuvp), wpool=bf(wpool),
        wout=bf(w_out), wup=bf(w_up), wdown=bf(w_down),
        gmix=g_mix.reshape(1, -1), gq=g_q.reshape(1, -1), gkv=g_kv.reshape(1, -1),
        pscale=pool_scale.reshape(1, -1), gmlp=g_mlp.reshape(1, -1), gfin=g_final.reshape(1, -1),
    )


def _rope_tables(positions):
    inv = ROPE_BASE ** (-jnp.arange(0, ROPE_DIM, 2, dtype=F32) / ROPE_DIM)
    ang = positions[:, None] * inv[None, :]
    cos2 = jnp.tile(jnp.cos(ang), (1, 2))
    sin2 = jnp.tile(jnp.sin(ang), (1, 2))
    n = positions.shape[0]
    qs = SOFTMAX_SCALE * LOG2E
    t_qa = jnp.concatenate([cos2, jnp.ones((n, NOPE_DIM), F32),
                            jnp.zeros((n, HEAD_PAD - ROPE_DIM - NOPE_DIM), F32)], axis=1) * qs
    zr = jnp.zeros((n, HEAD_PAD - ROPE_DIM), F32)
    t_qb = jnp.concatenate([sin2, zr], axis=1) * qs
    t_ka = jnp.concatenate([cos2, zr], axis=1)
    t_kb = jnp.concatenate([sin2, zr], axis=1)
    return jnp.concatenate([t_qa, t_qb, t_ka, t_kb], axis=1)


def _seq_tile(s, target):
    t = min(s, target)
    while s % t:
        t //= 2
    return t


def kernel(x_prompt, x_sample, cache_latent, cache_krope, state_pool, page_table, c_prompt, c_sample,
           w_mod, b_mod, g_mix, w_in, g_q, w_uq, g_kv, w_uk, w_uv, w_pool, pool_scale, w_out,
           g_mlp, w_up, w_down, g_final):
    bp, sp, d = x_prompt.shape
    db, ss, _ = x_sample.shape
    depth = w_mod.shape[0]
    n_pages = page_table.shape[1]
    page = cache_latent.shape[2]
    past = n_pages * page
    pool_w = state_pool.shape[-1]
    kv_lora = g_kv.shape[-1]

    tab_p = _rope_tables(jnp.arange(sp, dtype=F32))
    tab_s = _rope_tables(jnp.arange(ss, dtype=F32) + float(past))
    ts_p = _seq_tile(sp, 512)
    tb_s = _seq_tile(db, max(1, 256 // ss))

    xp, xs = x_prompt, x_sample
    outs = [[] for _ in range(6)]
    for l in range(depth):
        w = _layer_weights(w_in[l], g_mix[l], g_q[l], w_uq[l], g_kv[l], w_uk[l], w_uv[l], w_pool[l],
                           pool_scale[l], w_out[l], g_mlp[l], w_up[l], w_down[l], g_final)
        mod = _modulation(jnp.concatenate([c_prompt, c_sample], axis=0), w_mod[l], b_mod[l])
        mod = mod.reshape(bp + db, N_MOD, d)
        mod_p, mod_s = mod[:bp], mod[bp:]

        hist_p = jnp.zeros((bp, HIST_PAD, pool_w), F32)
        q, kcat, v, lat_p, kr_p, pool_p, nh_p = _pre(
            xp, mod_p, hist_p, tab_p, w, pos0=0.0, absorb=False, tb=1, ts=ts_p)
        att_p = _attention(q, kcat, v, tq=_seq_tile(sp, 512))
        final = l == depth - 1

        hist_s = jnp.pad(state_pool[l], ((0, 0), (HIST_PAD - POOL_HIST, 0), (0, 0)))
        qs, qlat, lat_s, kr_s, pool_s, nh_s = _pre(
            xs, mod_s, hist_s, tab_s, w, pos0=float(past), absorb=True, tb=tb_s, ts=ss)
        rows = ss * N_HEADS
        xp, o_lat = _post_prompt_with_decode(
            xp, mod_p, pool_p, att_p, w, page_table,
            qlat.reshape(db, rows, kv_lora),
            qs.reshape(db, rows, HEAD_PAD),
            lat_s, jnp.swapaxes(kr_s, 1, 2),
            cache_latent[l], jnp.swapaxes(cache_krope[l], 1, 2), final=final, ts=ts_p)
        xs = _post(xs, mod_s, pool_s, o_lat.reshape(db, ss, N_HEADS * kv_lora), w,
                   absorb=True, final=final, tb=tb_s, ts=ss)

        for lst, val in zip(outs, (lat_p, jnp.swapaxes(kr_p, 1, 2), nh_p[:, HIST_PAD - POOL_HIST:],
                                   lat_s, kr_s, nh_s[:, HIST_PAD - POOL_HIST:])):
            lst.append(val)
    return (xp, xs) + tuple(jnp.stack(o) for o in outs)
```

```python
import functools
import math

import jax
import jax.numpy as jnp
from jax import lax
from jax.experimental import pallas as pl
from jax.experimental.pallas import tpu as pltpu

F32 = jnp.float32
BF16 = jnp.bfloat16

N_HEADS = 8
NOPE_DIM = 64
ROPE_DIM = 32
V_DIM = 64
POOL_WINDOWS = (2, 4, 8, 16)
POOL_HIST = max(POOL_WINDOWS) - 1
HIST_PAD = POOL_HIST + 1
N_MOD = 6
ROPE_BASE = 10000.0
EPS = 1e-6
NEG_INF = -1e30
SOFTMAX_SCALE = 1.0 / math.sqrt(NOPE_DIM + ROPE_DIM)
LOG2E = math.log2(math.e)

LANES = 128
SUBLANES = 8
VMEM_LIMIT = 56 * 1024 * 1024
VMEM_LIMIT_MERGED = 62 * 1024 * 1024

HEAD_PAD = LANES


def _params(sem, vmem=VMEM_LIMIT):
    return pltpu.CompilerParams(dimension_semantics=sem, vmem_limit_bytes=vmem)


def _nt_dot(a, b):
    return lax.dot_general(a, b, (((1,), (1,)), ((), ())), preferred_element_type=F32)


def _dot(a, b):
    return jnp.dot(a, b, preferred_element_type=F32)


def _rms(x, g):
    return x * lax.rsqrt(jnp.mean(x * x, axis=-1, keepdims=True) + EPS) * g


def _mod_kernel(c_ref, w_ref, b_ref, o_ref):
    c = c_ref[...]
    sc = c * (1.0 / (1.0 + jnp.exp(-c)))
    o_ref[...] = _dot(sc.astype(BF16), w_ref[...].astype(BF16)) + b_ref[...]


def _modulation(c, w_mod, b_mod):
    n, d = c.shape
    nout = w_mod.shape[1]
    tn = d
    return pl.pallas_call(
        _mod_kernel,
        grid=(nout // tn,),
        in_specs=[
            pl.BlockSpec((n, d), lambda j: (0, 0)),
            pl.BlockSpec((d, tn), lambda j: (0, j)),
            pl.BlockSpec((1, tn), lambda j: (0, j)),
        ],
        out_specs=pl.BlockSpec((n, tn), lambda j: (0, j)),
        out_shape=jax.ShapeDtypeStruct((n, nout), F32),
        compiler_params=_params(("arbitrary",)),
        name="mod",
    )(c, w_mod, b_mod.reshape(1, nout))


def _pre_kernel(*refs, pos0, absorb, pool_w, q_lora, kv_lora, n_split):
    (x_ref, mod_ref, hist_ref, tab_ref, win_ref, wq_ref, wkv_ref, wpool_ref,
     gmix_ref, gq_ref, gkv_ref, pscale_ref) = refs[:12]
    if absorb:
        (q_ref, qlat_ref, lat_ref, kr_ref, pool_ref, nh_ref, ext_ref) = refs[12:]
    else:
        (q_ref, kcat_ref, v_ref, lat_ref, kr_ref, pool_ref, nh_ref, ext_ref) = refs[12:]
    si = pl.program_id(1)
    tb, ts, d = x_ref.shape
    hq = N_HEADS * HEAD_PAD
    shift1 = mod_ref[:, 0:1, :]
    scale1 = mod_ref[:, 1:2, :]
    o1 = pool_w
    o2 = o1 + q_lora
    o3 = o2 + kv_lora
    gw = pool_w // len(POOL_WINDOWS)

    @pl.when(si == 0)
    def _():
        ext_ref[:, 0:HIST_PAD, :] = hist_ref[...]

    def rows_block(r0, n):
        rs = slice(r0, r0 + n)
        r = tb * n
        e0 = HIST_PAD + r0
        val = {}

        def project():
            h = _rms(x_ref[:, rs, :], gmix_ref[...]) * (1.0 + scale1) + shift1
            proj = _dot(h.reshape(r, d).astype(BF16), win_ref[...])
            val["u"] = proj[:, :o1].reshape(tb, n, pool_w)
            ext_ref[:, e0:e0 + n, :] = val["u"]
            val["cq"] = proj[:, o1:o2]
            val["ckv"] = proj[:, o2:o3]
            val["kr_a"] = proj[:, o3:o3 + LANES].reshape(tb, n, LANES)
            val["kr_b"] = proj[:, o3 + LANES:o3 + 2 * LANES].reshape(tb, n, LANES)

        def queries():
            t_qa = tab_ref[rs, 0 * LANES:1 * LANES]
            t_qb = tab_ref[rs, 1 * LANES:2 * LANES]
            qab = _dot(_rms(val["cq"], gq_ref[...]).astype(BF16), wq_ref[...])
            for hd in range(N_HEADS):
                lo = hd * HEAD_PAD
                qa = qab[:, lo:lo + HEAD_PAD].reshape(tb, n, HEAD_PAD)
                qb = qab[:, hq + lo:hq + lo + HEAD_PAD].reshape(tb, n, HEAD_PAD)
                qh = qa * t_qa + qb * t_qb
                if absorb:
                    q_ref[:, hd, rs, :] = qh
                    ql = _dot(qh.reshape(r, HEAD_PAD).astype(BF16), wkv_ref[hd])
                    qlat_ref[:, hd, rs, :] = ql.reshape(tb, n, kv_lora)
                else:
                    q_ref[:, rs, lo:lo + HEAD_PAD] = qh.astype(BF16)

        def keys():
            lat = _rms(val["ckv"], gkv_ref[...])
            lat_ref[:, rs, :] = lat.reshape(tb, n, kv_lora)
            kr128 = (val["kr_a"] * tab_ref[rs, 2 * LANES:3 * LANES]
                     + val["kr_b"] * tab_ref[rs, 3 * LANES:4 * LANES])
            if absorb:
                kr_ref[:, rs, :] = kr128[:, :, :ROPE_DIM]
            else:
                for i in range(tb):
                    kr_ref[i, :, rs] = jnp.transpose(kr128[i])[:ROPE_DIM, :]
                kv = _dot(lat.astype(BF16), wkv_ref[...])
                for hd in range(N_HEADS):
                    lo = hd * HEAD_PAD
                    kcat_ref[:, rs, lo:lo + HEAD_PAD] = (
                        kv[:, lo:lo + HEAD_PAD].reshape(tb, n, HEAD_PAD) + kr128).astype(BF16)
                v_ref[:, rs, :] = kv[:, hq:].reshape(tb, n, N_HEADS * V_DIM).astype(BF16)

        def pooling():
            pos = (pos0 + (si * ts + r0).astype(F32)
                   + lax.broadcasted_iota(jnp.int32, (1, n, gw), 1).astype(F32))
            parts = []
            for g, w in enumerate(POOL_WINDOWS):
                lo = g * gw
                acc = ext_ref[:, e0:e0 + n, lo:lo + gw]
                for i in range(1, w):
                    acc = acc + ext_ref[:, e0 - i:e0 - i + n, lo:lo + gw]
                inv = 1.0 / jnp.minimum(float(w), pos + 1.0)
                parts.append(acc * inv - val["u"][:, :, lo:lo + gw])
            dpool = jnp.concatenate(parts, axis=-1).reshape(r, pool_w).astype(BF16)
            y = _dot(dpool, wpool_ref[...]) * pscale_ref[...]
            pool_ref[:, rs, :] = y.reshape(tb, n, pool_w).astype(BF16)

        return [project, pooling, queries, keys]

    blocks = [rows_block(blk * (ts // n_split), ts // n_split) for blk in range(n_split)]
    for stage_of_blocks in zip(*blocks):
        for stage in stage_of_blocks:
            stage()

    tail = ext_ref[:, ts:ts + HIST_PAD, :]
    nh_ref[...] = tail
    ext_ref[:, 0:HIST_PAD, :] = tail


def _pre(x, mod, hist16, tab, w, *, pos0, absorb, tb, ts):
    b, s, d = x.shape
    pool_w = hist16.shape[-1]
    q_lora = w["gq"].shape[-1]
    kv_lora = w["gkv"].shape[-1]
    hq = N_HEADS * HEAD_PAD
    grid = (b // tb, s // ts)
    tile = lambda width: pl.BlockSpec((tb, ts, width), lambda i, j: (i, j, 0))
    full = lambda a: pl.BlockSpec(a.shape, lambda i, j: (0,) * a.ndim)
    wkv = w["wabs"] if absorb else w["wkv"]
    in_specs = [
        tile(d),
        pl.BlockSpec((tb, N_MOD, d), lambda i, j: (i, 0, 0)),
        pl.BlockSpec((tb, HIST_PAD, pool_w), lambda i, j: (i, 0, 0)),
        pl.BlockSpec((ts, 4 * LANES), lambda i, j: (j, 0)),
        full(w["win"]), full(w["wq"]), full(wkv), full(w["wpool"]),
        full(w["gmix"]), full(w["gq"]), full(w["gkv"]), full(w["pscale"]),
    ]
    sds = jax.ShapeDtypeStruct
    if absorb:
        heads = lambda width: pl.BlockSpec((tb, N_HEADS, ts, width), lambda i, j: (i, 0, j, 0))
        out_shape = [sds((b, N_HEADS, s, HEAD_PAD), F32), sds((b, N_HEADS, s, kv_lora), F32)]
        out_specs = [heads(HEAD_PAD), heads(kv_lora)]
    else:
        out_shape = [sds((b, s, hq), BF16), sds((b, s, hq), BF16), sds((b, s, N_HEADS * V_DIM), BF16)]
        out_specs = [tile(hq), tile(hq), tile(N_HEADS * V_DIM)]
    if absorb:
        kr_shape, kr_spec = sds((b, s, ROPE_DIM), F32), tile(ROPE_DIM)
    else:
        kr_shape = sds((b, ROPE_DIM, s), F32)
        kr_spec = pl.BlockSpec((tb, ROPE_DIM, ts), lambda i, j: (i, 0, j))
    out_shape += [sds((b, s, kv_lora), F32), kr_shape, sds((b, s, pool_w), BF16),
                  sds((b, HIST_PAD, pool_w), F32)]
    out_specs += [tile(kv_lora), kr_spec, tile(pool_w),
                  pl.BlockSpec((tb, HIST_PAD, pool_w), lambda i, j: (i, 0, 0))]
    n_split = 2 if ts % (2 * LANES) == 0 else 1
    kern = functools.partial(_pre_kernel, pos0=float(pos0), absorb=absorb, pool_w=pool_w,
                             q_lora=q_lora, kv_lora=kv_lora, n_split=n_split)
    return pl.pallas_call(
        kern,
        grid=grid,
        in_specs=in_specs,
        out_specs=out_specs,
        out_shape=out_shape,
        scratch_shapes=[pltpu.VMEM((tb, HIST_PAD + ts, pool_w), F32)],
        compiler_params=_params(("arbitrary", "arbitrary")),
        name="pre_sample" if absorb else "pre_prompt",
    )(x, mod, hist16, tab, w["win"], w["wq"], wkv, w["wpool"], w["gmix"], w["gq"], w["gkv"], w["pscale"])


def _attn_kernel(q_ref, k_ref, v_ref, o_ref, *, tq):
    s = q_ref.shape[1]
    th = tq // 2
    first_head = lax.broadcasted_iota(jnp.int32, (tq, 2 * V_DIM), 1) < V_DIM
    rowmax = lambda a: jnp.max(a, axis=-1, keepdims=True)
    rowsum = lambda a: jnp.sum(a, axis=-1, keepdims=True)
    units = [(qi, hh) for qi in range(s // tq) for hh in range(2)]
    val = [dict() for _ in units]

    def scores(u):
        qi, hh = units[u]
        q0 = qi * tq
        hs = slice(hh * HEAD_PAD, (hh + 1) * HEAD_PAD)
        s_d = []
        for j in range(2):
            nk = (j + 1) * th
            r0 = q0 + j * th
            row = lax.broadcasted_iota(jnp.int32, (th, nk), 0) + j * th
            col = lax.broadcasted_iota(jnp.int32, (th, nk), 1)
            s_d.append(jnp.where(row >= col, _nt_dot(q_ref[0, r0:r0 + th, hs], k_ref[0, q0:q0 + nk, hs]),
                                 NEG_INF))
        val[u]["s_d"] = s_d
        val[u]["s_o"] = _nt_dot(q_ref[0, q0:q0 + tq, hs], k_ref[0, 0:q0, hs]) if qi > 0 else None

    def softmax(u):
        s_d, s_o = val[u]["s_d"], val[u]["s_o"]
        m = [rowmax(s_d[j]) for j in range(2)]
        if s_o is not None:
            m = [jnp.maximum(m[j], rowmax(s_o[j * th:(j + 1) * th])) for j in range(2)]
            p_o = jnp.concatenate([jnp.exp2(s_o[j * th:(j + 1) * th] - m[j]) for j in range(2)], axis=0)
            val[u]["l_o"] = rowsum(p_o)
            val[u]["p_o"] = p_o.astype(BF16)
        p_d = [jnp.exp2(s_d[j] - m[j]) for j in range(2)]
        val[u]["l_d"] = [rowsum(p) for p in p_d]
        val[u]["p_d"] = [p.astype(BF16) for p in p_d]

    def values(u):
        qi, hh = units[u]
        q0 = qi * tq
        acc = jnp.concatenate([_dot(val[u]["p_d"][j], v_ref[0, q0:q0 + (j + 1) * th, :]) for j in range(2)],
                              axis=0)
        l = jnp.concatenate(val[u]["l_d"], axis=0)
        if val[u]["s_o"] is not None:
            acc = acc + _dot(val[u]["p_o"], v_ref[0, 0:q0, :])
            l = l + val[u]["l_o"]
        out = acc * (1.0 / l)
        if hh == 0:
            val[u]["out"] = out
        else:
            o_ref[0, q0:q0 + tq, :] = jnp.where(first_head, val[u - 1]["out"], out).astype(BF16)

    n = len(units)
    for u in range(n + 2):
        if u < n:
            scores(u)
        if 1 <= u <= n:
            softmax(u - 1)
        if u >= 2:
            values(u - 2)


def _attention(q, kcat, v, *, tq):
    b, s, _ = q.shape
    pairs = N_HEADS // 2
    return pl.pallas_call(
        functools.partial(_attn_kernel, tq=tq),
        grid=(b, pairs),
        in_specs=[
            pl.BlockSpec((1, s, 2 * HEAD_PAD), lambda i, p: (i, 0, p)),
            pl.BlockSpec((1, s, 2 * HEAD_PAD), lambda i, p: (i, 0, p)),
            pl.BlockSpec((1, s, 2 * V_DIM), lambda i, p: (i, 0, p)),
        ],
        out_specs=pl.BlockSpec((1, s, 2 * V_DIM), lambda i, p: (i, 0, p)),
        out_shape=jax.ShapeDtypeStruct((b, s, N_HEADS * V_DIM), BF16),
        compiler_params=_params(("arbitrary", "arbitrary")),
        name="attn_prompt",
    )(q, kcat, v)


def _decode_ops(pt_ref, qlat_ref, qpad_ref, latn_ref, krn_ref, clat_hbm, ckr_hbm, o_ref,
                latbuf, krbuf, s_ref, sem, *, n_pages, page, chunk, s_new):
    t_past = n_pages * page
    t_all = t_past + page
    rows = qlat_ref.shape[1]
    spans = [(lo, min(chunk, t_all - lo)) for lo in list(range(0, t_past, chunk)) + [t_past]]

    def page_copies(bb, sl, p):
        pg = pt_ref[bb, p]
        dst = pl.ds(p * page, page)
        return (pltpu.make_async_copy(clat_hbm.at[pg], latbuf.at[sl, dst], sem.at[0, sl]),
                pltpu.make_async_copy(ckr_hbm.at[pg], krbuf.at[sl, :, dst], sem.at[1, sl]))

    def start_pages(bb, sl):
        for p in range(n_pages):
            for c in page_copies(bb, sl, p):
                c.start(priority=p % 2)

    def wait_pages(bb, sl):
        for p in range(n_pages):
            for c in page_copies(bb, sl, p):
                c.wait()

    def attend_stages(sl):
        val = {}

        def new_page():
            latbuf[sl, t_past:t_all, :] = jnp.zeros((page, latbuf.shape[-1]), F32)
            krbuf[sl, :, t_past:t_all] = jnp.zeros((ROPE_DIM, page), F32)
            latbuf[sl, t_past:t_past + s_new, :] = latn_ref[sl]
            krbuf[sl, :, t_past:t_past + s_new] = krn_ref[sl]

        def scores(lo, n):
            def run():
                lat_c = latbuf[sl, lo:lo + n, :].astype(BF16)
                s_ref[:, lo:lo + n] = (
                    _nt_dot(qlat_ref[sl].astype(BF16), lat_c)
                    + _dot(qpad_ref[sl][:, :ROPE_DIM].astype(BF16), krbuf[sl, :, lo:lo + n].astype(BF16)))
            return run

        def softmax():
            tok = lax.rem(lax.broadcasted_iota(jnp.int32, (rows, page), 0), s_new)
            col = lax.broadcasted_iota(jnp.int32, (rows, page), 1)
            s_ref[:, t_past:t_all] = jnp.where(col <= tok, s_ref[:, t_past:t_all], NEG_INF)
            s_all = s_ref[...]
            m = jnp.max(s_all, axis=-1, keepdims=True)
            p = jnp.exp2(s_all - m)
            val["inv_l"] = 1.0 / jnp.sum(p, axis=-1, keepdims=True)
            val["p"] = p.astype(BF16)
            val["out"] = None

        def values(lo, n, last):
            def run():
                t = _dot(val["p"][:, lo:lo + n], latbuf[sl, lo:lo + n, :].astype(BF16)) * val["inv_l"]
                val["out"] = t if val["out"] is None else val["out"] + t
                if last:
                    o_ref[sl] = val["out"]
            return run

        return ([new_page] + [scores(lo, n) for lo, n in spans] + [softmax]
                + [values(lo, n, i == len(spans) - 1) for i, (lo, n) in enumerate(spans)])

    return start_pages, wait_pages, attend_stages


def _post_dec_kernel(pt_ref, x_ref, mod_ref, pool_ref, att_ref, wout_ref, wuv_ref, wup_ref, wdown_ref,
                     gmlp_ref, gfin_ref, qlat_ref, qpad_ref, latn_ref, krn_ref, clat_hbm, ckr_hbm,
                     y_ref, o_ref, latbuf, krbuf, s_ref, sem, *, final, pool_w, kv_lora, ff_chunk,
                     n_pages, page, chunk, s_new):
    step = pl.program_id(0) * pl.num_programs(1) + pl.program_id(1)
    n_steps = pl.num_programs(0) * pl.num_programs(1)
    n_slots = latbuf.shape[0]
    start_pages, wait_pages, attend_stages = _decode_ops(
        pt_ref, qlat_ref, qpad_ref, latn_ref, krn_ref, clat_hbm, ckr_hbm, o_ref, latbuf, krbuf, s_ref, sem,
        n_pages=n_pages, page=page, chunk=chunk, s_new=s_new)
    tile_stages = _post_stages(x_ref, mod_ref, pool_ref, att_ref, wout_ref, wuv_ref, wup_ref, wdown_ref,
                               gmlp_ref, gfin_ref, y_ref, absorb=False, final=final, pool_w=pool_w,
                               kv_lora=kv_lora, ff_chunk=ff_chunk)
    b0 = step * n_slots
    last_b0 = (n_steps - 1) * n_slots

    @pl.when(step == 0)
    def _():
        for j in range(n_slots):
            start_pages(j, j)

    attn_stages = []
    for j in range(n_slots):
        attn_stages.append(functools.partial(wait_pages, b0 + j, j))
        attn_stages += attend_stages(j)
        attn_stages.append(functools.partial(start_pages, jnp.minimum(b0 + n_slots + j, last_b0 + j), j))

    done = 0
    for i, stage in enumerate(tile_stages):
        upto = (i + 1) * len(attn_stages) // len(tile_stages)
        for a in attn_stages[done:upto]:
            a()
        done = upto
        stage()

    @pl.when(step == n_steps - 1)
    def _():
        for j in range(n_slots):
            wait_pages(last_b0 + j, j)


def _post_prompt_with_decode(x, mod, pool, att, w, page_table, qlat, qpad, lat_new, kr_new,
                             cache_lat, cache_kr, *, final, ts):
    b, s, d = x.shape
    pool_w = pool.shape[-1]
    db, rows, kv_lora = qlat.shape
    n_pages = page_table.shape[1]
    page = cache_lat.shape[1]
    s_new = lat_new.shape[1]
    t_past = n_pages * page
    nj = s // ts
    n_steps = b * nj
    assert db % n_steps == 0, "sample batches are spread evenly over the prompt tiles"
    n_slots = db // n_steps
    tile = lambda width: pl.BlockSpec((1, ts, width), lambda i, j, pt: (i, j, 0))
    const = lambda a: pl.BlockSpec(a.shape, lambda i, j, pt: (0,) * a.ndim, pipeline_mode=pl.Buffered(1))
    per_step = lambda *shape: pl.BlockSpec((n_slots,) + shape, lambda i, j, pt: (i * nj + j, 0, 0))
    kern = functools.partial(_post_dec_kernel, final=final, pool_w=pool_w, kv_lora=kv_lora,
                             ff_chunk=min(1024, w["wup"].shape[1]), n_pages=n_pages, page=page,
                             chunk=min(2048, t_past), s_new=s_new)
    grid_spec = pltpu.PrefetchScalarGridSpec(
        num_scalar_prefetch=1,
        grid=(b, nj),
        in_specs=[
            tile(d),
            pl.BlockSpec((1, N_MOD, d), lambda i, j, pt: (i, 0, 0)),
            tile(pool_w),
            tile(att.shape[-1]),
            const(w["wout"]), const(w["wuvp"]), const(w["wup"]), const(w["wdown"]),
            const(w["gmlp"]), const(w["gfin"]),
            per_step(rows, kv_lora), per_step(rows, HEAD_PAD), per_step(s_new, kv_lora),
            per_step(ROPE_DIM, s_new),
            pl.BlockSpec(memory_space=pl.ANY),
            pl.BlockSpec(memory_space=pl.ANY),
        ],
        out_specs=[tile(d), per_step(rows, kv_lora)],
        scratch_shapes=[
            pltpu.VMEM((n_slots, t_past + page, kv_lora), F32),
            pltpu.VMEM((n_slots, ROPE_DIM, t_past + page), F32),
            pltpu.VMEM((rows, t_past + page), F32),
            pltpu.SemaphoreType.DMA((2, n_slots)),
        ],
    )
    return pl.pallas_call(
        kern,
        grid_spec=grid_spec,
        out_shape=[jax.ShapeDtypeStruct((b, s, d), F32), jax.ShapeDtypeStruct((db, rows, kv_lora), F32)],
        compiler_params=_params(("arbitrary", "arbitrary"), vmem=VMEM_LIMIT_MERGED),
        name="post_prompt_attn_sample",
    )(page_table, x, mod, pool, att, w["wout"], w["wuvp"], w["wup"], w["wdown"], w["gmlp"], w["gfin"],
      qlat, qpad, lat_new, kr_new, cache_lat, cache_kr)


def _post_kernel(x_ref, mod_ref, pool_ref, att_ref, wout_ref, wuv_ref, wup_ref, wdown_ref,
                 gmlp_ref, gfin_ref, y_ref, *, absorb, final, pool_w, kv_lora, ff_chunk):
    for stage in _post_stages(x_ref, mod_ref, pool_ref, att_ref, wout_ref, wuv_ref, wup_ref, wdown_ref,
                              gmlp_ref, gfin_ref, y_ref, absorb=absorb, final=final, pool_w=pool_w,
                              kv_lora=kv_lora, ff_chunk=ff_chunk):
        stage()


def _post_stages(x_ref, mod_ref, pool_ref, att_ref, wout_ref, wuv_ref, wup_ref, wdown_ref,
                 gmlp_ref, gfin_ref, y_ref, *, absorb, final, pool_w, kv_lora, ff_chunk):
    tb, ts, d = x_ref.shape
    r = tb * ts
    d_ff = wup_ref.shape[1]
    n_chunks = d_ff // ff_chunk
    val = {}

    def project():
        mix = _dot(pool_ref[...].reshape(r, pool_w), wout_ref[0:pool_w, :])
        if absorb:
            for p in range(N_HEADS // 2):
                o_pair = None
                for hh in range(2):
                    hd = 2 * p + hh
                    o_lat = att_ref[:, hd, :, :].reshape(r, kv_lora).astype(BF16)
                    t = _dot(o_lat, wuv_ref[hd])
                    o_pair = t if o_pair is None else o_pair + t
                lo = pool_w + p * 2 * V_DIM
                mix = mix + _dot(o_pair.astype(BF16), wout_ref[lo:lo + 2 * V_DIM, :])
        else:
            mix = mix + _dot(att_ref[...].reshape(r, N_HEADS * V_DIM), wout_ref[pool_w:, :])
        x1 = x_ref[...] + mod_ref[:, 2:3, :] * mix.reshape(tb, ts, d)
        val["x1"] = x1
        val["h2"] = (_rms(x1, gmlp_ref[...]) * (1.0 + mod_ref[:, 4:5, :]) + mod_ref[:, 3:4, :]
                     ).reshape(r, d).astype(BF16)
        val["mlp"] = None

    def mlp_up(c):
        def run():
            hid = jnp.maximum(_dot(val["h2"], wup_ref[:, c * ff_chunk:(c + 1) * ff_chunk]), 0.0)
            val["hid"] = (hid * hid).astype(BF16)
        return run

    def mlp_down(c):
        def run():
            t = _dot(val["hid"], wdown_ref[c * ff_chunk:(c + 1) * ff_chunk, :])
            val["mlp"] = t if val["mlp"] is None else val["mlp"] + t
            if c == n_chunks - 1:
                x2 = val["x1"] + mod_ref[:, 5:6, :] * val["mlp"].reshape(tb, ts, d)
                y_ref[...] = _rms(x2, gfin_ref[...]) if final else x2
        return run

    stages = [project]
    for c in range(n_chunks):
        stages += [mlp_up(c), mlp_down(c)]
    return stages


def _post_sample(x, mod, pool, o_lat, w, *, final, tb, ts):
    b, s, d = x.shape
    pool_w = pool.shape[-1]
    kv_lora = o_lat.shape[-1]
    tile = lambda width: pl.BlockSpec((tb, ts, width), lambda i, j: (i, j, 0))
    const = lambda a: pl.BlockSpec(a.shape, lambda i, j: (0,) * a.ndim, pipeline_mode=pl.Buffered(1))
    kern = functools.partial(_post_kernel, absorb=True, final=final, pool_w=pool_w, kv_lora=kv_lora,
                             ff_chunk=min(1024, w["wup"].shape[1]))
    return pl.pallas_call(
        kern,
        grid=(b // tb, s // ts),
        in_specs=[
            tile(d),
            pl.BlockSpec((tb, N_MOD, d), lambda i, j: (i, 0, 0)),
            tile(pool_w),
            pl.BlockSpec((tb, N_HEADS, ts, kv_lora), lambda i, j: (i, 0, j, 0)),
            const(w["wout"]), const(w["wuvp"]), const(w["wup"]), const(w["wdown"]),
            const(w["gmlp"]), const(w["gfin"]),
        ],
        out_specs=tile(d),
        out_shape=jax.ShapeDtypeStruct((b, s, d), F32),
        compiler_params=_params(("arbitrary", "arbitrary")),
        name="post_sample",
    )(x, mod, pool, o_lat, w["wout"], w["wuvp"], w["wup"], w["wdown"], w["gmlp"], w["gfin"])


def _rot_half(wr):
    half = ROPE_DIM // 2
    return jnp.concatenate([-wr[..., half:], wr[..., :half]], axis=-1)


def _layer_weights(w_in, g_mix, g_q, w_uq, g_kv, w_uk, w_uv, w_pool, pool_scale, w_out, g_mlp,
                   w_up, w_down, g_final):
    d, _ = w_in.shape
    q_lora = g_q.shape[-1]
    kv_lora = g_kv.shape[-1]
    n_groups, gw, _ = w_pool.shape
    pool_w = n_groups * gw
    o3 = pool_w + q_lora + kv_lora
    zpad = lambda rows, cols: jnp.zeros((rows, cols), F32)

    w_kr = w_in[:, o3:]
    win = jnp.concatenate([w_in[:, :o3], w_kr, zpad(d, LANES - ROPE_DIM),
                           _rot_half(w_kr), zpad(d, LANES - ROPE_DIM)], axis=1)

    q_rope = w_uq[:, :, NOPE_DIM:]
    q_nope = w_uq[:, :, :NOPE_DIM]
    hz = jnp.zeros((q_lora, N_HEADS, HEAD_PAD - ROPE_DIM - NOPE_DIM), F32)
    wq_a = jnp.concatenate([q_rope, q_nope, hz], axis=-1).reshape(q_lora, N_HEADS * HEAD_PAD)
    wq_b = jnp.concatenate([_rot_half(q_rope), jnp.zeros((q_lora, N_HEADS, HEAD_PAD - ROPE_DIM), F32)],
                           axis=-1).reshape(q_lora, N_HEADS * HEAD_PAD)
    wq = jnp.concatenate([wq_a, wq_b], axis=1)

    kz_lo = jnp.zeros((kv_lora, N_HEADS, ROPE_DIM), F32)
    kz_hi = jnp.zeros((kv_lora, N_HEADS, HEAD_PAD - ROPE_DIM - NOPE_DIM), F32)
    wk = jnp.concatenate([kz_lo, w_uk, kz_hi], axis=-1).reshape(kv_lora, N_HEADS * HEAD_PAD)
    wkv = jnp.concatenate([wk, w_uv.reshape(kv_lora, N_HEADS * V_DIM)], axis=1)
    wabs = jnp.transpose(jnp.concatenate([kz_lo, w_uk, kz_hi], axis=-1), (1, 2, 0))

    uv = jnp.transpose(w_uv, (1, 0, 2))
    uz = jnp.zeros_like(uv)
    even = (jnp.arange(N_HEADS) % 2 == 0)[:, None, None]
    wuvp = jnp.where(even, jnp.concatenate([uv, uz], axis=-1), jnp.concatenate([uz, uv], axis=-1))

    wpool = jax.scipy.linalg.block_diag(*[w_pool[g] for g in range(n_groups)])

    bf = lambda a: a.astype(BF16)
    return dict(
        win=bf(win), wq=bf(wq), wkv=bf(wkv), wabs=bf(wabs), wuvp=bf(wuvp), wpool=bf(wpool),
        wout=bf(w_out), wup=bf(w_up), wdown=bf(w_down),
        gmix=g_mix.reshape(1, -1), gq=g_q.reshape(1, -1), gkv=g_kv.reshape(1, -1),
        pscale=pool_scale.reshape(1, -1), gmlp=g_mlp.reshape(1, -1), gfin=g_final.reshape(1, -1),
    )


def _rope_tables(positions):
    inv = ROPE_BASE ** (-jnp.arange(0, ROPE_DIM, 2, dtype=F32) / ROPE_DIM)
    ang = positions[:, None] * inv[None, :]
    cos2 = jnp.tile(jnp.cos(ang), (1, 2))
    sin2 = jnp.tile(jnp.sin(ang), (1, 2))
    n = positions.shape[0]
    qs = SOFTMAX_SCALE * LOG2E
    t_qa = jnp.concatenate([cos2, jnp.ones((n, NOPE_DIM), F32),
                            jnp.zeros((n, HEAD_PAD - ROPE_DIM - NOPE_DIM), F32)], axis=1) * qs
    zr = jnp.zeros((n, HEAD_PAD - ROPE_DIM), F32)
    t_qb = jnp.concatenate([sin2, zr], axis=1) * qs
    t_ka = jnp.concatenate([cos2, zr], axis=1)
    t_kb = jnp.concatenate([sin2, zr], axis=1)
    return jnp.concatenate([t_qa, t_qb, t_ka, t_kb], axis=1)


def _seq_tile(s, target):
    t = min(s, target)
    while s % t:
        t //= 2
    return t


def kernel(x_prompt, x_sample, cache_latent, cache_krope, state_pool, page_table, c_prompt, c_sample,
           w_mod, b_mod, g_mix, w_in, g_q, w_uq, g_kv, w_uk, w_uv, w_pool, pool_scale, w_out,
           g_mlp, w_up, w_down, g_final):
    bp, sp, d = x_prompt.shape
    db, ss, _ = x_sample.shape
    depth = w_mod.shape[0]
    n_pages = page_table.shape[1]
    page = cache_latent.shape[2]
    past = n_pages * page
    pool_w = state_pool.shape[-1]
    kv_lora = g_kv.shape[-1]

    tab_p = _rope_tables(jnp.arange(sp, dtype=F32))
    tab_s = _rope_tables(jnp.arange(ss, dtype=F32) + float(past))
    ts_p = _seq_tile(sp, 512)
    tb_s = _seq_tile(db, max(1, 512 // ss))

    xp, xs = x_prompt, x_sample
    outs = [[] for _ in range(6)]
    for l in range(depth):
        w = _layer_weights(w_in[l], g_mix[l], g_q[l], w_uq[l], g_kv[l], w_uk[l], w_uv[l], w_pool[l],
                           pool_scale[l], w_out[l], g_mlp[l], w_up[l], w_down[l], g_final)
        mod = _modulation(jnp.concatenate([c_prompt, c_sample], axis=0), w_mod[l], b_mod[l])
        mod = mod.reshape(bp + db, N_MOD, d)
        mod_p, mod_s = mod[:bp], mod[bp:]

        hist_p = jnp.zeros((bp, HIST_PAD, pool_w), F32)
        q, kcat, v, lat_p, kr_p, pool_p, nh_p = _pre(
            xp, mod_p, hist_p, tab_p, w, pos0=0.0, absorb=False, tb=1, ts=ts_p)
        att_p = _attention(q, kcat, v, tq=_seq_tile(sp, 512))
        final = l == depth - 1

        hist_s = jnp.pad(state_pool[l], ((0, 0), (HIST_PAD - POOL_HIST, 0), (0, 0)))
        qs, qlat, lat_s, kr_s, pool_s, nh_s = _pre(
            xs, mod_s, hist_s, tab_s, w, pos0=float(past), absorb=True, tb=tb_s, ts=ss)
        rows = ss * N_HEADS
        xp, o_lat = _post_prompt_with_decode(
            xp, mod_p, pool_p, att_p, w, page_table,
            qlat.reshape(db, rows, kv_lora),
            qs.reshape(db, rows, HEAD_PAD),
            lat_s, jnp.swapaxes(kr_s, 1, 2),
            cache_latent[l], jnp.swapaxes(cache_krope[l], 1, 2), final=final, ts=ts_p)
        xs = _post_sample(xs, mod_s, pool_s, o_lat.reshape(db, N_HEADS, ss, kv_lora), w,
                          final=final, tb=tb_s, ts=ss)

        for lst, val in zip(outs, (lat_p, jnp.swapaxes(kr_p, 1, 2), nh_p[:, HIST_PAD - POOL_HIST:],
                                   lat_s, kr_s, nh_s[:, HIST_PAD - POOL_HIST:])):
            lst.append(val)
    return (xp, xs) + tuple(jnp.stack(o) for o in outs)
```

```python
import functools
import math

import jax
import jax.numpy as jnp
from jax import lax
from jax.experimental import pallas as pl
from jax.experimental.pallas import tpu as pltpu

F32 = jnp.float32
BF16 = jnp.bfloat16

N_HEADS = 8
NOPE_DIM = 64
ROPE_DIM = 32
V_DIM = 64
POOL_WINDOWS = (2, 4, 8, 16)
POOL_HIST = max(POOL_WINDOWS) - 1
HIST_PAD = POOL_HIST + 1
N_MOD = 6
ROPE_BASE = 10000.0
EPS = 1e-6
NEG_INF = -1e30
SOFTMAX_SCALE = 1.0 / math.sqrt(NOPE_DIM + ROPE_DIM)
LOG2E = math.log2(math.e)

LANES = 128
SUBLANES = 8
VMEM_LIMIT = 56 * 1024 * 1024
VMEM_LIMIT_MERGED = 62 * 1024 * 1024

HEAD_PAD = LANES


def _params(sem, vmem=VMEM_LIMIT):
    return pltpu.CompilerParams(dimension_semantics=sem, vmem_limit_bytes=vmem)


def _nt_dot(a, b):
    return lax.dot_general(a, b, (((1,), (1,)), ((), ())), preferred_element_type=F32)


def _dot(a, b):
    return jnp.dot(a, b, preferred_element_type=F32)


def _rms(x, g):
    return x * lax.rsqrt(jnp.mean(x * x, axis=-1, keepdims=True) + EPS) * g


def _mod_kernel(c_ref, w_ref, b_ref, o_ref):
    c = c_ref[...]
    sc = c * (1.0 / (1.0 + jnp.exp(-c)))
    o_ref[...] = _dot(sc.astype(BF16), w_ref[...].astype(BF16)) + b_ref[...]


def _modulation(c, w_mod, b_mod):
    n, d = c.shape
    nout = w_mod.shape[1]
    tn = d
    return pl.pallas_call(
        _mod_kernel,
        grid=(nout // tn,),
        in_specs=[
            pl.BlockSpec((n, d), lambda j: (0, 0)),
            pl.BlockSpec((d, tn), lambda j: (0, j)),
            pl.BlockSpec((1, tn), lambda j: (0, j)),
        ],
        out_specs=pl.BlockSpec((n, tn), lambda j: (0, j)),
        out_shape=jax.ShapeDtypeStruct((n, nout), F32),
        compiler_params=_params(("arbitrary",)),
        name="mod",
    )(c, w_mod, b_mod.reshape(1, nout))


def _pre_kernel(*refs, pos0, absorb, pool_w, q_lora, kv_lora, n_split):
    (x_ref, mod_ref, hist_ref, tab_ref, win_ref, wq_ref, wkv_ref, wpool_ref,
     gmix_ref, gq_ref, gkv_ref, pscale_ref) = refs[:12]
    if absorb:
        (q_ref, qlat_ref, lat_ref, kr_ref, pool_ref, nh_ref, ext_ref) = refs[12:]
    else:
        (q_ref, kcat_ref, v_ref, lat_ref, kr_ref, pool_ref, nh_ref, ext_ref) = refs[12:]
    si = pl.program_id(1)
    tb, ts, d = x_ref.shape
    hq = N_HEADS * HEAD_PAD
    shift1 = mod_ref[:, 0:1, :]
    scale1 = mod_ref[:, 1:2, :]
    o1 = pool_w
    o2 = o1 + q_lora
    o3 = o2 + kv_lora
    gw = pool_w // len(POOL_WINDOWS)

    @pl.when(si == 0)
    def _():
        ext_ref[:, 0:HIST_PAD, :] = hist_ref[...]

    def rows_block(r0, n):
        rs = slice(r0, r0 + n)
        r = tb * n
        e0 = HIST_PAD + r0
        val = {}

        def project():
            h = _rms(x_ref[:, rs, :], gmix_ref[...]) * (1.0 + scale1) + shift1
            proj = _dot(h.reshape(r, d).astype(BF16), win_ref[...])
            val["u"] = proj[:, :o1].reshape(tb, n, pool_w)
            ext_ref[:, e0:e0 + n, :] = val["u"]
            val["cq"] = proj[:, o1:o2]
            val["ckv"] = proj[:, o2:o3]
            val["kr_a"] = proj[:, o3:o3 + LANES].reshape(tb, n, LANES)
            val["kr_b"] = proj[:, o3 + LANES:o3 + 2 * LANES].reshape(tb, n, LANES)

        def queries():
            t_qa = tab_ref[rs, 0 * LANES:1 * LANES]
            t_qb = tab_ref[rs, 1 * LANES:2 * LANES]
            qab = _dot(_rms(val["cq"], gq_ref[...]).astype(BF16), wq_ref[...])
            for hd in range(N_HEADS):
                lo = hd * HEAD_PAD
                qa = qab[:, lo:lo + HEAD_PAD].reshape(tb, n, HEAD_PAD)
                qb = qab[:, hq + lo:hq + lo + HEAD_PAD].reshape(tb, n, HEAD_PAD)
                qh = qa * t_qa + qb * t_qb
                if absorb:
                    q_ref[:, hd, rs, :] = qh
                    ql = _dot(qh.reshape(r, HEAD_PAD).astype(BF16), wkv_ref[hd])
                    qlat_ref[:, hd, rs, :] = ql.reshape(tb, n, kv_lora)
                else:
                    q_ref[:, rs, lo:lo + HEAD_PAD] = qh.astype(BF16)

        def keys():
            lat = _rms(val["ckv"], gkv_ref[...])
            lat_ref[:, rs, :] = lat.reshape(tb, n, kv_lora)
            kr128 = (val["kr_a"] * tab_ref[rs, 2 * LANES:3 * LANES]
                     + val["kr_b"] * tab_ref[rs, 3 * LANES:4 * LANES])
            if absorb:
                kr_ref[:, rs, :] = kr128[:, :, :ROPE_DIM]
            else:
                for i in range(tb):
                    kr_ref[i, :, rs] = jnp.transpose(kr128[i])[:ROPE_DIM, :]
                kv = _dot(lat.astype(BF16), wkv_ref[...])
                for hd in range(N_HEADS):
                    lo = hd * HEAD_PAD
                    kcat_ref[:, rs, lo:lo + HEAD_PAD] = (
                        kv[:, lo:lo + HEAD_PAD].reshape(tb, n, HEAD_PAD) + kr128).astype(BF16)
                v_ref[:, rs, :] = kv[:, hq:].reshape(tb, n, N_HEADS * V_DIM).astype(BF16)

        def pooling():
            pos = (pos0 + (si * ts + r0).astype(F32)
                   + lax.broadcasted_iota(jnp.int32, (1, n, gw), 1).astype(F32))
            parts = []
            for g, w in enumerate(POOL_WINDOWS):
                lo = g * gw
                acc = ext_ref[:, e0:e0 + n, lo:lo + gw]
                for i in range(1, w):
                    acc = acc + ext_ref[:, e0 - i:e0 - i + n, lo:lo + gw]
                inv = 1.0 / jnp.minimum(float(w), pos + 1.0)
                parts.append(acc * inv - val["u"][:, :, lo:lo + gw])
            dpool = jnp.concatenate(parts, axis=-1).reshape(r, pool_w).astype(BF16)
            y = _dot(dpool, wpool_ref[...]) * pscale_ref[...]
            pool_ref[:, rs, :] = y.reshape(tb, n, pool_w).astype(BF16)

        return [project, pooling, queries, keys]

    blocks = [rows_block(blk * (ts // n_split), ts // n_split) for blk in range(n_split)]
    for stage_of_blocks in zip(*blocks):
        for stage in stage_of_blocks:
            stage()

    tail = ext_ref[:, ts:ts + HIST_PAD, :]
    nh_ref[...] = tail
    ext_ref[:, 0:HIST_PAD, :] = tail


def _pre(x, mod, hist16, tab, w, *, pos0, absorb, tb, ts):
    b, s, d = x.shape
    pool_w = hist16.shape[-1]
    q_lora = w["gq"].shape[-1]
    kv_lora = w["gkv"].shape[-1]
    hq = N_HEADS * HEAD_PAD
    grid = (b // tb, s // ts)
    tile = lambda width: pl.BlockSpec((tb, ts, width), lambda i, j: (i, j, 0))
    full = lambda a: pl.BlockSpec(a.shape, lambda i, j: (0,) * a.ndim)
    wkv = w["wabs"] if absorb else w["wkv"]
    in_specs = [
        tile(d),
        pl.BlockSpec((tb, N_MOD, d), lambda i, j: (i, 0, 0)),
        pl.BlockSpec((tb, HIST_PAD, pool_w), lambda i, j: (i, 0, 0)),
        pl.BlockSpec((ts, 4 * LANES), lambda i, j: (j, 0)),
        full(w["win"]), full(w["wq"]), full(wkv), full(w["wpool"]),
        full(w["gmix"]), full(w["gq"]), full(w["gkv"]), full(w["pscale"]),
    ]
    sds = jax.ShapeDtypeStruct
    if absorb:
        heads = lambda width: pl.BlockSpec((tb, N_HEADS, ts, width), lambda i, j: (i, 0, j, 0))
        out_shape = [sds((b, N_HEADS, s, HEAD_PAD), F32), sds((b, N_HEADS, s, kv_lora), F32)]
        out_specs = [heads(HEAD_PAD), heads(kv_lora)]
    else:
        out_shape = [sds((b, s, hq), BF16), sds((b, s, hq), BF16), sds((b, s, N_HEADS * V_DIM), BF16)]
        out_specs = [tile(hq), tile(hq), tile(N_HEADS * V_DIM)]
    if absorb:
        kr_shape, kr_spec = sds((b, s, ROPE_DIM), F32), tile(ROPE_DIM)
    else:
        kr_shape = sds((b, ROPE_DIM, s), F32)
        kr_spec = pl.BlockSpec((tb, ROPE_DIM, ts), lambda i, j: (i, 0, j))
    out_shape += [sds((b, s, kv_lora), F32), kr_shape, sds((b, s, pool_w), BF16),
                  sds((b, HIST_PAD, pool_w), F32)]
    out_specs += [tile(kv_lora), kr_spec, tile(pool_w),
                  pl.BlockSpec((tb, HIST_PAD, pool_w), lambda i, j: (i, 0, 0))]
    n_split = ts // (2 * LANES) if ts % (2 * LANES) == 0 else 1
    kern = functools.partial(_pre_kernel, pos0=float(pos0), absorb=absorb, pool_w=pool_w,
                             q_lora=q_lora, kv_lora=kv_lora, n_split=n_split)
    return pl.pallas_call(
        kern,
        grid=grid,
        in_specs=in_specs,
        out_specs=out_specs,
        out_shape=out_shape,
        scratch_shapes=[pltpu.VMEM((tb, HIST_PAD + ts, pool_w), F32)],
        compiler_params=_params(("arbitrary", "arbitrary")),
        name="pre_sample" if absorb else "pre_prompt",
    )(x, mod, hist16, tab, w["win"], w["wq"], wkv, w["wpool"], w["gmix"], w["gq"], w["gkv"], w["pscale"])


def _attn_kernel(q_ref, k_ref, v_ref, o_ref, *, tq):
    s = q_ref.shape[1]
    th = tq // 2
    first_head = lax.broadcasted_iota(jnp.int32, (tq, 2 * V_DIM), 1) < V_DIM
    rowmax = lambda a: jnp.max(a, axis=-1, keepdims=True)
    rowsum = lambda a: jnp.sum(a, axis=-1, keepdims=True)
    heads = q_ref.shape[2] // HEAD_PAD
    units = [(qi, hh) for qi in range(s // tq) for hh in range(heads)]
    val = [dict() for _ in units]

    def scores(u):
        qi, hh = units[u]
        q0 = qi * tq
        hs = slice(hh * HEAD_PAD, (hh + 1) * HEAD_PAD)
        s_d = []
        for j in range(2):
            nk = (j + 1) * th
            r0 = q0 + j * th
            row = lax.broadcasted_iota(jnp.int32, (th, nk), 0) + j * th
            col = lax.broadcasted_iota(jnp.int32, (th, nk), 1)
            s_d.append(jnp.where(row >= col, _nt_dot(q_ref[0, r0:r0 + th, hs], k_ref[0, q0:q0 + nk, hs]),
                                 NEG_INF))
        val[u]["s_d"] = s_d
        val[u]["s_o"] = _nt_dot(q_ref[0, q0:q0 + tq, hs], k_ref[0, 0:q0, hs]) if qi > 0 else None

    def softmax(u):
        s_d, s_o = val[u]["s_d"], val[u]["s_o"]
        m = [rowmax(s_d[j]) for j in range(2)]
        if s_o is not None:
            m = [jnp.maximum(m[j], rowmax(s_o[j * th:(j + 1) * th])) for j in range(2)]
            p_o = jnp.concatenate([jnp.exp2(s_o[j * th:(j + 1) * th] - m[j]) for j in range(2)], axis=0)
            val[u]["l_o"] = rowsum(p_o)
            val[u]["p_o"] = p_o.astype(BF16)
        p_d = [jnp.exp2(s_d[j] - m[j]) for j in range(2)]
        val[u]["l_d"] = [rowsum(p) for p in p_d]
        val[u]["p_d"] = [p.astype(BF16) for p in p_d]

    def values(u):
        qi, hh = units[u]
        q0 = qi * tq
        vs = slice(hh // 2 * 2 * V_DIM, (hh // 2 + 1) * 2 * V_DIM)
        acc = jnp.concatenate([_dot(val[u]["p_d"][j], v_ref[0, q0:q0 + (j + 1) * th, vs]) for j in range(2)],
                              axis=0)
        l = jnp.concatenate(val[u]["l_d"], axis=0)
        if val[u]["s_o"] is not None:
            acc = acc + _dot(val[u]["p_o"], v_ref[0, 0:q0, vs])
            l = l + val[u]["l_o"]
        out = acc * (1.0 / l)
        if hh % 2 == 0:
            val[u]["out"] = out
        else:
            o_ref[0, q0:q0 + tq, vs] = jnp.where(first_head, val[u - 1]["out"], out).astype(BF16)

    scores(0)
    for u in range(len(units)):
        if u + 1 < len(units):
            scores(u + 1)
        softmax(u)
        values(u)


def _attention(q, kcat, v, *, tq, heads):
    b, s, _ = q.shape
    return pl.pallas_call(
        functools.partial(_attn_kernel, tq=tq),
        grid=(b, N_HEADS // heads),
        in_specs=[
            pl.BlockSpec((1, s, heads * HEAD_PAD), lambda i, p: (i, 0, p)),
            pl.BlockSpec((1, s, heads * HEAD_PAD), lambda i, p: (i, 0, p)),
            pl.BlockSpec((1, s, heads * V_DIM), lambda i, p: (i, 0, p)),
        ],
        out_specs=pl.BlockSpec((1, s, heads * V_DIM), lambda i, p: (i, 0, p)),
        out_shape=jax.ShapeDtypeStruct((b, s, N_HEADS * V_DIM), BF16),
        compiler_params=_params(("arbitrary", "arbitrary")),
        name="attn_prompt",
    )(q, kcat, v)


def _decode_ops(pt_ref, qlat_ref, qpad_ref, latn_ref, krn_ref, clat_hbm, ckr_hbm, o_ref,
                latbuf, krbuf, s_ref, sem, *, n_pages, page, chunk, s_new):
    t_past = n_pages * page
    t_all = t_past + page
    rows = qlat_ref.shape[1]
    spans = [(lo, min(chunk, t_all - lo)) for lo in list(range(0, t_past, chunk)) + [t_past]]

    def page_copies(bb, sl, p):
        pg = pt_ref[bb, p]
        dst = pl.ds(p * page, page)
        return (pltpu.make_async_copy(clat_hbm.at[pg], latbuf.at[sl, dst], sem.at[0, sl]),
                pltpu.make_async_copy(ckr_hbm.at[pg], krbuf.at[sl, :, dst], sem.at[1, sl]))

    def start_pages(bb, sl):
        for p in range(n_pages):
            for c in page_copies(bb, sl, p):
                c.start(priority=p % 2)

    def wait_pages(bb, sl):
        for p in range(n_pages):
            for c in page_copies(bb, sl, p):
                c.wait()

    def attend_stages(sl):
        val = {}

        def new_page():
            latbuf[sl, t_past:t_all, :] = jnp.zeros((page, latbuf.shape[-1]), F32)
            krbuf[sl, :, t_past:t_all] = jnp.zeros((ROPE_DIM, page), F32)
            latbuf[sl, t_past:t_past + s_new, :] = latn_ref[sl]
            krbuf[sl, :, t_past:t_past + s_new] = krn_ref[sl]

        def scores(lo, n):
            def run():
                lat_c = latbuf[sl, lo:lo + n, :].astype(BF16)
                s_ref[:, lo:lo + n] = (
                    _nt_dot(qlat_ref[sl].astype(BF16), lat_c)
                    + _dot(qpad_ref[sl][:, :ROPE_DIM].astype(BF16), krbuf[sl, :, lo:lo + n].astype(BF16)))
            return run

        def softmax():
            tok = lax.rem(lax.broadcasted_iota(jnp.int32, (rows, page), 0), s_new)
            col = lax.broadcasted_iota(jnp.int32, (rows, page), 1)
            s_ref[:, t_past:t_all] = jnp.where(col <= tok, s_ref[:, t_past:t_all], NEG_INF)
            s_all = s_ref[...]
            m = jnp.max(s_all, axis=-1, keepdims=True)
            p = jnp.exp2(s_all - m)
            val["inv_l"] = 1.0 / jnp.sum(p, axis=-1, keepdims=True)
            val["p"] = p.astype(BF16)
            val["out"] = None

        def values(lo, n, last):
            def run():
                t = _dot(val["p"][:, lo:lo + n], latbuf[sl, lo:lo + n, :].astype(BF16)) * val["inv_l"]
                val["out"] = t if val["out"] is None else val["out"] + t
                if last:
                    o_ref[sl] = val["out"]
            return run

        return ([new_page] + [scores(lo, n) for lo, n in spans] + [softmax]
                + [values(lo, n, i == len(spans) - 1) for i, (lo, n) in enumerate(spans)])

    return start_pages, wait_pages, attend_stages


def _post_dec_kernel(pt_ref, x_ref, mod_ref, pool_ref, att_ref, wout_ref, wuv_ref, wup_ref, wdown_ref,
                     gmlp_ref, gfin_ref, qlat_ref, qpad_ref, latn_ref, krn_ref, clat_hbm, ckr_hbm,
                     y_ref, o_ref, latbuf, krbuf, s_ref, sem, *, final, pool_w, kv_lora, ff_chunk,
                     n_pages, page, chunk, s_new):
    step = pl.program_id(0) * pl.num_programs(1) + pl.program_id(1)
    n_steps = pl.num_programs(0) * pl.num_programs(1)
    n_slots = latbuf.shape[0]
    start_pages, wait_pages, attend_stages = _decode_ops(
        pt_ref, qlat_ref, qpad_ref, latn_ref, krn_ref, clat_hbm, ckr_hbm, o_ref, latbuf, krbuf, s_ref, sem,
        n_pages=n_pages, page=page, chunk=chunk, s_new=s_new)
    tile_stages = _post_stages(x_ref, mod_ref, pool_ref, att_ref, wout_ref, wuv_ref, wup_ref, wdown_ref,
                               gmlp_ref, gfin_ref, y_ref, absorb=False, final=final, pool_w=pool_w,
                               kv_lora=kv_lora, ff_chunk=ff_chunk)
    b0 = step * n_slots
    last_b0 = (n_steps - 1) * n_slots

    @pl.when(step == 0)
    def _():
        for j in range(n_slots):
            start_pages(j, j)

    attn_stages = []
    for j in range(n_slots):
        attn_stages.append(functools.partial(wait_pages, b0 + j, j))
        attn_stages += attend_stages(j)
        attn_stages.append(functools.partial(start_pages, jnp.minimum(b0 + n_slots + j, last_b0 + j), j))

    done = 0
    for i, stage in enumerate(tile_stages):
        upto = (i + 1) * len(attn_stages) // len(tile_stages)
        for a in attn_stages[done:upto]:
            a()
        done = upto
        stage()

    @pl.when(step == n_steps - 1)
    def _():
        for j in range(n_slots):
            wait_pages(last_b0 + j, j)


def _post_prompt_with_decode(x, mod, pool, att, w, page_table, qlat, qpad, lat_new, kr_new,
                             cache_lat, cache_kr, *, final, ts):
    b, s, d = x.shape
    pool_w = pool.shape[-1]
    db, rows, kv_lora = qlat.shape
    n_pages = page_table.shape[1]
    page = cache_lat.shape[1]
    s_new = lat_new.shape[1]
    t_past = n_pages * page
    nj = s // ts
    n_steps = b * nj
    assert db % n_steps == 0, "sample batches are spread evenly over the prompt tiles"
    n_slots = db // n_steps
    tile = lambda width: pl.BlockSpec((1, ts, width), lambda i, j, pt: (i, j, 0))
    const = lambda a: pl.BlockSpec(a.shape, lambda i, j, pt: (0,) * a.ndim, pipeline_mode=pl.Buffered(1))
    per_step = lambda *shape: pl.BlockSpec((n_slots,) + shape, lambda i, j, pt: (i * nj + j, 0, 0))
    kern = functools.partial(_post_dec_kernel, final=final, pool_w=pool_w, kv_lora=kv_lora,
                             ff_chunk=min(1024, w["wup"].shape[1]), n_pages=n_pages, page=page,
                             chunk=min(2048, t_past), s_new=s_new)
    grid_spec = pltpu.PrefetchScalarGridSpec(
        num_scalar_prefetch=1,
        grid=(b, nj),
        in_specs=[
            tile(d),
            pl.BlockSpec((1, N_MOD, d), lambda i, j, pt: (i, 0, 0)),
            tile(pool_w),
            tile(att.shape[-1]),
            const(w["wout"]), const(w["wuvp"]), const(w["wup"]), const(w["wdown"]),
            const(w["gmlp"]), const(w["gfin"]),
            per_step(rows, kv_lora), per_step(rows, HEAD_PAD), per_step(s_new, kv_lora),
            per_step(ROPE_DIM, s_new),
            pl.BlockSpec(memory_space=pl.ANY),
            pl.BlockSpec(memory_space=pl.ANY),
        ],
        out_specs=[tile(d), per_step(rows, kv_lora)],
        scratch_shapes=[
            pltpu.VMEM((n_slots, t_past + page, kv_lora), F32),
            pltpu.VMEM((n_slots, ROPE_DIM, t_past + page), F32),
            pltpu.VMEM((rows, t_past + page), F32),
            pltpu.SemaphoreType.DMA((2, n_slots)),
        ],
    )
    return pl.pallas_call(
        kern,
        grid_spec=grid_spec,
        out_shape=[jax.ShapeDtypeStruct((b, s, d), F32), jax.ShapeDtypeStruct((db, rows, kv_lora), F32)],
        compiler_params=_params(("arbitrary", "arbitrary"), vmem=VMEM_LIMIT_MERGED),
        name="post_prompt_attn_sample",
    )(page_table, x, mod, pool, att, w["wout"], w["wuvp"], w["wup"], w["wdown"], w["gmlp"], w["gfin"],
      qlat, qpad, lat_new, kr_new, cache_lat, cache_kr)


def _post_kernel(x_ref, mod_ref, pool_ref, att_ref, wout_ref, wuv_ref, wup_ref, wdown_ref,
                 gmlp_ref, gfin_ref, y_ref, *, absorb, final, pool_w, kv_lora, ff_chunk):
    for stage in _post_stages(x_ref, mod_ref, pool_ref, att_ref, wout_ref, wuv_ref, wup_ref, wdown_ref,
                              gmlp_ref, gfin_ref, y_ref, absorb=absorb, final=final, pool_w=pool_w,
                              kv_lora=kv_lora, ff_chunk=ff_chunk):
        stage()


def _post_stages(x_ref, mod_ref, pool_ref, att_ref, wout_ref, wuv_ref, wup_ref, wdown_ref,
                 gmlp_ref, gfin_ref, y_ref, *, absorb, final, pool_w, kv_lora, ff_chunk):
    tb, ts, d = x_ref.shape
    r = tb * ts
    d_ff = wup_ref.shape[1]
    n_chunks = d_ff // ff_chunk
    val = {}

    def project():
        mix = _dot(pool_ref[...].reshape(r, pool_w), wout_ref[0:pool_w, :])
        if absorb:
            for p in range(N_HEADS // 2):
                o_pair = None
                for hh in range(2):
                    hd = 2 * p + hh
                    o_lat = att_ref[:, hd, :, :].reshape(r, kv_lora).astype(BF16)
                    t = _dot(o_lat, wuv_ref[hd])
                    o_pair = t if o_pair is None else o_pair + t
                lo = pool_w + p * 2 * V_DIM
                mix = mix + _dot(o_pair.astype(BF16), wout_ref[lo:lo + 2 * V_DIM, :])
        else:
            mix = mix + _dot(att_ref[...].reshape(r, N_HEADS * V_DIM), wout_ref[pool_w:, :])
        x1 = x_ref[...] + mod_ref[:, 2:3, :] * mix.reshape(tb, ts, d)
        val["x1"] = x1
        val["h2"] = (_rms(x1, gmlp_ref[...]) * (1.0 + mod_ref[:, 4:5, :]) + mod_ref[:, 3:4, :]
                     ).reshape(r, d).astype(BF16)
        val["mlp"] = None

    def mlp_up(c):
        def run():
            hid = jnp.maximum(_dot(val["h2"], wup_ref[:, c * ff_chunk:(c + 1) * ff_chunk]), 0.0)
            val["hid"] = (hid * hid).astype(BF16)
        return run

    def mlp_down(c):
        def run():
            t = _dot(val["hid"], wdown_ref[c * ff_chunk:(c + 1) * ff_chunk, :])
            val["mlp"] = t if val["mlp"] is None else val["mlp"] + t
            if c == n_chunks - 1:
                x2 = val["x1"] + mod_ref[:, 5:6, :] * val["mlp"].reshape(tb, ts, d)
                y_ref[...] = _rms(x2, gfin_ref[...]) if final else x2
        return run

    stages = [project]
    for c in range(n_chunks):
        stages += [mlp_up(c), mlp_down(c)]
    return stages


def _post_sample(x, mod, pool, o_lat, w, *, final, tb, ts):
    b, s, d = x.shape
    pool_w = pool.shape[-1]
    kv_lora = o_lat.shape[-1]
    tile = lambda width: pl.BlockSpec((tb, ts, width), lambda i, j: (i, j, 0))
    const = lambda a: pl.BlockSpec(a.shape, lambda i, j: (0,) * a.ndim, pipeline_mode=pl.Buffered(1))
    kern = functools.partial(_post_kernel, absorb=True, final=final, pool_w=pool_w, kv_lora=kv_lora,
                             ff_chunk=min(1024, w["wup"].shape[1]))
    return pl.pallas_call(
        kern,
        grid=(b // tb, s // ts),
        in_specs=[
            tile(d),
            pl.BlockSpec((tb, N_MOD, d), lambda i, j: (i, 0, 0)),
            tile(pool_w),
            pl.BlockSpec((tb, N_HEADS, ts, kv_lora), lambda i, j: (i, 0, j, 0)),
            const(w["wout"]), const(w["wuvp"]), const(w["wup"]), const(w["wdown"]),
            const(w["gmlp"]), const(w["gfin"]),
        ],
        out_specs=tile(d),
        out_shape=jax.ShapeDtypeStruct((b, s, d), F32),
        compiler_params=_params(("arbitrary", "arbitrary")),
        name="post_sample",
    )(x, mod, pool, o_lat, w["wout"], w["wuvp"], w["wup"], w["wdown"], w["gmlp"], w["gfin"])


def _rot_half(wr):
    half = ROPE_DIM // 2
    return jnp.concatenate([-wr[..., half:], wr[..., :half]], axis=-1)


def _layer_weights(w_in, g_mix, g_q, w_uq, g_kv, w_uk, w_uv, w_pool, pool_scale, w_out, g_mlp,
                   w_up, w_down, g_final):
    d, _ = w_in.shape
    q_lora = g_q.shape[-1]
    kv_lora = g_kv.shape[-1]
    n_groups, gw, _ = w_pool.shape
    pool_w = n_groups * gw
    o3 = pool_w + q_lora + kv_lora
    zpad = lambda rows, cols: jnp.zeros((rows, cols), F32)

    w_kr = w_in[:, o3:]
    win = jnp.concatenate([w_in[:, :o3], w_kr, zpad(d, LANES - ROPE_DIM),
                           _rot_half(w_kr), zpad(d, LANES - ROPE_DIM)], axis=1)

    q_rope = w_uq[:, :, NOPE_DIM:]
    q_nope = w_uq[:, :, :NOPE_DIM]
    hz = jnp.zeros((q_lora, N_HEADS, HEAD_PAD - ROPE_DIM - NOPE_DIM), F32)
    wq_a = jnp.concatenate([q_rope, q_nope, hz], axis=-1).reshape(q_lora, N_HEADS * HEAD_PAD)
    wq_b = jnp.concatenate([_rot_half(q_rope), jnp.zeros((q_lora, N_HEADS, HEAD_PAD - ROPE_DIM), F32)],
                           axis=-1).reshape(q_lora, N_HEADS * HEAD_PAD)
    wq = jnp.concatenate([wq_a, wq_b], axis=1)

    kz_lo = jnp.zeros((kv_lora, N_HEADS, ROPE_DIM), F32)
    kz_hi = jnp.zeros((kv_lora, N_HEADS, HEAD_PAD - ROPE_DIM - NOPE_DIM), F32)
    wk = jnp.concatenate([kz_lo, w_uk, kz_hi], axis=-1).reshape(kv_lora, N_HEADS * HEAD_PAD)
    wkv = jnp.concatenate([wk, w_uv.reshape(kv_lora, N_HEADS * V_DIM)], axis=1)
    wabs = jnp.transpose(jnp.concatenate([kz_lo, w_uk, kz_hi], axis=-1), (1, 2, 0))

    uv = jnp.transpose(w_uv, (1, 0, 2))
    uz = jnp.zeros_like(uv)
    even = (jnp.arange(N_HEADS) % 2 == 0)[:, None, None]
    wuvp = jnp.where(even, jnp.concatenate([uv, uz], axis=-1), jnp.concatenate([uz, uv], axis=-1))

    wpool = jax.scipy.linalg.block_diag(*[w_pool[g] for g in range(n_groups)])

    bf = lambda a: a.astype(BF16)
    return dict(
        win=bf(win), wq=bf(wq), wkv=bf(wkv), wabs=bf(wabs), wuvp=bf(wuvp), wpool=bf(wpool),
        wout=bf(w_out), wup=bf(w_up), wdown=bf(w_down),
        gmix=g_mix.reshape(1, -1), gq=g_q.reshape(1, -1), gkv=g_kv.reshape(1, -1),
        pscale=pool_scale.reshape(1, -1), gmlp=g_mlp.reshape(1, -1), gfin=g_final.reshape(1, -1),
    )


def _rope_tables(positions):
    inv = ROPE_BASE ** (-jnp.arange(0, ROPE_DIM, 2, dtype=F32) / ROPE_DIM)
    ang = positions[:, None] * inv[None, :]
    cos2 = jnp.tile(jnp.cos(ang), (1, 2))
    sin2 = jnp.tile(jnp.sin(ang), (1, 2))
    n = positions.shape[0]
    qs = SOFTMAX_SCALE * LOG2E
    t_qa = jnp.concatenate([cos2, jnp.ones((n, NOPE_DIM), F32),
                            jnp.zeros((n, HEAD_PAD - ROPE_DIM - NOPE_DIM), F32)], axis=1) * qs
    zr = jnp.zeros((n, HEAD_PAD - ROPE_DIM), F32)
    t_qb = jnp.concatenate([sin2, zr], axis=1) * qs
    t_ka = jnp.concatenate([cos2, zr], axis=1)
    t_kb = jnp.concatenate([sin2, zr], axis=1)
    return jnp.concatenate([t_qa, t_qb, t_ka, t_kb], axis=1)


def _seq_tile(s, target):
    t = min(s, target)
    while s % t:
        t //= 2
    return t


def kernel(x_prompt, x_sample, cache_latent, cache_krope, state_pool, page_table, c_prompt, c_sample,
           w_mod, b_mod, g_mix, w_in, g_q, w_uq, g_kv, w_uk, w_uv, w_pool, pool_scale, w_out,
           g_mlp, w_up, w_down, g_final):
    bp, sp, d = x_prompt.shape
    db, ss, _ = x_sample.shape
    depth = w_mod.shape[0]
    n_pages = page_table.shape[1]
    page = cache_latent.shape[2]
    past = n_pages * page
    pool_w = state_pool.shape[-1]
    kv_lora = g_kv.shape[-1]

    tab_p = _rope_tables(jnp.arange(sp, dtype=F32))
    tab_s = _rope_tables(jnp.arange(ss, dtype=F32) + float(past))
    ts_p = _seq_tile(sp, 512)
    tb_s = _seq_tile(db, max(1, 512 // ss))

    xp, xs = x_prompt, x_sample
    outs = [[] for _ in range(6)]
    for l in range(depth):
        w = _layer_weights(w_in[l], g_mix[l], g_q[l], w_uq[l], g_kv[l], w_uk[l], w_uv[l], w_pool[l],
                           pool_scale[l], w_out[l], g_mlp[l], w_up[l], w_down[l], g_final)
        mod = _modulation(jnp.concatenate([c_prompt, c_sample], axis=0), w_mod[l], b_mod[l])
        mod = mod.reshape(bp + db, N_MOD, d)
        mod_p, mod_s = mod[:bp], mod[bp:]

        hist_p = jnp.zeros((bp, HIST_PAD, pool_w), F32)
        q, kcat, v, lat_p, kr_p, pool_p, nh_p = _pre(
            xp, mod_p, hist_p, tab_p, w, pos0=0.0, absorb=False, tb=1, ts=_seq_tile(sp, 1024))
        att_p = _attention(q, kcat, v, tq=_seq_tile(sp, 512), heads=4)
        final = l == depth - 1

        hist_s = jnp.pad(state_pool[l], ((0, 0), (HIST_PAD - POOL_HIST, 0), (0, 0)))
        qs, qlat, lat_s, kr_s, pool_s, nh_s = _pre(
            xs, mod_s, hist_s, tab_s, w, pos0=float(past), absorb=True, tb=tb_s, ts=ss)
        rows = ss * N_HEADS
        xp, o_lat = _post_prompt_with_decode(
            xp, mod_p, pool_p, att_p, w, page_table,
            qlat.reshape(db, rows, kv_lora),
            qs.reshape(db, rows, HEAD_PAD),
            lat_s, jnp.swapaxes(kr_s, 1, 2),
            cache_latent[l], jnp.swapaxes(cache_krope[l], 1, 2), final=final, ts=ts_p)
        xs = _post_sample(xs, mod_s, pool_s, o_lat.reshape(db, N_HEADS, ss, kv_lora), w,
                          final=final, tb=tb_s, ts=ss)

        for lst, val in zip(outs, (lat_p, jnp.swapaxes(kr_p, 1, 2), nh_p[:, HIST_PAD - POOL_HIST:],
                                   lat_s, kr_s, nh_s[:, HIST_PAD - POOL_HIST:])):
            lst.append(val)
    return (xp, xs) + tuple(jnp.stack(o) for o in outs)
```

```python
import functools
import math

import jax
import jax.numpy as jnp
from jax import lax
from jax.experimental import pallas as pl
from jax.experimental.pallas import tpu as pltpu

F32 = jnp.float32
BF16 = jnp.bfloat16

N_HEADS = 8
NOPE_DIM = 64
ROPE_DIM = 32
V_DIM = 64
POOL_WINDOWS = (2, 4, 8, 16)
POOL_HIST = max(POOL_WINDOWS) - 1
HIST_PAD = POOL_HIST + 1
assert all(w & (w - 1) == 0 for w in POOL_WINDOWS), "window sums are built by doubling"
N_MOD = 6
ROPE_BASE = 10000.0
EPS = 1e-6
NEG_INF = -1e30
SOFTMAX_SCALE = 1.0 / math.sqrt(NOPE_DIM + ROPE_DIM)
LOG2E = math.log2(math.e)

LANES = 128
SUBLANES = 8
VMEM_LIMIT = 56 * 1024 * 1024
VMEM_LIMIT_MERGED = 62 * 1024 * 1024

HEAD_PAD = LANES


def _params(sem, vmem=VMEM_LIMIT):
    return pltpu.CompilerParams(dimension_semantics=sem, vmem_limit_bytes=vmem)


def _nt_dot(a, b):
    return lax.dot_general(a, b, (((1,), (1,)), ((), ())), preferred_element_type=F32)


def _dot(a, b):
    return jnp.dot(a, b, preferred_element_type=F32)


def _rms(x, g):
    return x * lax.rsqrt(jnp.mean(x * x, axis=-1, keepdims=True) + EPS) * g


def _mod_kernel(c_ref, w_ref, b_ref, o_ref):
    c = c_ref[...]
    sc = c * (1.0 / (1.0 + jnp.exp(-c)))
    o_ref[...] = _dot(sc.astype(BF16), w_ref[...].astype(BF16)) + b_ref[...]


def _modulation(c, w_mod, b_mod):
    n, d = c.shape
    nout = w_mod.shape[1]
    tn = d
    return pl.pallas_call(
        _mod_kernel,
        grid=(nout // tn,),
        in_specs=[
            pl.BlockSpec((n, d), lambda j: (0, 0)),
            pl.BlockSpec((d, tn), lambda j: (0, j)),
            pl.BlockSpec((1, tn), lambda j: (0, j)),
        ],
        out_specs=pl.BlockSpec((n, tn), lambda j: (0, j)),
        out_shape=jax.ShapeDtypeStruct((n, nout), F32),
        compiler_params=_params(("arbitrary",)),
        name="mod",
    )(c, w_mod, b_mod.reshape(1, nout))


def _pre_kernel(*refs, pos0, absorb, pool_w, q_lora, kv_lora, n_split):
    (x_ref, mod_ref, hist_ref, tab_ref, win_ref, wq_ref, wkv_ref, wpool_ref,
     gmix_ref, gq_ref, gkv_ref, pscale_ref) = refs[:12]
    if absorb:
        (q_ref, qlat_ref, lat_ref, kr_ref, pool_ref, nh_ref, ext_ref) = refs[12:]
    else:
        (q_ref, kcat_ref, v_ref, lat_ref, kr_ref, pool_ref, nh_ref, ext_ref) = refs[12:]
    si = pl.program_id(1)
    tb, ts, d = x_ref.shape
    hq = N_HEADS * HEAD_PAD
    shift1 = mod_ref[:, 0:1, :]
    scale1 = mod_ref[:, 1:2, :]
    o1 = pool_w
    o2 = o1 + q_lora
    o3 = o2 + kv_lora
    gw = pool_w // len(POOL_WINDOWS)

    @pl.when(si == 0)
    def _():
        ext_ref[:, 0:HIST_PAD, :] = hist_ref[...]

    def rows_block(r0, n):
        rs = slice(r0, r0 + n)
        r = tb * n
        e0 = HIST_PAD + r0
        val = {}

        def project():
            h = _rms(x_ref[:, rs, :], gmix_ref[...]) * (1.0 + scale1) + shift1
            proj = _dot(h.reshape(r, d).astype(BF16), win_ref[...])
            val["u"] = proj[:, :o1].reshape(tb, n, pool_w)
            ext_ref[:, e0:e0 + n, :] = val["u"]
            val["cq"] = proj[:, o1:o2]
            val["ckv"] = proj[:, o2:o3]
            val["kr_a"] = proj[:, o3:o3 + LANES].reshape(tb, n, LANES)
            val["kr_b"] = proj[:, o3 + LANES:o3 + 2 * LANES].reshape(tb, n, LANES)

        def queries():
            t_qa = tab_ref[rs, 0 * LANES:1 * LANES]
            t_qb = tab_ref[rs, 1 * LANES:2 * LANES]
            qab = _dot(_rms(val["cq"], gq_ref[...]).astype(BF16), wq_ref[...])
            per_group = LANES // ROPE_DIM
            for hd in range(N_HEADS):
                lo = hd * HEAD_PAD
                qa = qab[:, lo:lo + HEAD_PAD].reshape(tb, n, HEAD_PAD)
                grp = hq + hd // per_group * LANES
                qb = qab[:, grp:grp + LANES]
                off = hd % per_group * ROPE_DIM
                if off:
                    qb = pltpu.roll(qb, LANES - off, 1)
                qh = qa * t_qa + qb.reshape(tb, n, HEAD_PAD) * t_qb
                if absorb:
                    q_ref[:, hd, rs, :] = qh
                    ql = _dot(qh.reshape(r, HEAD_PAD).astype(BF16), wkv_ref[hd])
                    qlat_ref[:, hd, rs, :] = ql.reshape(tb, n, kv_lora)
                else:
                    q_ref[:, rs, lo:lo + HEAD_PAD] = qh.astype(BF16)

        def keys():
            lat = _rms(val["ckv"], gkv_ref[...])
            lat_ref[:, rs, :] = lat.reshape(tb, n, kv_lora)
            kr128 = (val["kr_a"] * tab_ref[rs, 2 * LANES:3 * LANES]
                     + val["kr_b"] * tab_ref[rs, 3 * LANES:4 * LANES])
            if absorb:
                kr_ref[:, rs, :] = kr128[:, :, :ROPE_DIM]
            else:
                for i in range(tb):
                    kr_ref[i, :, rs] = jnp.transpose(kr128[i])[:ROPE_DIM, :]
                kv = _dot(lat.astype(BF16), wkv_ref[...])
                for hd in range(N_HEADS):
                    lo = hd * HEAD_PAD
                    kcat_ref[:, rs, lo:lo + HEAD_PAD] = (
                        kv[:, lo:lo + HEAD_PAD].reshape(tb, n, HEAD_PAD) + kr128).astype(BF16)
                v_ref[:, rs, :] = kv[:, hq:].reshape(tb, n, N_HEADS * V_DIM).astype(BF16)

        def pooling():
            pos = (pos0 + (si * ts + r0).astype(F32)
                   + lax.broadcasted_iota(jnp.int32, (1, n, gw), 1).astype(F32))
            parts = []
            for g, w in enumerate(POOL_WINDOWS):
                lo = g * gw
                acc = ext_ref[:, e0 - HIST_PAD:e0 + n, lo:lo + gw]
                k = 1
                while k < w:
                    acc = acc + pltpu.roll(acc, k, 1)
                    k *= 2
                acc = acc[:, HIST_PAD:, :]
                inv = 1.0 / jnp.minimum(float(w), pos + 1.0)
                parts.append(acc * inv - val["u"][:, :, lo:lo + gw])
            dpool = jnp.concatenate(parts, axis=-1).reshape(r, pool_w).astype(BF16)
            y = _dot(dpool, wpool_ref[...]) * pscale_ref[...]
            pool_ref[:, rs, :] = y.reshape(tb, n, pool_w).astype(BF16)

        return [project, pooling, queries, keys]

    blocks = [rows_block(blk * (ts // n_split), ts // n_split) for blk in range(n_split)]
    for stage_of_blocks in zip(*blocks):
        for stage in stage_of_blocks:
            stage()

    tail = ext_ref[:, ts:ts + HIST_PAD, :]
    nh_ref[...] = tail
    ext_ref[:, 0:HIST_PAD, :] = tail


def _pre(x, mod, hist16, tab, w, *, pos0, absorb, tb, ts):
    b, s, d = x.shape
    pool_w = hist16.shape[-1]
    q_lora = w["gq"].shape[-1]
    kv_lora = w["gkv"].shape[-1]
    hq = N_HEADS * HEAD_PAD
    grid = (b // tb, s // ts)
    tile = lambda width: pl.BlockSpec((tb, ts, width), lambda i, j: (i, j, 0))
    full = lambda a: pl.BlockSpec(a.shape, lambda i, j: (0,) * a.ndim)
    wkv = w["wabs"] if absorb else w["wkv"]
    in_specs = [
        tile(d),
        pl.BlockSpec((tb, N_MOD, d), lambda i, j: (i, 0, 0)),
        pl.BlockSpec((tb, HIST_PAD, pool_w), lambda i, j: (i, 0, 0)),
        pl.BlockSpec((ts, 4 * LANES), lambda i, j: (j, 0)),
        full(w["win"]), full(w["wq"]), full(wkv), full(w["wpool"]),
        full(w["gmix"]), full(w["gq"]), full(w["gkv"]), full(w["pscale"]),
    ]
    sds = jax.ShapeDtypeStruct
    if absorb:
        heads = lambda width: pl.BlockSpec((tb, N_HEADS, ts, width), lambda i, j: (i, 0, j, 0))
        out_shape = [sds((b, N_HEADS, s, HEAD_PAD), F32), sds((b, N_HEADS, s, kv_lora), F32)]
        out_specs = [heads(HEAD_PAD), heads(kv_lora)]
    else:
        out_shape = [sds((b, s, hq), BF16), sds((b, s, hq), BF16), sds((b, s, N_HEADS * V_DIM), BF16)]
        out_specs = [tile(hq), tile(hq), tile(N_HEADS * V_DIM)]
    if absorb:
        kr_shape, kr_spec = sds((b, s, ROPE_DIM), F32), tile(ROPE_DIM)
    else:
        kr_shape = sds((b, ROPE_DIM, s), F32)
        kr_spec = pl.BlockSpec((tb, ROPE_DIM, ts), lambda i, j: (i, 0, j))
    out_shape += [sds((b, s, kv_lora), F32), kr_shape, sds((b, s, pool_w), BF16),
                  sds((b, HIST_PAD, pool_w), F32)]
    out_specs += [tile(kv_lora), kr_spec, tile(pool_w),
                  pl.BlockSpec((tb, HIST_PAD, pool_w), lambda i, j: (i, 0, 0))]
    n_split = ts // (2 * LANES) if ts % (2 * LANES) == 0 else 1
    kern = functools.partial(_pre_kernel, pos0=float(pos0), absorb=absorb, pool_w=pool_w,
                             q_lora=q_lora, kv_lora=kv_lora, n_split=n_split)
    return pl.pallas_call(
        kern,
        grid=grid,
        in_specs=in_specs,
        out_specs=out_specs,
        out_shape=out_shape,
        scratch_shapes=[pltpu.VMEM((tb, HIST_PAD + ts, pool_w), F32)],
        compiler_params=_params(("arbitrary", "arbitrary")),
        name="pre_sample" if absorb else "pre_prompt",
    )(x, mod, hist16, tab, w["win"], w["wq"], wkv, w["wpool"], w["gmix"], w["gq"], w["gkv"], w["pscale"])


def _attn_kernel(q_ref, k_ref, v_ref, o_ref, *, tq):
    s = q_ref.shape[1]
    th = tq // 2
    first_head = lax.broadcasted_iota(jnp.int32, (tq, 2 * V_DIM), 1) < V_DIM
    rowmax = lambda a: jnp.max(a, axis=-1, keepdims=True)
    rowsum = lambda a: jnp.sum(a, axis=-1, keepdims=True)
    heads = q_ref.shape[2] // HEAD_PAD
    units = [(qi, hh) for qi in range(s // tq) for hh in range(heads)]
    val = [dict() for _ in units]

    def scores(u):
        qi, hh = units[u]
        q0 = qi * tq
        hs = slice(hh * HEAD_PAD, (hh + 1) * HEAD_PAD)
        s_d = []
        for j in range(2):
            nk = (j + 1) * th
            r0 = q0 + j * th
            row = lax.broadcasted_iota(jnp.int32, (th, nk), 0) + j * th
            col = lax.broadcasted_iota(jnp.int32, (th, nk), 1)
            s_d.append(jnp.where(row >= col, _nt_dot(q_ref[0, r0:r0 + th, hs], k_ref[0, q0:q0 + nk, hs]),
                                 NEG_INF))
        val[u]["s_d"] = s_d
        val[u]["s_o"] = _nt_dot(q_ref[0, q0:q0 + tq, hs], k_ref[0, 0:q0, hs]) if qi > 0 else None

    def softmax(u):
        s_d, s_o = val[u]["s_d"], val[u]["s_o"]
        m = [rowmax(s_d[j]) for j in range(2)]
        if s_o is not None:
            m = [jnp.maximum(m[j], rowmax(s_o[j * th:(j + 1) * th])) for j in range(2)]
            p_o = jnp.concatenate([jnp.exp2(s_o[j * th:(j + 1) * th] - m[j]) for j in range(2)], axis=0)
            val[u]["l_o"] = rowsum(p_o)
            val[u]["p_o"] = p_o.astype(BF16)
        p_d = [jnp.exp2(s_d[j] - m[j]) for j in range(2)]
        val[u]["l_d"] = [rowsum(p) for p in p_d]
        val[u]["p_d"] = [p.astype(BF16) for p in p_d]

    def values(u):
        qi, hh = units[u]
        q0 = qi * tq
        vs = slice(hh // 2 * 2 * V_DIM, (hh // 2 + 1) * 2 * V_DIM)
        acc = jnp.concatenate([_dot(val[u]["p_d"][j], v_ref[0, q0:q0 + (j + 1) * th, vs]) for j in range(2)],
                              axis=0)
        l = jnp.concatenate(val[u]["l_d"], axis=0)
        if val[u]["s_o"] is not None:
            acc = acc + _dot(val[u]["p_o"], v_ref[0, 0:q0, vs])
            l = l + val[u]["l_o"]
        out = acc * (1.0 / l)
        if hh % 2 == 0:
            val[u]["out"] = out
        else:
            o_ref[0, q0:q0 + tq, vs] = jnp.where(first_head, val[u - 1]["out"], out).astype(BF16)

    scores(0)
    for u in range(len(units)):
        if u + 1 < len(units):
            scores(u + 1)
        softmax(u)
        values(u)


def _attention(q, kcat, v, *, tq, heads):
    b, s, _ = q.shape
    return pl.pallas_call(
        functools.partial(_attn_kernel, tq=tq),
        grid=(b, N_HEADS // heads),
        in_specs=[
            pl.BlockSpec((1, s, heads * HEAD_PAD), lambda i, p: (i, 0, p)),
            pl.BlockSpec((1, s, heads * HEAD_PAD), lambda i, p: (i, 0, p)),
            pl.BlockSpec((1, s, heads * V_DIM), lambda i, p: (i, 0, p)),
        ],
        out_specs=pl.BlockSpec((1, s, heads * V_DIM), lambda i, p: (i, 0, p)),
        out_shape=jax.ShapeDtypeStruct((b, s, N_HEADS * V_DIM), BF16),
        compiler_params=_params(("arbitrary", "arbitrary")),
        name="attn_prompt",
    )(q, kcat, v)


def _decode_ops(pt_ref, qlat_ref, qpad_ref, latn_ref, krn_ref, clat_hbm, ckr_hbm, o_ref,
                latbuf, krbuf, s_ref, sem, *, n_pages, page, chunk, s_new):
    t_past = n_pages * page
    t_all = t_past + page
    rows = qlat_ref.shape[1]
    spans = [(lo, min(chunk, t_all - lo)) for lo in list(range(0, t_past, chunk)) + [t_past]]

    def page_copies(bb, sl, p):
        pg = pt_ref[bb, p]
        dst = pl.ds(p * page, page)
        return (pltpu.make_async_copy(clat_hbm.at[pg], latbuf.at[sl, dst], sem.at[0, sl]),
                pltpu.make_async_copy(ckr_hbm.at[pg], krbuf.at[sl, :, dst], sem.at[1, sl]))

    def start_pages(bb, sl):
        for p in range(n_pages):
            for c in page_copies(bb, sl, p):
                c.start(priority=p % 2)

    def wait_pages(bb, sl):
        for p in range(n_pages):
            for c in page_copies(bb, sl, p):
                c.wait()

    def attend_stages(sl):
        val = {}

        def new_page():
            latbuf[sl, t_past:t_all, :] = jnp.zeros((page, latbuf.shape[-1]), F32)
            krbuf[sl, :, t_past:t_all] = jnp.zeros((ROPE_DIM, page), F32)
            latbuf[sl, t_past:t_past + s_new, :] = latn_ref[sl]
            krbuf[sl, :, t_past:t_past + s_new] = krn_ref[sl]

        def scores(lo, n):
            def run():
                lat_c = latbuf[sl, lo:lo + n, :].astype(BF16)
                s_ref[:, lo:lo + n] = (
                    _nt_dot(qlat_ref[sl].astype(BF16), lat_c)
                    + _dot(qpad_ref[sl][:, :ROPE_DIM].astype(BF16), krbuf[sl, :, lo:lo + n].astype(BF16)))
            return run

        def softmax():
            tok = lax.rem(lax.broadcasted_iota(jnp.int32, (rows, page), 0), s_new)
            col = lax.broadcasted_iota(jnp.int32, (rows, page), 1)
            s_ref[:, t_past:t_all] = jnp.where(col <= tok, s_ref[:, t_past:t_all], NEG_INF)
            s_all = s_ref[...]
            m = jnp.max(s_all, axis=-1, keepdims=True)
            p = jnp.exp2(s_all - m)
            val["inv_l"] = 1.0 / jnp.sum(p, axis=-1, keepdims=True)
            val["p"] = p.astype(BF16)
            val["out"] = None

        def values(lo, n, last):
            def run():
                t = _dot(val["p"][:, lo:lo + n], latbuf[sl, lo:lo + n, :].astype(BF16)) * val["inv_l"]
                val["out"] = t if val["out"] is None else val["out"] + t
                if last:
                    o_ref[sl] = val["out"]
            return run

        return ([new_page] + [scores(lo, n) for lo, n in spans] + [softmax]
                + [values(lo, n, i == len(spans) - 1) for i, (lo, n) in enumerate(spans)])

    return start_pages, wait_pages, attend_stages


def _post_dec_kernel(pt_ref, x_ref, mod_ref, pool_ref, att_ref, wout_ref, wuv_ref, wup_ref, wdown_ref,
                     gmlp_ref, gfin_ref, qlat_ref, qpad_ref, latn_ref, krn_ref, clat_hbm, ckr_hbm,
                     y_ref, o_ref, latbuf, krbuf, s_ref, sem, *, final, pool_w, kv_lora, ff_chunk,
                     n_pages, page, chunk, s_new):
    step = pl.program_id(0) * pl.num_programs(1) + pl.program_id(1)
    n_steps = pl.num_programs(0) * pl.num_programs(1)
    n_slots = latbuf.shape[0]
    start_pages, wait_pages, attend_stages = _decode_ops(
        pt_ref, qlat_ref, qpad_ref, latn_ref, krn_ref, clat_hbm, ckr_hbm, o_ref, latbuf, krbuf, s_ref, sem,
        n_pages=n_pages, page=page, chunk=chunk, s_new=s_new)
    tile_stages = _post_stages(x_ref, mod_ref, pool_ref, att_ref, wout_ref, wuv_ref, wup_ref, wdown_ref,
                               gmlp_ref, gfin_ref, y_ref, absorb=False, final=final, pool_w=pool_w,
                               kv_lora=kv_lora, ff_chunk=ff_chunk)
    b0 = step * n_slots
    last_b0 = (n_steps - 1) * n_slots

    @pl.when(step == 0)
    def _():
        for j in range(n_slots):
            start_pages(j, j)

    attn_stages = []
    for j in range(n_slots):
        attn_stages.append(functools.partial(wait_pages, b0 + j, j))
        attn_stages += attend_stages(j)
        attn_stages.append(functools.partial(start_pages, jnp.minimum(b0 + n_slots + j, last_b0 + j), j))

    done = 0
    for i, stage in enumerate(tile_stages):
        upto = (i + 1) * len(attn_stages) // len(tile_stages)
        for a in attn_stages[done:upto]:
            a()
        done = upto
        stage()

    @pl.when(step == n_steps - 1)
    def _():
        for j in range(n_slots):
            wait_pages(last_b0 + j, j)


def _post_prompt_with_decode(x, mod, pool, att, w, page_table, qlat, qpad, lat_new, kr_new,
                             cache_lat, cache_kr, *, final, ts):
    b, s, d = x.shape
    pool_w = pool.shape[-1]
    db, rows, kv_lora = qlat.shape
    n_pages = page_table.shape[1]
    page = cache_lat.shape[1]
    s_new = lat_new.shape[1]
    t_past = n_pages * page
    nj = s // ts
    n_steps = b * nj
    assert db % n_steps == 0, "sample batches are spread evenly over the prompt tiles"
    n_slots = db // n_steps
    tile = lambda width: pl.BlockSpec((1, ts, width), lambda i, j, pt: (i, j, 0))
    const = lambda a: pl.BlockSpec(a.shape, lambda i, j, pt: (0,) * a.ndim, pipeline_mode=pl.Buffered(1))
    per_step = lambda *shape: pl.BlockSpec((n_slots,) + shape, lambda i, j, pt: (i * nj + j, 0, 0))
    kern = functools.partial(_post_dec_kernel, final=final, pool_w=pool_w, kv_lora=kv_lora,
                             ff_chunk=min(1024, w["wup"].shape[1]), n_pages=n_pages, page=page,
                             chunk=min(2048, t_past), s_new=s_new)
    grid_spec = pltpu.PrefetchScalarGridSpec(
        num_scalar_prefetch=1,
        grid=(b, nj),
        in_specs=[
            tile(d),
            pl.BlockSpec((1, N_MOD, d), lambda i, j, pt: (i, 0, 0)),
            tile(pool_w),
            tile(att.shape[-1]),
            const(w["wout"]), const(w["wuvp"]), const(w["wup"]), const(w["wdown"]),
            const(w["gmlp"]), const(w["gfin"]),
            per_step(rows, kv_lora), per_step(rows, HEAD_PAD), per_step(s_new, kv_lora),
            per_step(ROPE_DIM, s_new),
            pl.BlockSpec(memory_space=pl.ANY),
            pl.BlockSpec(memory_space=pl.ANY),
        ],
        out_specs=[tile(d), per_step(rows, kv_lora)],
        scratch_shapes=[
            pltpu.VMEM((n_slots, t_past + page, kv_lora), F32),
            pltpu.VMEM((n_slots, ROPE_DIM, t_past + page), F32),
            pltpu.VMEM((rows, t_past + page), F32),
            pltpu.SemaphoreType.DMA((2, n_slots)),
        ],
    )
    return pl.pallas_call(
        kern,
        grid_spec=grid_spec,
        out_shape=[jax.ShapeDtypeStruct((b, s, d), F32), jax.ShapeDtypeStruct((db, rows, kv_lora), F32)],
        compiler_params=_params(("arbitrary", "arbitrary"), vmem=VMEM_LIMIT_MERGED),
        name="post_prompt_attn_sample",
    )(page_table, x, mod, pool, att, w["wout"], w["wuvp"], w["wup"], w["wdown"], w["gmlp"], w["gfin"],
      qlat, qpad, lat_new, kr_new, cache_lat, cache_kr)


def _post_kernel(x_ref, mod_ref, pool_ref, att_ref, wout_ref, wuv_ref, wup_ref, wdown_ref,
                 gmlp_ref, gfin_ref, y_ref, *, absorb, final, pool_w, kv_lora, ff_chunk):
    for stage in _post_stages(x_ref, mod_ref, pool_ref, att_ref, wout_ref, wuv_ref, wup_ref, wdown_ref,
                              gmlp_ref, gfin_ref, y_ref, absorb=absorb, final=final, pool_w=pool_w,
                              kv_lora=kv_lora, ff_chunk=ff_chunk):
        stage()


def _post_stages(x_ref, mod_ref, pool_ref, att_ref, wout_ref, wuv_ref, wup_ref, wdown_ref,
                 gmlp_ref, gfin_ref, y_ref, *, absorb, final, pool_w, kv_lora, ff_chunk):
    tb, ts, d = x_ref.shape
    r = tb * ts
    d_ff = wup_ref.shape[1]
    n_chunks = d_ff // ff_chunk
    val = {}

    def project():
        mix = _dot(pool_ref[...].reshape(r, pool_w), wout_ref[0:pool_w, :])
        if absorb:
            for p in range(N_HEADS // 2):
                o_pair = None
                for hh in range(2):
                    hd = 2 * p + hh
                    o_lat = att_ref[:, hd, :, :].reshape(r, kv_lora).astype(BF16)
                    t = _dot(o_lat, wuv_ref[hd])
                    o_pair = t if o_pair is None else o_pair + t
                lo = pool_w + p * 2 * V_DIM
                mix = mix + _dot(o_pair.astype(BF16), wout_ref[lo:lo + 2 * V_DIM, :])
        else:
            mix = mix + _dot(att_ref[...].reshape(r, N_HEADS * V_DIM), wout_ref[pool_w:, :])
        x1 = x_ref[...] + mod_ref[:, 2:3, :] * mix.reshape(tb, ts, d)
        val["x1"] = x1
        val["h2"] = (_rms(x1, gmlp_ref[...]) * (1.0 + mod_ref[:, 4:5, :]) + mod_ref[:, 3:4, :]
                     ).reshape(r, d).astype(BF16)
        val["mlp"] = None

    def mlp_up(c):
        def run():
            hid = jnp.maximum(_dot(val["h2"], wup_ref[:, c * ff_chunk:(c + 1) * ff_chunk]), 0.0)
            val["hid"] = (hid * hid).astype(BF16)
        return run

    def mlp_down(c):
        def run():
            t = _dot(val["hid"], wdown_ref[c * ff_chunk:(c + 1) * ff_chunk, :])
            val["mlp"] = t if val["mlp"] is None else val["mlp"] + t
            if c == n_chunks - 1:
                x2 = val["x1"] + mod_ref[:, 5:6, :] * val["mlp"].reshape(tb, ts, d)
                y_ref[...] = _rms(x2, gfin_ref[...]) if final else x2
        return run

    stages = [project]
    for c in range(n_chunks):
        stages += [mlp_up(c), mlp_down(c)]
    return stages


def _post_sample(x, mod, pool, o_lat, w, *, final, tb, ts):
    b, s, d = x.shape
    pool_w = pool.shape[-1]
    kv_lora = o_lat.shape[-1]
    tile = lambda width: pl.BlockSpec((tb, ts, width), lambda i, j: (i, j, 0))
    const = lambda a: pl.BlockSpec(a.shape, lambda i, j: (0,) * a.ndim, pipeline_mode=pl.Buffered(1))
    kern = functools.partial(_post_kernel, absorb=True, final=final, pool_w=pool_w, kv_lora=kv_lora,
                             ff_chunk=min(1024, w["wup"].shape[1]))
    return pl.pallas_call(
        kern,
        grid=(b // tb, s // ts),
        in_specs=[
            tile(d),
            pl.BlockSpec((tb, N_MOD, d), lambda i, j: (i, 0, 0)),
            tile(pool_w),
            pl.BlockSpec((tb, N_HEADS, ts, kv_lora), lambda i, j: (i, 0, j, 0)),
            const(w["wout"]), const(w["wuvp"]), const(w["wup"]), const(w["wdown"]),
            const(w["gmlp"]), const(w["gfin"]),
        ],
        out_specs=tile(d),
        out_shape=jax.ShapeDtypeStruct((b, s, d), F32),
        compiler_params=_params(("arbitrary", "arbitrary")),
        name="post_sample",
    )(x, mod, pool, o_lat, w["wout"], w["wuvp"], w["wup"], w["wdown"], w["gmlp"], w["gfin"])


def _rot_half(wr):
    half = ROPE_DIM // 2
    return jnp.concatenate([-wr[..., half:], wr[..., :half]], axis=-1)


def _layer_weights(w_in, g_mix, g_q, w_uq, g_kv, w_uk, w_uv, w_pool, pool_scale, w_out, g_mlp,
                   w_up, w_down, g_final):
    d, _ = w_in.shape
    q_lora = g_q.shape[-1]
    kv_lora = g_kv.shape[-1]
    n_groups, gw, _ = w_pool.shape
    pool_w = n_groups * gw
    o3 = pool_w + q_lora + kv_lora
    zpad = lambda rows, cols: jnp.zeros((rows, cols), F32)

    w_kr = w_in[:, o3:]
    win = jnp.concatenate([w_in[:, :o3], w_kr, zpad(d, LANES - ROPE_DIM),
                           _rot_half(w_kr), zpad(d, LANES - ROPE_DIM)], axis=1)

    q_rope = w_uq[:, :, NOPE_DIM:]
    q_nope = w_uq[:, :, :NOPE_DIM]
    hz = jnp.zeros((q_lora, N_HEADS, HEAD_PAD - ROPE_DIM - NOPE_DIM), F32)
    wq_a = jnp.concatenate([q_rope, q_nope, hz], axis=-1).reshape(q_lora, N_HEADS * HEAD_PAD)
    wq_b = _rot_half(q_rope).reshape(q_lora, N_HEADS * ROPE_DIM)
    wq = jnp.concatenate([wq_a, wq_b], axis=1)

    kz_lo = jnp.zeros((kv_lora, N_HEADS, ROPE_DIM), F32)
    kz_hi = jnp.zeros((kv_lora, N_HEADS, HEAD_PAD - ROPE_DIM - NOPE_DIM), F32)
    wk = jnp.concatenate([kz_lo, w_uk, kz_hi], axis=-1).reshape(kv_lora, N_HEADS * HEAD_PAD)
    wkv = jnp.concatenate([wk, w_uv.reshape(kv_lora, N_HEADS * V_DIM)], axis=1)
    wabs = jnp.transpose(jnp.concatenate([kz_lo, w_uk, kz_hi], axis=-1), (1, 2, 0))

    uv = jnp.transpose(w_uv, (1, 0, 2))
    uz = jnp.zeros_like(uv)
    even = (jnp.arange(N_HEADS) % 2 == 0)[:, None, None]
    wuvp = jnp.where(even, jnp.concatenate([uv, uz], axis=-1), jnp.concatenate([uz, uv], axis=-1))

    wpool = jax.scipy.linalg.block_diag(*[w_pool[g] for g in range(n_groups)])

    bf = lambda a: a.astype(BF16)
    return dict(
        win=bf(win), wq=bf(wq), wkv=bf(wkv), wabs=bf(wabs), wuvp=bf(wuvp), wpool=bf(wpool),
        wout=bf(w_out), wup=bf(w_up), wdown=bf(w_down),
        gmix=g_mix.reshape(1, -1), gq=g_q.reshape(1, -1), gkv=g_kv.reshape(1, -1),
        pscale=pool_scale.reshape(1, -1), gmlp=g_mlp.reshape(1, -1), gfin=g_final.reshape(1, -1),
    )


def _rope_tables(positions):
    inv = ROPE_BASE ** (-jnp.arange(0, ROPE_DIM, 2, dtype=F32) / ROPE_DIM)
    ang = positions[:, None] * inv[None, :]
    cos2 = jnp.tile(jnp.cos(ang), (1, 2))
    sin2 = jnp.tile(jnp.sin(ang), (1, 2))
    n = positions.shape[0]
    qs = SOFTMAX_SCALE * LOG2E
    t_qa = jnp.concatenate([cos2, jnp.ones((n, NOPE_DIM), F32),
                            jnp.zeros((n, HEAD_PAD - ROPE_DIM - NOPE_DIM), F32)], axis=1) * qs
    zr = jnp.zeros((n, HEAD_PAD - ROPE_DIM), F32)
    t_qb = jnp.concatenate([sin2, zr], axis=1) * qs
    t_ka = jnp.concatenate([cos2, zr], axis=1)
    t_kb = jnp.concatenate([sin2, zr], axis=1)
    return jnp.concatenate([t_qa, t_qb, t_ka, t_kb], axis=1)


def _seq_tile(s, target):
    t = min(s, target)
    while s % t:
        t //= 2
    return t


def kernel(x_prompt, x_sample, cache_latent, cache_krope, state_pool, page_table, c_prompt, c_sample,
           w_mod, b_mod, g_mix, w_in, g_q, w_uq, g_kv, w_uk, w_uv, w_pool, pool_scale, w_out,
           g_mlp, w_up, w_down, g_final):
    bp, sp, d = x_prompt.shape
    db, ss, _ = x_sample.shape
    depth = w_mod.shape[0]
    n_pages = page_table.shape[1]
    page = cache_latent.shape[2]
    past = n_pages * page
    pool_w = state_pool.shape[-1]
    kv_lora = g_kv.shape[-1]

    tab_p = _rope_tables(jnp.arange(sp, dtype=F32))
    tab_s = _rope_tables(jnp.arange(ss, dtype=F32) + float(past))
    ts_p = _seq_tile(sp, 512)
    tb_s = _seq_tile(db, max(1, 512 // ss))

    xp, xs = x_prompt, x_sample
    outs = [[] for _ in range(6)]
    for l in range(depth):
        w = _layer_weights(w_in[l], g_mix[l], g_q[l], w_uq[l], g_kv[l], w_uk[l], w_uv[l], w_pool[l],
                           pool_scale[l], w_out[l], g_mlp[l], w_up[l], w_down[l], g_final)
        mod = _modulation(jnp.concatenate([c_prompt, c_sample], axis=0), w_mod[l], b_mod[l])
        mod = mod.reshape(bp + db, N_MOD, d)
        mod_p, mod_s = mod[:bp], mod[bp:]

        hist_p = jnp.zeros((bp, HIST_PAD, pool_w), F32)
        q, kcat, v, lat_p, kr_p, pool_p, nh_p = _pre(
            xp, mod_p, hist_p, tab_p, w, pos0=0.0, absorb=False, tb=1, ts=_seq_tile(sp, 1024))
        att_p = _attention(q, kcat, v, tq=_seq_tile(sp, 512), heads=4)
        final = l == depth - 1

        hist_s = jnp.pad(state_pool[l], ((0, 0), (HIST_PAD - POOL_HIST, 0), (0, 0)))
        qs, qlat, lat_s, kr_s, pool_s, nh_s = _pre(
            xs, mod_s, hist_s, tab_s, w, pos0=float(past), absorb=True, tb=tb_s, ts=ss)
        rows = ss * N_HEADS
        xp, o_lat = _post_prompt_with_decode(
            xp, mod_p, pool_p, att_p, w, page_table,
            qlat.reshape(db, rows, kv_lora),
            qs.reshape(db, rows, HEAD_PAD),
            lat_s, jnp.swapaxes(kr_s, 1, 2),
            cache_latent[l], jnp.swapaxes(cache_krope[l], 1, 2), final=final, ts=ts_p)
        xs = _post_sample(xs, mod_s, pool_s, o_lat.reshape(db, N_HEADS, ss, kv_lora), w,
                          final=final, tb=tb_s, ts=ss)

        for lst, val in zip(outs, (lat_p, jnp.swapaxes(kr_p, 1, 2), nh_p[:, HIST_PAD - POOL_HIST:],
                                   lat_s, kr_s, nh_s[:, HIST_PAD - POOL_HIST:])):
            lst.append(val)
    return (xp, xs) + tuple(jnp.stack(o) for o in outs)
```

```python
import functools
import math

import jax
import jax.numpy as jnp
from jax import lax
from jax.experimental import pallas as pl
from jax.experimental.pallas import tpu as pltpu

F32 = jnp.float32
BF16 = jnp.bfloat16

N_HEADS = 8
NOPE_DIM = 64
ROPE_DIM = 32
V_DIM = 64
POOL_WINDOWS = (2, 4, 8, 16)
POOL_HIST = max(POOL_WINDOWS) - 1
HIST_PAD = POOL_HIST + 1
assert all(w & (w - 1) == 0 for w in POOL_WINDOWS), "window sums are built by doubling"
N_MOD = 6
ROPE_BASE = 10000.0
EPS = 1e-6
NEG_INF = -1e30
SOFTMAX_SCALE = 1.0 / math.sqrt(NOPE_DIM + ROPE_DIM)
LOG2E = math.log2(math.e)

LANES = 128
VMEM_LIMIT = 56 * 1024 * 1024
VMEM_LIMIT_MERGED = 62 * 1024 * 1024

HEAD_PAD = LANES


def _params(sem, vmem=VMEM_LIMIT):
    return pltpu.CompilerParams(dimension_semantics=sem, vmem_limit_bytes=vmem)


def _nt_dot(a, b):
    return lax.dot_general(a, b, (((1,), (1,)), ((), ())), preferred_element_type=F32)


def _dot(a, b):
    return jnp.dot(a, b, preferred_element_type=F32)


def _rms(x, g):
    return x * lax.rsqrt(jnp.mean(x * x, axis=-1, keepdims=True) + EPS) * g


def _mod_kernel(c_ref, w_ref, b_ref, o_ref):
    c = c_ref[...]
    sc = c * (1.0 / (1.0 + jnp.exp(-c)))
    o_ref[...] = _dot(sc.astype(BF16), w_ref[...].astype(BF16)) + b_ref[...]


def _modulation(c, w_mod, b_mod):
    n, d = c.shape
    nout = w_mod.shape[1]
    tn = 2 * d if nout % (2 * d) == 0 else d
    return pl.pallas_call(
        _mod_kernel,
        grid=(nout // tn,),
        in_specs=[
            pl.BlockSpec((n, d), lambda j: (0, 0)),
            pl.BlockSpec((d, tn), lambda j: (0, j)),
            pl.BlockSpec((1, tn), lambda j: (0, j)),
        ],
        out_specs=pl.BlockSpec((n, tn), lambda j: (0, j)),
        out_shape=jax.ShapeDtypeStruct((n, nout), F32),
        compiler_params=_params(("arbitrary",)),
        name="mod",
    )(c, w_mod, b_mod.reshape(1, nout))


def _pre_kernel(*refs, pos0, absorb, pool_w, q_lora, kv_lora, n_split):
    (x_ref, mod_ref, hist_ref, tab_ref, win_ref, wq_ref, wkv_ref, wpool_ref,
     gmix_ref, gq_ref, gkv_ref, pscale_ref) = refs[:12]
    if absorb:
        (q_ref, qlat_ref, lat_ref, kr_ref, pool_ref, nh_ref, ext_ref) = refs[12:]
    else:
        (q_ref, kcat_ref, v_ref, lat_ref, kr_ref, pool_ref, nh_ref, ext_ref) = refs[12:]
    si = pl.program_id(1)
    tb, ts, d = x_ref.shape
    hq = N_HEADS * HEAD_PAD
    shift1 = mod_ref[:, 0:1, :]
    scale1 = mod_ref[:, 1:2, :]
    o1 = pool_w
    o2 = o1 + q_lora
    o3 = o2 + kv_lora
    gw = pool_w // len(POOL_WINDOWS)

    @pl.when(si == 0)
    def _():
        ext_ref[:, 0:HIST_PAD, :] = hist_ref[...]

    def rows_block(r0, n):
        rs = slice(r0, r0 + n)
        r = tb * n
        e0 = HIST_PAD + r0
        val = {}

        def project():
            h = _rms(x_ref[:, rs, :], gmix_ref[...]) * (1.0 + scale1) + shift1
            proj = _dot(h.reshape(r, d).astype(BF16), win_ref[...])
            val["u"] = proj[:, :o1].reshape(tb, n, pool_w)
            ext_ref[:, e0:e0 + n, :] = val["u"]
            val["cq"] = proj[:, o1:o2]
            val["ckv"] = proj[:, o2:o3]
            val["kr_a"] = proj[:, o3:o3 + LANES].reshape(tb, n, LANES)
            val["kr_b"] = proj[:, o3 + LANES:o3 + 2 * LANES].reshape(tb, n, LANES)

        def queries():
            t_qa = tab_ref[rs, 0 * LANES:1 * LANES]
            t_qb = tab_ref[rs, 1 * LANES:2 * LANES]
            qab = _dot(_rms(val["cq"], gq_ref[...]).astype(BF16), wq_ref[...])
            per_group = LANES // ROPE_DIM
            for hd in range(N_HEADS):
                lo = hd * HEAD_PAD
                qa = qab[:, lo:lo + HEAD_PAD].reshape(tb, n, HEAD_PAD)
                grp = hq + hd // per_group * LANES
                qb = qab[:, grp:grp + LANES]
                off = hd % per_group * ROPE_DIM
                if off:
                    qb = pltpu.roll(qb, LANES - off, 1)
                qh = qa * t_qa + qb.reshape(tb, n, HEAD_PAD) * t_qb
                if absorb:
                    q_ref[:, hd, rs, :] = qh
                    ql = _dot(qh.reshape(r, HEAD_PAD).astype(BF16), wkv_ref[hd])
                    qlat_ref[:, hd, rs, :] = ql.reshape(tb, n, kv_lora)
                else:
                    q_ref[:, rs, lo:lo + HEAD_PAD] = qh.astype(BF16)

        def keys():
            lat = _rms(val["ckv"], gkv_ref[...])
            lat_ref[:, rs, :] = lat.reshape(tb, n, kv_lora)
            kr128 = (val["kr_a"] * tab_ref[rs, 2 * LANES:3 * LANES]
                     + val["kr_b"] * tab_ref[rs, 3 * LANES:4 * LANES])
            if absorb:
                kr_ref[:, rs, :] = kr128[:, :, :ROPE_DIM]
            else:
                for i in range(tb):
                    kr_ref[i, :, rs] = jnp.transpose(kr128[i])[:ROPE_DIM, :]
                kv = _dot(lat.astype(BF16), wkv_ref[...])
                for hd in range(N_HEADS):
                    lo = hd * HEAD_PAD
                    kcat_ref[:, rs, lo:lo + HEAD_PAD] = (
                        kv[:, lo:lo + HEAD_PAD].reshape(tb, n, HEAD_PAD) + kr128).astype(BF16)
                v_ref[:, rs, :] = kv[:, hq:].reshape(tb, n, N_HEADS * V_DIM).astype(BF16)

        def pooling():
            pos = (pos0 + (si * ts + r0).astype(F32)
                   + lax.broadcasted_iota(jnp.int32, (1, n, gw), 1).astype(F32))
            parts = []
            for g, w in enumerate(POOL_WINDOWS):
                lo = g * gw
                acc = ext_ref[:, e0 - HIST_PAD:e0 + n, lo:lo + gw]
                k = 1
                while k < w:
                    acc = acc + pltpu.roll(acc, k, 1)
                    k *= 2
                acc = acc[:, HIST_PAD:, :]
                inv = 1.0 / jnp.minimum(float(w), pos + 1.0)
                parts.append(acc * inv - val["u"][:, :, lo:lo + gw])
            dpool = jnp.concatenate(parts, axis=-1).reshape(r, pool_w).astype(BF16)
            y = _dot(dpool, wpool_ref[...]) * pscale_ref[...]
            pool_ref[:, rs, :] = y.reshape(tb, n, pool_w).astype(BF16)

        return [project, pooling, queries, keys]

    blocks = [rows_block(blk * (ts // n_split), ts // n_split) for blk in range(n_split)]
    for stage_of_blocks in zip(*blocks):
        for stage in stage_of_blocks:
            stage()

    tail = ext_ref[:, ts:ts + HIST_PAD, :]
    nh_ref[...] = tail
    ext_ref[:, 0:HIST_PAD, :] = tail


def _pre(x, mod, hist16, tab, w, *, pos0, absorb, tb, ts):
    b, s, d = x.shape
    pool_w = hist16.shape[-1]
    q_lora = w["gq"].shape[-1]
    kv_lora = w["gkv"].shape[-1]
    hq = N_HEADS * HEAD_PAD
    grid = (b // tb, s // ts)
    tile = lambda width: pl.BlockSpec((tb, ts, width), lambda i, j: (i, j, 0))
    full = lambda a: pl.BlockSpec(a.shape, lambda i, j: (0,) * a.ndim)
    wkv = w["wabs"] if absorb else w["wkv"]
    in_specs = [
        tile(d),
        pl.BlockSpec((tb, N_MOD, d), lambda i, j: (i, 0, 0)),
        pl.BlockSpec((tb, HIST_PAD, pool_w), lambda i, j: (i, 0, 0)),
        pl.BlockSpec((ts, 4 * LANES), lambda i, j: (j, 0)),
        full(w["win"]), full(w["wq"]), full(wkv), full(w["wpool"]),
        full(w["gmix"]), full(w["gq"]), full(w["gkv"]), full(w["pscale"]),
    ]
    sds = jax.ShapeDtypeStruct
    if absorb:
        heads = lambda width: pl.BlockSpec((tb, N_HEADS, ts, width), lambda i, j: (i, 0, j, 0))
        out_shape = [sds((b, N_HEADS, s, HEAD_PAD), F32), sds((b, N_HEADS, s, kv_lora), F32)]
        out_specs = [heads(HEAD_PAD), heads(kv_lora)]
    else:
        out_shape = [sds((b, s, hq), BF16), sds((b, s, hq), BF16), sds((b, s, N_HEADS * V_DIM), BF16)]
        out_specs = [tile(hq), tile(hq), tile(N_HEADS * V_DIM)]
    if absorb:
        kr_shape, kr_spec = sds((b, s, ROPE_DIM), F32), tile(ROPE_DIM)
    else:
        kr_shape = sds((b, ROPE_DIM, s), F32)
        kr_spec = pl.BlockSpec((tb, ROPE_DIM, ts), lambda i, j: (i, 0, j))
    out_shape += [sds((b, s, kv_lora), F32), kr_shape, sds((b, s, pool_w), BF16),
                  sds((b, HIST_PAD, pool_w), F32)]
    out_specs += [tile(kv_lora), kr_spec, tile(pool_w),
                  pl.BlockSpec((tb, HIST_PAD, pool_w), lambda i, j: (i, 0, 0))]
    n_split = ts // (2 * LANES) if ts % (2 * LANES) == 0 else 1
    kern = functools.partial(_pre_kernel, pos0=float(pos0), absorb=absorb, pool_w=pool_w,
                             q_lora=q_lora, kv_lora=kv_lora, n_split=n_split)
    return pl.pallas_call(
        kern,
        grid=grid,
        in_specs=in_specs,
        out_specs=out_specs,
        out_shape=out_shape,
        scratch_shapes=[pltpu.VMEM((tb, HIST_PAD + ts, pool_w), F32)],
        compiler_params=_params(("arbitrary", "arbitrary")),
        name="pre_sample" if absorb else "pre_prompt",
    )(x, mod, hist16, tab, w["win"], w["wq"], wkv, w["wpool"], w["gmix"], w["gq"], w["gkv"], w["pscale"])


def _attn_kernel(q_ref, k_ref, v_ref, o_ref, *, tq):
    s = q_ref.shape[1]
    th = tq // 2
    first_head = lax.broadcasted_iota(jnp.int32, (tq, 2 * V_DIM), 1) < V_DIM
    rowmax = lambda a: jnp.max(a, axis=-1, keepdims=True)
    rowsum = lambda a: jnp.sum(a, axis=-1, keepdims=True)
    heads = q_ref.shape[2] // HEAD_PAD
    units = [(qi, hh) for qi in range(s // tq) for hh in range(heads)]
    val = [dict() for _ in units]

    def scores(u):
        qi, hh = units[u]
        q0 = qi * tq
        hs = slice(hh * HEAD_PAD, (hh + 1) * HEAD_PAD)
        s_d = []
        for j in range(2):
            nk = (j + 1) * th
            r0 = q0 + j * th
            row = lax.broadcasted_iota(jnp.int32, (th, nk), 0) + j * th
            col = lax.broadcasted_iota(jnp.int32, (th, nk), 1)
            s_d.append(jnp.where(row >= col, _nt_dot(q_ref[0, r0:r0 + th, hs], k_ref[0, q0:q0 + nk, hs]),
                                 NEG_INF))
        val[u]["s_d"] = s_d
        val[u]["s_o"] = _nt_dot(q_ref[0, q0:q0 + tq, hs], k_ref[0, 0:q0, hs]) if qi > 0 else None

    def softmax(u):
        s_d, s_o = val[u]["s_d"], val[u]["s_o"]
        m = [rowmax(s_d[j]) for j in range(2)]
        if s_o is not None:
            m = [jnp.maximum(m[j], rowmax(s_o[j * th:(j + 1) * th])) for j in range(2)]
            p_o = jnp.concatenate([jnp.exp2(s_o[j * th:(j + 1) * th] - m[j]) for j in range(2)], axis=0)
            val[u]["l_o"] = rowsum(p_o)
            val[u]["p_o"] = p_o.astype(BF16)
        p_d = [jnp.exp2(s_d[j] - m[j]) for j in range(2)]
        val[u]["l_d"] = [rowsum(p) for p in p_d]
        val[u]["p_d"] = [p.astype(BF16) for p in p_d]

    def values(u):
        qi, hh = units[u]
        q0 = qi * tq
        vs = slice(hh // 2 * 2 * V_DIM, (hh // 2 + 1) * 2 * V_DIM)
        acc = jnp.concatenate([_dot(val[u]["p_d"][j], v_ref[0, q0:q0 + (j + 1) * th, vs]) for j in range(2)],
                              axis=0)
        l = jnp.concatenate(val[u]["l_d"], axis=0)
        if val[u]["s_o"] is not None:
            acc = acc + _dot(val[u]["p_o"], v_ref[0, 0:q0, vs])
            l = l + val[u]["l_o"]
        out = acc * (1.0 / l)
        if hh % 2 == 0:
            val[u]["out"] = out
        else:
            o_ref[0, q0:q0 + tq, vs] = jnp.where(first_head, val[u - 1]["out"], out).astype(BF16)

    scores(0)
    for u in range(len(units)):
        if u + 1 < len(units):
            scores(u + 1)
        softmax(u)
        values(u)


def _attention(q, kcat, v, *, tq, heads):
    b, s, _ = q.shape
    return pl.pallas_call(
        functools.partial(_attn_kernel, tq=tq),
        grid=(b, N_HEADS // heads),
        in_specs=[
            pl.BlockSpec((1, s, heads * HEAD_PAD), lambda i, p: (i, 0, p)),
            pl.BlockSpec((1, s, heads * HEAD_PAD), lambda i, p: (i, 0, p)),
            pl.BlockSpec((1, s, heads * V_DIM), lambda i, p: (i, 0, p)),
        ],
        out_specs=pl.BlockSpec((1, s, heads * V_DIM), lambda i, p: (i, 0, p)),
        out_shape=jax.ShapeDtypeStruct((b, s, N_HEADS * V_DIM), BF16),
        compiler_params=_params(("arbitrary", "arbitrary")),
        name="attn_prompt",
    )(q, kcat, v)


def _decode_ops(pt_ref, qlat_ref, qpad_ref, latn_ref, krn_ref, clat_hbm, ckr_hbm, o_ref,
                latbuf, krbuf, s_ref, sem, *, n_pages, page, chunk, s_new):
    t_past = n_pages * page
    t_all = t_past + page
    rows = qlat_ref.shape[1]
    spans = [(lo, min(chunk, t_all - lo)) for lo in list(range(0, t_past, chunk)) + [t_past]]

    def page_copies(bb, sl, p):
        pg = pt_ref[bb, p]
        dst = pl.ds(p * page, page)
        return (pltpu.make_async_copy(clat_hbm.at[pg], latbuf.at[sl, dst], sem.at[0, sl]),
                pltpu.make_async_copy(ckr_hbm.at[pg], krbuf.at[sl, :, dst], sem.at[1, sl]))

    def start_pages(bb, sl):
        for p in range(n_pages):
            for c in page_copies(bb, sl, p):
                c.start(priority=p % 2)

    def wait_pages(bb, sl):
        for p in range(n_pages):
            for c in page_copies(bb, sl, p):
                c.wait()

    def attend_stages(sl):
        val = {}

        def new_page():
            latbuf[sl, t_past:t_all, :] = jnp.zeros((page, latbuf.shape[-1]), F32)
            krbuf[sl, :, t_past:t_all] = jnp.zeros((ROPE_DIM, page), F32)
            latbuf[sl, t_past:t_past + s_new, :] = latn_ref[sl]
            krbuf[sl, :, t_past:t_past + s_new] = krn_ref[sl]

        def scores(lo, n):
            def run():
                lat_c = latbuf[sl, lo:lo + n, :].astype(BF16)
                s_ref[:, lo:lo + n] = (
                    _nt_dot(qlat_ref[sl].astype(BF16), lat_c)
                    + _dot(qpad_ref[sl][:, :ROPE_DIM].astype(BF16), krbuf[sl, :, lo:lo + n].astype(BF16)))
            return run

        def softmax():
            tok = lax.rem(lax.broadcasted_iota(jnp.int32, (rows, page), 0), s_new)
            col = lax.broadcasted_iota(jnp.int32, (rows, page), 1)
            s_ref[:, t_past:t_all] = jnp.where(col <= tok, s_ref[:, t_past:t_all], NEG_INF)
            s_all = s_ref[...]
            m = jnp.max(s_all, axis=-1, keepdims=True)
            p = jnp.exp2(s_all - m)
            val["inv_l"] = 1.0 / jnp.sum(p, axis=-1, keepdims=True)
            val["p"] = p.astype(BF16)
            val["out"] = None

        def values(lo, n, last):
            def run():
                t = _dot(val["p"][:, lo:lo + n], latbuf[sl, lo:lo + n, :].astype(BF16)) * val["inv_l"]
                val["out"] = t if val["out"] is None else val["out"] + t
                if last:
                    o_ref[sl] = val["out"]
            return run

        return ([new_page] + [scores(lo, n) for lo, n in spans] + [softmax]
                + [values(lo, n, i == len(spans) - 1) for i, (lo, n) in enumerate(spans)])

    return start_pages, wait_pages, attend_stages


def _post_dec_kernel(pt_ref, x_ref, mod_ref, pool_ref, att_ref, wout_ref, wuv_ref, wup_ref, wdown_ref,
                     gmlp_ref, gfin_ref, qlat_ref, qpad_ref, latn_ref, krn_ref, clat_hbm, ckr_hbm,
                     y_ref, o_ref, latbuf, krbuf, s_ref, sem, *, final, pool_w, kv_lora, ff_chunk,
                     n_pages, page, chunk, s_new):
    step = pl.program_id(0) * pl.num_programs(1) + pl.program_id(1)
    n_steps = pl.num_programs(0) * pl.num_programs(1)
    n_slots = latbuf.shape[0]
    start_pages, wait_pages, attend_stages = _decode_ops(
        pt_ref, qlat_ref, qpad_ref, latn_ref, krn_ref, clat_hbm, ckr_hbm, o_ref, latbuf, krbuf, s_ref, sem,
        n_pages=n_pages, page=page, chunk=chunk, s_new=s_new)
    tile_stages = _post_stages(x_ref, mod_ref, pool_ref, att_ref, wout_ref, wuv_ref, wup_ref, wdown_ref,
                               gmlp_ref, gfin_ref, y_ref, absorb=False, final=final, pool_w=pool_w,
                               kv_lora=kv_lora, ff_chunk=ff_chunk)
    b0 = step * n_slots
    last_b0 = (n_steps - 1) * n_slots

    @pl.when(step == 0)
    def _():
        for j in range(n_slots):
            start_pages(j, j)

    attn_stages = []
    for j in range(n_slots):
        attn_stages.append(functools.partial(wait_pages, b0 + j, j))
        attn_stages += attend_stages(j)
        attn_stages.append(functools.partial(start_pages, jnp.minimum(b0 + n_slots + j, last_b0 + j), j))

    done = 0
    for i, stage in enumerate(tile_stages):
        upto = (i + 1) * len(attn_stages) // len(tile_stages)
        for a in attn_stages[done:upto]:
            a()
        done = upto
        stage()

    @pl.when(step == n_steps - 1)
    def _():
        for j in range(n_slots):
            wait_pages(last_b0 + j, j)


def _post_prompt_with_decode(x, mod, pool, att, w, page_table, qlat, qpad, lat_new, kr_new,
                             cache_lat, cache_kr, *, final, ts):
    b, s, d = x.shape
    pool_w = pool.shape[-1]
    db, rows, kv_lora = qlat.shape
    n_pages = page_table.shape[1]
    page = cache_lat.shape[1]
    s_new = lat_new.shape[1]
    t_past = n_pages * page
    nj = s // ts
    n_steps = b * nj
    assert db % n_steps == 0, "sample batches are spread evenly over the prompt tiles"
    n_slots = db // n_steps
    tile = lambda width: pl.BlockSpec((1, ts, width), lambda i, j, pt: (i, j, 0))
    const = lambda a: pl.BlockSpec(a.shape, lambda i, j, pt: (0,) * a.ndim, pipeline_mode=pl.Buffered(1))
    per_step = lambda *shape: pl.BlockSpec((n_slots,) + shape, lambda i, j, pt: (i * nj + j, 0, 0))
    kern = functools.partial(_post_dec_kernel, final=final, pool_w=pool_w, kv_lora=kv_lora,
                             ff_chunk=min(1024, w["wup"].shape[1]), n_pages=n_pages, page=page,
                             chunk=min(2048, t_past), s_new=s_new)
    grid_spec = pltpu.PrefetchScalarGridSpec(
        num_scalar_prefetch=1,
        grid=(b, nj),
        in_specs=[
            tile(d),
            pl.BlockSpec((1, N_MOD, d), lambda i, j, pt: (i, 0, 0)),
            tile(pool_w),
            tile(att.shape[-1]),
            const(w["wout"]), const(w["wuvp"]), const(w["wup"]), const(w["wdown"]),
            const(w["gmlp"]), const(w["gfin"]),
            per_step(rows, kv_lora), per_step(rows, HEAD_PAD), per_step(s_new, kv_lora),
            per_step(ROPE_DIM, s_new),
            pl.BlockSpec(memory_space=pl.ANY),
            pl.BlockSpec(memory_space=pl.ANY),
        ],
        out_specs=[tile(d), per_step(rows, kv_lora)],
        scratch_shapes=[
            pltpu.VMEM((n_slots, t_past + page, kv_lora), F32),
            pltpu.VMEM((n_slots, ROPE_DIM, t_past + page), F32),
            pltpu.VMEM((rows, t_past + page), F32),
            pltpu.SemaphoreType.DMA((2, n_slots)),
        ],
    )
    return pl.pallas_call(
        kern,
        grid_spec=grid_spec,
        out_shape=[jax.ShapeDtypeStruct((b, s, d), F32), jax.ShapeDtypeStruct((db, rows, kv_lora), F32)],
        compiler_params=_params(("arbitrary", "arbitrary"), vmem=VMEM_LIMIT_MERGED),
        name="post_prompt_attn_sample",
    )(page_table, x, mod, pool, att, w["wout"], w["wuvp"], w["wup"], w["wdown"], w["gmlp"], w["gfin"],
      qlat, qpad, lat_new, kr_new, cache_lat, cache_kr)


def _post_kernel(x_ref, mod_ref, pool_ref, att_ref, wout_ref, wuv_ref, wup_ref, wdown_ref,
                 gmlp_ref, gfin_ref, y_ref, *, absorb, final, pool_w, kv_lora, ff_chunk):
    for stage in _post_stages(x_ref, mod_ref, pool_ref, att_ref, wout_ref, wuv_ref, wup_ref, wdown_ref,
                              gmlp_ref, gfin_ref, y_ref, absorb=absorb, final=final, pool_w=pool_w,
                              kv_lora=kv_lora, ff_chunk=ff_chunk):
        stage()


def _post_stages(x_ref, mod_ref, pool_ref, att_ref, wout_ref, wuv_ref, wup_ref, wdown_ref,
                 gmlp_ref, gfin_ref, y_ref, *, absorb, final, pool_w, kv_lora, ff_chunk):
    tb, ts, d = x_ref.shape
    r = tb * ts
    d_ff = wup_ref.shape[1]
    n_chunks = d_ff // ff_chunk
    val = {}

    def project():
        mix = _dot(pool_ref[...].reshape(r, pool_w), wout_ref[0:pool_w, :])
        if absorb:
            for p in range(N_HEADS // 2):
                o_pair = None
                for hh in range(2):
                    hd = 2 * p + hh
                    o_lat = att_ref[:, hd, :, :].reshape(r, kv_lora).astype(BF16)
                    t = _dot(o_lat, wuv_ref[hd])
                    o_pair = t if o_pair is None else o_pair + t
                lo = pool_w + p * 2 * V_DIM
                mix = mix + _dot(o_pair.astype(BF16), wout_ref[lo:lo + 2 * V_DIM, :])
        else:
            mix = mix + _dot(att_ref[...].reshape(r, N_HEADS * V_DIM), wout_ref[pool_w:, :])
        x1 = x_ref[...] + mod_ref[:, 2:3, :] * mix.reshape(tb, ts, d)
        val["x1"] = x1
        val["h2"] = (_rms(x1, gmlp_ref[...]) * (1.0 + mod_ref[:, 4:5, :]) + mod_ref[:, 3:4, :]
                     ).reshape(r, d).astype(BF16)
        val["mlp"] = None

    def mlp_up(c):
        def run():
            hid = jnp.maximum(_dot(val["h2"], wup_ref[:, c * ff_chunk:(c + 1) * ff_chunk]), 0.0)
            val["hid"] = (hid * hid).astype(BF16)
        return run

    def mlp_down(c):
        def run():
            t = _dot(val["hid"], wdown_ref[c * ff_chunk:(c + 1) * ff_chunk, :])
            val["mlp"] = t if val["mlp"] is None else val["mlp"] + t
            if c == n_chunks - 1:
                x2 = val["x1"] + mod_ref[:, 5:6, :] * val["mlp"].reshape(tb, ts, d)
                y_ref[...] = _rms(x2, gfin_ref[...]) if final else x2
        return run

    stages = [project]
    for c in range(n_chunks):
        stages += [mlp_up(c), mlp_down(c)]
    return stages


def _post_sample(x, mod, pool, o_lat, w, *, final, tb, ts):
    b, s, d = x.shape
    pool_w = pool.shape[-1]
    kv_lora = o_lat.shape[-1]
    tile = lambda width: pl.BlockSpec((tb, ts, width), lambda i, j: (i, j, 0))
    const = lambda a: pl.BlockSpec(a.shape, lambda i, j: (0,) * a.ndim, pipeline_mode=pl.Buffered(1))
    kern = functools.partial(_post_kernel, absorb=True, final=final, pool_w=pool_w, kv_lora=kv_lora,
                             ff_chunk=min(1024, w["wup"].shape[1]))
    return pl.pallas_call(
        kern,
        grid=(b // tb, s // ts),
        in_specs=[
            tile(d),
            pl.BlockSpec((tb, N_MOD, d), lambda i, j: (i, 0, 0)),
            tile(pool_w),
            pl.BlockSpec((tb, N_HEADS, ts, kv_lora), lambda i, j: (i, 0, j, 0)),
            const(w["wout"]), const(w["wuvp"]), const(w["wup"]), const(w["wdown"]),
            const(w["gmlp"]), const(w["gfin"]),
        ],
        out_specs=tile(d),
        out_shape=jax.ShapeDtypeStruct((b, s, d), F32),
        compiler_params=_params(("arbitrary", "arbitrary")),
        name="post_sample",
    )(x, mod, pool, o_lat, w["wout"], w["wuvp"], w["wup"], w["wdown"], w["gmlp"], w["gfin"])


def _rot_half(wr):
    half = ROPE_DIM // 2
    return jnp.concatenate([-wr[..., half:], wr[..., :half]], axis=-1)


def _layer_weights(w_in, g_mix, g_q, w_uq, g_kv, w_uk, w_uv, w_pool, pool_scale, w_out, g_mlp,
                   w_up, w_down, g_final):
    d, _ = w_in.shape
    q_lora = g_q.shape[-1]
    kv_lora = g_kv.shape[-1]
    n_groups, gw, _ = w_pool.shape
    pool_w = n_groups * gw
    o3 = pool_w + q_lora + kv_lora
    zpad = lambda rows, cols: jnp.zeros((rows, cols), F32)

    w_kr = w_in[:, o3:]
    win = jnp.concatenate([w_in[:, :o3], w_kr, zpad(d, LANES - ROPE_DIM),
                           _rot_half(w_kr), zpad(d, LANES - ROPE_DIM)], axis=1)

    q_rope = w_uq[:, :, NOPE_DIM:]
    q_nope = w_uq[:, :, :NOPE_DIM]
    hz = jnp.zeros((q_lora, N_HEADS, HEAD_PAD - ROPE_DIM - NOPE_DIM), F32)
    wq_a = jnp.concatenate([q_rope, q_nope, hz], axis=-1).reshape(q_lora, N_HEADS * HEAD_PAD)
    wq_b = _rot_half(q_rope).reshape(q_lora, N_HEADS * ROPE_DIM)
    wq = jnp.concatenate([wq_a, wq_b], axis=1)

    kz_lo = jnp.zeros((kv_lora, N_HEADS, ROPE_DIM), F32)
    kz_hi = jnp.zeros((kv_lora, N_HEADS, HEAD_PAD - ROPE_DIM - NOPE_DIM), F32)
    wk = jnp.concatenate([kz_lo, w_uk, kz_hi], axis=-1).reshape(kv_lora, N_HEADS * HEAD_PAD)
    wkv = jnp.concatenate([wk, w_uv.reshape(kv_lora, N_HEADS * V_DIM)], axis=1)
    wabs = jnp.transpose(jnp.concatenate([kz_lo, w_uk, kz_hi], axis=-1), (1, 2, 0))

    uv = jnp.transpose(w_uv, (1, 0, 2))
    uz = jnp.zeros_like(uv)
    even = (jnp.arange(N_HEADS) % 2 == 0)[:, None, None]
    wuvp = jnp.where(even, jnp.concatenate([uv, uz], axis=-1), jnp.concatenate([uz, uv], axis=-1))

    wpool = jax.scipy.linalg.block_diag(*[w_pool[g] for g in range(n_groups)])

    bf = lambda a: a.astype(BF16)
    return dict(
        win=bf(win), wq=bf(wq), wkv=bf(wkv), wabs=bf(wabs), wuvp=bf(wuvp), wpool=bf(wpool),
        wout=bf(w_out), wup=bf(w_up), wdown=bf(w_down),
        gmix=g_mix.reshape(1, -1), gq=g_q.reshape(1, -1), gkv=g_kv.reshape(1, -1),
        pscale=pool_scale.reshape(1, -1), gmlp=g_mlp.reshape(1, -1), gfin=g_final.reshape(1, -1),
    )


def _rope_tables(positions):
    inv = ROPE_BASE ** (-jnp.arange(0, ROPE_DIM, 2, dtype=F32) / ROPE_DIM)
    ang = positions[:, None] * inv[None, :]
    cos2 = jnp.tile(jnp.cos(ang), (1, 2))
    sin2 = jnp.tile(jnp.sin(ang), (1, 2))
    n = positions.shape[0]
    qs = SOFTMAX_SCALE * LOG2E
    t_qa = jnp.concatenate([cos2, jnp.ones((n, NOPE_DIM), F32),
                            jnp.zeros((n, HEAD_PAD - ROPE_DIM - NOPE_DIM), F32)], axis=1) * qs
    zr = jnp.zeros((n, HEAD_PAD - ROPE_DIM), F32)
    t_qb = jnp.concatenate([sin2, zr], axis=1) * qs
    t_ka = jnp.concatenate([cos2, zr], axis=1)
    t_kb = jnp.concatenate([sin2, zr], axis=1)
    return jnp.concatenate([t_qa, t_qb, t_ka, t_kb], axis=1)


def _seq_tile(s, target):
    t = min(s, target)
    while s % t:
        t //= 2
    return t


def kernel(x_prompt, x_sample, cache_latent, cache_krope, state_pool, page_table, c_prompt, c_sample,
           w_mod, b_mod, g_mix, w_in, g_q, w_uq, g_kv, w_uk, w_uv, w_pool, pool_scale, w_out,
           g_mlp, w_up, w_down, g_final):
    bp, sp, d = x_prompt.shape
    db, ss, _ = x_sample.shape
    depth = w_mod.shape[0]
    n_pages = page_table.shape[1]
    page = cache_latent.shape[2]
    past = n_pages * page
    pool_w = state_pool.shape[-1]
    kv_lora = g_kv.shape[-1]

    tab_p = _rope_tables(jnp.arange(sp, dtype=F32))
    tab_s = _rope_tables(jnp.arange(ss, dtype=F32) + float(past))
    ts_p = _seq_tile(sp, 512)
    tb_s = _seq_tile(db, max(1, 512 // ss))

    xp, xs = x_prompt, x_sample
    outs = [[] for _ in range(6)]
    for l in range(depth):
        w = _layer_weights(w_in[l], g_mix[l], g_q[l], w_uq[l], g_kv[l], w_uk[l], w_uv[l], w_pool[l],
                           pool_scale[l], w_out[l], g_mlp[l], w_up[l], w_down[l], g_final)
        mod = _modulation(jnp.concatenate([c_prompt, c_sample], axis=0), w_mod[l], b_mod[l])
        mod = mod.reshape(bp + db, N_MOD, d)
        mod_p, mod_s = mod[:bp], mod[bp:]

        hist_p = jnp.zeros((bp, HIST_PAD, pool_w), F32)
        q, kcat, v, lat_p, kr_p, pool_p, nh_p = _pre(
            xp, mod_p, hist_p, tab_p, w, pos0=0.0, absorb=False, tb=1, ts=_seq_tile(sp, 1024))
        att_p = _attention(q, kcat, v, tq=_seq_tile(sp, 512), heads=4)
        final = l == depth - 1

        hist_s = jnp.pad(state_pool[l], ((0, 0), (HIST_PAD - POOL_HIST, 0), (0, 0)))
        qs, qlat, lat_s, kr_s, pool_s, nh_s = _pre(
            xs, mod_s, hist_s, tab_s, w, pos0=float(past), absorb=True, tb=tb_s, ts=ss)
        rows = ss * N_HEADS
        xp, o_lat = _post_prompt_with_decode(
            xp, mod_p, pool_p, att_p, w, page_table,
            qlat.reshape(db, rows, kv_lora),
            qs.reshape(db, rows, HEAD_PAD),
            lat_s, jnp.swapaxes(kr_s, 1, 2),
            cache_latent[l], jnp.swapaxes(cache_krope[l], 1, 2), final=final, ts=ts_p)
        xs = _post_sample(xs, mod_s, pool_s, o_lat.reshape(db, N_HEADS, ss, kv_lora), w,
                          final=final, tb=tb_s, ts=ss)

        for lst, val in zip(outs, (lat_p, jnp.swapaxes(kr_p, 1, 2), nh_p[:, HIST_PAD - POOL_HIST:],
                                   lat_s, kr_s, nh_s[:, HIST_PAD - POOL_HIST:])):
            lst.append(val)
    return (xp, xs) + tuple(jnp.stack(o) for o in outs)
```

```python
import functools
import math

import jax
import jax.numpy as jnp
from jax import lax
from jax.experimental import pallas as pl
from jax.experimental.pallas import tpu as pltpu

F32 = jnp.float32
BF16 = jnp.bfloat16

N_HEADS = 8
NOPE_DIM = 64
ROPE_DIM = 32
V_DIM = 64
POOL_WINDOWS = (2, 4, 8, 16)
POOL_HIST = max(POOL_WINDOWS) - 1
HIST_PAD = POOL_HIST + 1
assert all(w & (w - 1) == 0 for w in POOL_WINDOWS), "window sums are built by doubling"
N_MOD = 6
ROPE_BASE = 10000.0
EPS = 1e-6
NEG_INF = -1e30
SOFTMAX_SCALE = 1.0 / math.sqrt(NOPE_DIM + ROPE_DIM)
LOG2E = math.log2(math.e)

LANES = 128
SUBLANES = 8
VMEM_LIMIT = 56 * 1024 * 1024
VMEM_LIMIT_MERGED = 62 * 1024 * 1024

HEAD_PAD = LANES


def _params(sem, vmem=VMEM_LIMIT):
    return pltpu.CompilerParams(dimension_semantics=sem, vmem_limit_bytes=vmem)


def _nt_dot(a, b):
    return lax.dot_general(a, b, (((1,), (1,)), ((), ())), preferred_element_type=F32)


def _dot(a, b):
    return jnp.dot(a, b, preferred_element_type=F32)


def _rms(x, g):
    return x * lax.rsqrt(jnp.mean(x * x, axis=-1, keepdims=True) + EPS) * g


def _mod_kernel(c_ref, w_ref, b_ref, o_ref):
    c = c_ref[...]
    sc = c * (1.0 / (1.0 + jnp.exp(-c)))
    o_ref[...] = _dot(sc.astype(BF16), w_ref[...].astype(BF16)) + b_ref[...]


def _modulation(c, w_mod, b_mod):
    n, d = c.shape
    nout = w_mod.shape[1]
    tn = d
    return pl.pallas_call(
        _mod_kernel,
        grid=(nout // tn,),
        in_specs=[
            pl.BlockSpec((n, d), lambda j: (0, 0)),
            pl.BlockSpec((d, tn), lambda j: (0, j)),
            pl.BlockSpec((1, tn), lambda j: (0, j)),
        ],
        out_specs=pl.BlockSpec((n, tn), lambda j: (0, j)),
        out_shape=jax.ShapeDtypeStruct((n, nout), F32),
        compiler_params=_params(("arbitrary",)),
        name="mod",
    )(c, w_mod, b_mod.reshape(1, nout))


def _pre_kernel(*refs, pos0, absorb, pool_w, q_lora, kv_lora, n_split):
    (x_ref, mod_ref, hist_ref, tab_ref, win_ref, wq_ref, wkv_ref, wpool_ref,
     gmix_ref, gq_ref, gkv_ref, pscale_ref) = refs[:12]
    if absorb:
        (q_ref, qlat_ref, lat_ref, kr_ref, pool_ref, nh_ref, ext_ref) = refs[12:]
    else:
        (q_ref, kcat_ref, v_ref, lat_ref, kr_ref, pool_ref, nh_ref, ext_ref) = refs[12:]
    si = pl.program_id(1)
    tb, ts, d = x_ref.shape
    hq = N_HEADS * HEAD_PAD
    shift1 = mod_ref[:, 0:1, :]
    scale1 = mod_ref[:, 1:2, :]
    o1 = pool_w
    o2 = o1 + q_lora
    o3 = o2 + kv_lora
    gw = pool_w // len(POOL_WINDOWS)

    @pl.when(si == 0)
    def _():
        ext_ref[:, 0:HIST_PAD, :] = hist_ref[...]

    def rows_block(r0, n):
        rs = slice(r0, r0 + n)
        r = tb * n
        e0 = HIST_PAD + r0
        val = {}

        def project():
            h = _rms(x_ref[:, rs, :], gmix_ref[...]) * (1.0 + scale1) + shift1
            proj = _dot(h.reshape(r, d).astype(BF16), win_ref[...])
            val["u"] = proj[:, :o1].reshape(tb, n, pool_w)
            ext_ref[:, e0:e0 + n, :] = val["u"]
            val["cq"] = proj[:, o1:o2]
            val["ckv"] = proj[:, o2:o3]
            val["kr_a"] = proj[:, o3:o3 + LANES].reshape(tb, n, LANES)
            val["kr_b"] = proj[:, o3 + LANES:o3 + 2 * LANES].reshape(tb, n, LANES)

        def queries():
            t_qa = tab_ref[rs, 0 * LANES:1 * LANES]
            t_qb = tab_ref[rs, 1 * LANES:2 * LANES]
            qab = _dot(_rms(val["cq"], gq_ref[...]).astype(BF16), wq_ref[...])
            per_group = LANES // ROPE_DIM
            for hd in range(N_HEADS):
                lo = hd * HEAD_PAD
                qa = qab[:, lo:lo + HEAD_PAD].reshape(tb, n, HEAD_PAD)
                grp = hq + hd // per_group * LANES
                qb = qab[:, grp:grp + LANES]
                off = hd % per_group * ROPE_DIM
                if off:
                    qb = pltpu.roll(qb, LANES - off, 1)
                qh = qa * t_qa + qb.reshape(tb, n, HEAD_PAD) * t_qb
                if absorb:
                    q_ref[:, hd, rs, :] = qh
                    ql = _dot(qh.reshape(r, HEAD_PAD).astype(BF16), wkv_ref[hd])
                    qlat_ref[:, hd, rs, :] = ql.reshape(tb, n, kv_lora)
                else:
                    q_ref[:, rs, lo:lo + HEAD_PAD] = qh.astype(BF16)

        def keys():
            lat = _rms(val["ckv"], gkv_ref[...])
            lat_ref[:, rs, :] = lat.reshape(tb, n, kv_lora)
            kr128 = (val["kr_a"] * tab_ref[rs, 2 * LANES:3 * LANES]
                     + val["kr_b"] * tab_ref[rs, 3 * LANES:4 * LANES])
            if absorb:
                kr_ref[:, rs, :] = kr128[:, :, :ROPE_DIM]
            else:
                for i in range(tb):
                    kr_ref[i, :, rs] = jnp.transpose(kr128[i])[:ROPE_DIM, :]
                kv = _dot(lat.astype(BF16), wkv_ref[...])
                for hd in range(N_HEADS):
                    lo = hd * HEAD_PAD
                    kcat_ref[:, rs, lo:lo + HEAD_PAD] = (
                        kv[:, lo:lo + HEAD_PAD].reshape(tb, n, HEAD_PAD) + kr128).astype(BF16)
                v_ref[:, rs, :] = kv[:, hq:].reshape(tb, n, N_HEADS * V_DIM).astype(BF16)

        def pooling():
            pos = (pos0 + (si * ts + r0).astype(F32)
                   + lax.broadcasted_iota(jnp.int32, (1, n, gw), 1).astype(F32))
            parts = []
            for g, w in enumerate(POOL_WINDOWS):
                lo = g * gw
                acc = ext_ref[:, e0 - HIST_PAD:e0 + n, lo:lo + gw]
                k = 1
                while k < w:
                    acc = acc + pltpu.roll(acc, k, 1)
                    k *= 2
                acc = acc[:, HIST_PAD:, :]
                inv = 1.0 / jnp.minimum(float(w), pos + 1.0)
                parts.append(acc * inv - val["u"][:, :, lo:lo + gw])
            dpool = jnp.concatenate(parts, axis=-1).reshape(r, pool_w).astype(BF16)
            y = _dot(dpool, wpool_ref[...]) * pscale_ref[...]
            pool_ref[:, rs, :] = y.reshape(tb, n, pool_w).astype(BF16)

        return [project, pooling, queries, keys]

    blocks = [rows_block(blk * (ts // n_split), ts // n_split) for blk in range(n_split)]
    for stage_of_blocks in zip(*blocks):
        for stage in stage_of_blocks:
            stage()

    tail = ext_ref[:, ts:ts + HIST_PAD, :]
    nh_ref[...] = tail
    ext_ref[:, 0:HIST_PAD, :] = tail


def _pre(x, mod, hist16, tab, w, *, pos0, absorb, tb, ts):
    b, s, d = x.shape
    pool_w = hist16.shape[-1]
    q_lora = w["gq"].shape[-1]
    kv_lora = w["gkv"].shape[-1]
    hq = N_HEADS * HEAD_PAD
    grid = (b // tb, s // ts)
    tile = lambda width: pl.BlockSpec((tb, ts, width), lambda i, j: (i, j, 0))
    full = lambda a: pl.BlockSpec(a.shape, lambda i, j: (0,) * a.ndim)
    wkv = w["wabs"] if absorb else w["wkv"]
    in_specs = [
        tile(d),
        pl.BlockSpec((tb, N_MOD, d), lambda i, j: (i, 0, 0)),
        pl.BlockSpec((tb, HIST_PAD, pool_w), lambda i, j: (i, 0, 0)),
        pl.BlockSpec((ts, 4 * LANES), lambda i, j: (j, 0)),
        full(w["win"]), full(w["wq"]), full(wkv), full(w["wpool"]),
        full(w["gmix"]), full(w["gq"]), full(w["gkv"]), full(w["pscale"]),
    ]
    sds = jax.ShapeDtypeStruct
    if absorb:
        heads = lambda width: pl.BlockSpec((tb, N_HEADS, ts, width), lambda i, j: (i, 0, j, 0))
        out_shape = [sds((b, N_HEADS, s, HEAD_PAD), F32), sds((b, N_HEADS, s, kv_lora), F32)]
        out_specs = [heads(HEAD_PAD), heads(kv_lora)]
    else:
        out_shape = [sds((b, s, hq), BF16), sds((b, s, hq), BF16), sds((b, s, N_HEADS * V_DIM), BF16)]
        out_specs = [tile(hq), tile(hq), tile(N_HEADS * V_DIM)]
    if absorb:
        kr_shape, kr_spec = sds((b, s, ROPE_DIM), F32), tile(ROPE_DIM)
    else:
        kr_shape = sds((b, ROPE_DIM, s), F32)
        kr_spec = pl.BlockSpec((tb, ROPE_DIM, ts), lambda i, j: (i, 0, j))
    out_shape += [sds((b, s, kv_lora), F32), kr_shape, sds((b, s, pool_w), BF16),
                  sds((b, HIST_PAD, pool_w), F32)]
    out_specs += [tile(kv_lora), kr_spec, tile(pool_w),
                  pl.BlockSpec((tb, HIST_PAD, pool_w), lambda i, j: (i, 0, 0))]
    n_split = ts // (2 * LANES) if ts % (2 * LANES) == 0 else 1
    kern = functools.partial(_pre_kernel, pos0=float(pos0), absorb=absorb, pool_w=pool_w,
                             q_lora=q_lora, kv_lora=kv_lora, n_split=n_split)
    return pl.pallas_call(
        kern,
        grid=grid,
        in_specs=in_specs,
        out_specs=out_specs,
        out_shape=out_shape,
        scratch_shapes=[pltpu.VMEM((tb, HIST_PAD + ts, pool_w), F32)],
        compiler_params=_params(("arbitrary", "arbitrary")),
        name="pre_sample" if absorb else "pre_prompt",
    )(x, mod, hist16, tab, w["win"], w["wq"], wkv, w["wpool"], w["gmix"], w["gq"], w["gkv"], w["pscale"])


def _attn_kernel(q_ref, k_ref, v_ref, o_ref, *, tq):
    s = q_ref.shape[1]
    th = tq // 2
    first_head = lax.broadcasted_iota(jnp.int32, (tq, 2 * V_DIM), 1) < V_DIM
    rowmax = lambda a: jnp.max(a, axis=-1, keepdims=True)
    rowsum = lambda a: jnp.sum(a, axis=-1, keepdims=True)
    heads = q_ref.shape[2] // HEAD_PAD
    units = [(qi, hh) for qi in range(s // tq) for hh in range(heads)]
    val = [dict() for _ in units]

    def scores(u):
        qi, hh = units[u]
        q0 = qi * tq
        hs = slice(hh * HEAD_PAD, (hh + 1) * HEAD_PAD)
        s_d = []
        for j in range(2):
            nk = (j + 1) * th
            r0 = q0 + j * th
            row = lax.broadcasted_iota(jnp.int32, (th, nk), 0) + j * th
            col = lax.broadcasted_iota(jnp.int32, (th, nk), 1)
            s_d.append(jnp.where(row >= col, _nt_dot(q_ref[0, r0:r0 + th, hs], k_ref[0, q0:q0 + nk, hs]),
                                 NEG_INF))
        val[u]["s_d"] = s_d
        val[u]["s_o"] = _nt_dot(q_ref[0, q0:q0 + tq, hs], k_ref[0, 0:q0, hs]) if qi > 0 else None

    def softmax(u):
        s_d, s_o = val[u]["s_d"], val[u]["s_o"]
        m = [rowmax(s_d[j]) for j in range(2)]
        if s_o is not None:
            m = [jnp.maximum(m[j], rowmax(s_o[j * th:(j + 1) * th])) for j in range(2)]
            p_o = jnp.concatenate([jnp.exp2(s_o[j * th:(j + 1) * th] - m[j]) for j in range(2)], axis=0)
            val[u]["l_o"] = rowsum(p_o)
            val[u]["p_o"] = p_o.astype(BF16)
        p_d = [jnp.exp2(s_d[j] - m[j]) for j in range(2)]
        val[u]["l_d"] = [rowsum(p) for p in p_d]
        val[u]["p_d"] = [p.astype(BF16) for p in p_d]

    def values(u):
        qi, hh = units[u]
        q0 = qi * tq
        vs = slice(hh // 2 * 2 * V_DIM, (hh // 2 + 1) * 2 * V_DIM)
        acc = jnp.concatenate([_dot(val[u]["p_d"][j], v_ref[0, q0:q0 + (j + 1) * th, vs]) for j in range(2)],
                              axis=0)
        l = jnp.concatenate(val[u]["l_d"], axis=0)
        if val[u]["s_o"] is not None:
            acc = acc + _dot(val[u]["p_o"], v_ref[0, 0:q0, vs])
            l = l + val[u]["l_o"]
        out = acc * (1.0 / l)
        if hh % 2 == 0:
            val[u]["out"] = out
        else:
            o_ref[0, q0:q0 + tq, vs] = jnp.where(first_head, val[u - 1]["out"], out).astype(BF16)

    scores(0)
    for u in range(len(units)):
        if u + 1 < len(units):
            scores(u + 1)
        softmax(u)
        values(u)


def _attention(q, kcat, v, *, tq, heads):
    b, s, _ = q.shape
    return pl.pallas_call(
        functools.partial(_attn_kernel, tq=tq),
        grid=(b, N_HEADS // heads),
        in_specs=[
            pl.BlockSpec((1, s, heads * HEAD_PAD), lambda i, p: (i, 0, p)),
            pl.BlockSpec((1, s, heads * HEAD_PAD), lambda i, p: (i, 0, p)),
            pl.BlockSpec((1, s, heads * V_DIM), lambda i, p: (i, 0, p)),
        ],
        out_specs=pl.BlockSpec((1, s, heads * V_DIM), lambda i, p: (i, 0, p)),
        out_shape=jax.ShapeDtypeStruct((b, s, N_HEADS * V_DIM), BF16),
        compiler_params=_params(("arbitrary", "arbitrary")),
        name="attn_prompt",
    )(q, kcat, v)


def _decode_ops(pt_ref, qlat_ref, qpad_ref, latn_ref, krn_ref, clat_hbm, ckr_hbm, o_ref,
                latbuf, krbuf, s_ref, sem, *, n_pages, page, chunk, s_new):
    t_past = n_pages * page
    t_all = t_past + page
    rows = qlat_ref.shape[1]
    spans = [(lo, min(chunk, t_all - lo)) for lo in list(range(0, t_past, chunk)) + [t_past]]

    def page_copies(bb, sl, p):
        pg = pt_ref[bb, p]
        dst = pl.ds(p * page, page)
        return (pltpu.make_async_copy(clat_hbm.at[pg], latbuf.at[sl, dst], sem.at[0, sl]),
                pltpu.make_async_copy(ckr_hbm.at[pg], krbuf.at[sl, :, dst], sem.at[1, sl]))

    def start_pages(bb, sl):
        for p in range(n_pages):
            for c in page_copies(bb, sl, p):
                c.start(priority=p % 2)

    def wait_pages(bb, sl):
        for p in range(n_pages):
            for c in page_copies(bb, sl, p):
                c.wait()

    def attend_stages(sl):
        val = {}

        def new_page():
            latbuf[sl, t_past:t_all, :] = jnp.zeros((page, latbuf.shape[-1]), F32)
            krbuf[sl, :, t_past:t_all] = jnp.zeros((ROPE_DIM, page), F32)
            latbuf[sl, t_past:t_past + s_new, :] = latn_ref[sl]
            krbuf[sl, :, t_past:t_past + s_new] = krn_ref[sl]

        def scores(lo, n):
            def run():
                lat_c = latbuf[sl, lo:lo + n, :].astype(BF16)
                s_ref[:, lo:lo + n] = (
                    _nt_dot(qlat_ref[sl].astype(BF16), lat_c)
                    + _dot(qpad_ref[sl][:, :ROPE_DIM].astype(BF16), krbuf[sl, :, lo:lo + n].astype(BF16)))
            return run

        def softmax():
            tok = lax.rem(lax.broadcasted_iota(jnp.int32, (rows, page), 0), s_new)
            col = lax.broadcasted_iota(jnp.int32, (rows, page), 1)
            s_ref[:, t_past:t_all] = jnp.where(col <= tok, s_ref[:, t_past:t_all], NEG_INF)
            s_all = s_ref[...]
            m = jnp.max(s_all, axis=-1, keepdims=True)
            p = jnp.exp2(s_all - m)
            val["inv_l"] = 1.0 / jnp.sum(p, axis=-1, keepdims=True)
            val["p"] = p.astype(BF16)
            val["out"] = None

        def values(lo, n, last):
            def run():
                t = _dot(val["p"][:, lo:lo + n], latbuf[sl, lo:lo + n, :].astype(BF16)) * val["inv_l"]
                val["out"] = t if val["out"] is None else val["out"] + t
                if last:
                    o_ref[sl] = val["out"]
            return run

        return ([new_page] + [scores(lo, n) for lo, n in spans] + [softmax]
                + [values(lo, n, i == len(spans) - 1) for i, (lo, n) in enumerate(spans)])

    return start_pages, wait_pages, attend_stages


def _post_dec_kernel(pt_ref, x_ref, mod_ref, pool_ref, att_ref, wout_ref, wuv_ref, wup_ref, wdown_ref,
                     gmlp_ref, gfin_ref, qlat_ref, qpad_ref, latn_ref, krn_ref, clat_hbm, ckr_hbm,
                     y_ref, o_ref, latbuf, krbuf, s_ref, sem, *, final, pool_w, kv_lora, ff_chunk,
                     n_pages, page, chunk, s_new):
    step = pl.program_id(0) * pl.num_programs(1) + pl.program_id(1)
    n_steps = pl.num_programs(0) * pl.num_programs(1)
    n_slots = latbuf.shape[0]
    start_pages, wait_pages, attend_stages = _decode_ops(
        pt_ref, qlat_ref, qpad_ref, latn_ref, krn_ref, clat_hbm, ckr_hbm, o_ref, latbuf, krbuf, s_ref, sem,
        n_pages=n_pages, page=page, chunk=chunk, s_new=s_new)
    tile_stages = _post_stages(x_ref, mod_ref, pool_ref, att_ref, wout_ref, wuv_ref, wup_ref, wdown_ref,
                               gmlp_ref, gfin_ref, y_ref, absorb=False, final=final, pool_w=pool_w,
                               kv_lora=kv_lora, ff_chunk=ff_chunk)
    b0 = step * n_slots
    last_b0 = (n_steps - 1) * n_slots

    @pl.when(step == 0)
    def _():
        for j in range(n_slots):
            start_pages(j, j)

    attn_stages = []
    for j in range(n_slots):
        attn_stages.append(functools.partial(wait_pages, b0 + j, j))
        attn_stages += attend_stages(j)
        attn_stages.append(functools.partial(start_pages, jnp.minimum(b0 + n_slots + j, last_b0 + j), j))

    done = 0
    for i, stage in enumerate(tile_stages):
        upto = (i + 1) * len(attn_stages) // len(tile_stages)
        for a in attn_stages[done:upto]:
            a()
        done = upto
        stage()

    @pl.when(step == n_steps - 1)
    def _():
        for j in range(n_slots):
            wait_pages(last_b0 + j, j)


def _post_prompt_with_decode(x, mod, pool, att, w, page_table, qlat, qpad, lat_new, kr_new,
                             cache_lat, cache_kr, *, final, ts):
    b, s, d = x.shape
    pool_w = pool.shape[-1]
    db, rows, kv_lora = qlat.shape
    n_pages = page_table.shape[1]
    page = cache_lat.shape[1]
    s_new = lat_new.shape[1]
    t_past = n_pages * page
    nj = s // ts
    n_steps = b * nj
    assert db % n_steps == 0, "sample batches are spread evenly over the prompt tiles"
    n_slots = db // n_steps
    tile = lambda width: pl.BlockSpec((1, ts, width), lambda i, j, pt: (i, j, 0))
    const = lambda a: pl.BlockSpec(a.shape, lambda i, j, pt: (0,) * a.ndim, pipeline_mode=pl.Buffered(1))
    per_step = lambda *shape: pl.BlockSpec((n_slots,) + shape, lambda i, j, pt: (i * nj + j, 0, 0))
    kern = functools.partial(_post_dec_kernel, final=final, pool_w=pool_w, kv_lora=kv_lora,
                             ff_chunk=min(2048, w["wup"].shape[1]), n_pages=n_pages, page=page,
                             chunk=min(2048, t_past), s_new=s_new)
    grid_spec = pltpu.PrefetchScalarGridSpec(
        num_scalar_prefetch=1,
        grid=(b, nj),
        in_specs=[
            tile(d),
            pl.BlockSpec((1, N_MOD, d), lambda i, j, pt: (i, 0, 0)),
            tile(pool_w),
            tile(att.shape[-1]),
            const(w["wout"]), const(w["wuvp"]), const(w["wup"]), const(w["wdown"]),
            const(w["gmlp"]), const(w["gfin"]),
            per_step(rows, kv_lora), per_step(rows, HEAD_PAD), per_step(s_new, kv_lora),
            per_step(ROPE_DIM, s_new),
            pl.BlockSpec(memory_space=pl.ANY),
            pl.BlockSpec(memory_space=pl.ANY),
        ],
        out_specs=[tile(d), per_step(rows, kv_lora)],
        scratch_shapes=[
            pltpu.VMEM((n_slots, t_past + page, kv_lora), F32),
            pltpu.VMEM((n_slots, ROPE_DIM, t_past + page), F32),
            pltpu.VMEM((rows, t_past + page), F32),
            pltpu.SemaphoreType.DMA((2, n_slots)),
        ],
    )
    return pl.pallas_call(
        kern,
        grid_spec=grid_spec,
        out_shape=[jax.ShapeDtypeStruct((b, s, d), F32), jax.ShapeDtypeStruct((db, rows, kv_lora), F32)],
        compiler_params=_params(("arbitrary", "arbitrary"), vmem=VMEM_LIMIT_MERGED),
        name="post_prompt_attn_sample",
    )(page_table, x, mod, pool, att, w["wout"], w["wuvp"], w["wup"], w["wdown"], w["gmlp"], w["gfin"],
      qlat, qpad, lat_new, kr_new, cache_lat, cache_kr)


def _post_kernel(x_ref, mod_ref, pool_ref, att_ref, wout_ref, wuv_ref, wup_ref, wdown_ref,
                 gmlp_ref, gfin_ref, y_ref, *, absorb, final, pool_w, kv_lora, ff_chunk):
    for stage in _post_stages(x_ref, mod_ref, pool_ref, att_ref, wout_ref, wuv_ref, wup_ref, wdown_ref,
                              gmlp_ref, gfin_ref, y_ref, absorb=absorb, final=final, pool_w=pool_w,
                              kv_lora=kv_lora, ff_chunk=ff_chunk):
        stage()


def _post_stages(x_ref, mod_ref, pool_ref, att_ref, wout_ref, wuv_ref, wup_ref, wdown_ref,
                 gmlp_ref, gfin_ref, y_ref, *, absorb, final, pool_w, kv_lora, ff_chunk):
    tb, ts, d = x_ref.shape
    r = tb * ts
    d_ff = wup_ref.shape[1]
    n_chunks = d_ff // ff_chunk
    val = {}

    def project():
        mix = _dot(pool_ref[...].reshape(r, pool_w), wout_ref[0:pool_w, :])
        if absorb:
            for p in range(N_HEADS // 2):
                o_pair = None
                for hh in range(2):
                    hd = 2 * p + hh
                    o_lat = att_ref[:, hd, :, :].reshape(r, kv_lora).astype(BF16)
                    t = _dot(o_lat, wuv_ref[hd])
                    o_pair = t if o_pair is None else o_pair + t
                lo = pool_w + p * 2 * V_DIM
                mix = mix + _dot(o_pair.astype(BF16), wout_ref[lo:lo + 2 * V_DIM, :])
        else:
            mix = mix + _dot(att_ref[...].reshape(r, N_HEADS * V_DIM), wout_ref[pool_w:, :])
        x1 = x_ref[...] + mod_ref[:, 2:3, :] * mix.reshape(tb, ts, d)
        val["x1"] = x1
        val["h2"] = (_rms(x1, gmlp_ref[...]) * (1.0 + mod_ref[:, 4:5, :]) + mod_ref[:, 3:4, :]
                     ).reshape(r, d).astype(BF16)
        val["mlp"] = None

    def mlp_up(c):
        def run():
            hid = jnp.maximum(_dot(val["h2"], wup_ref[:, c * ff_chunk:(c + 1) * ff_chunk]), 0.0)
            val["hid"] = (hid * hid).astype(BF16)
        return run

    def mlp_down(c):
        def run():
            t = _dot(val["hid"], wdown_ref[c * ff_chunk:(c + 1) * ff_chunk, :])
            val["mlp"] = t if val["mlp"] is None else val["mlp"] + t
            if c == n_chunks - 1:
                x2 = val["x1"] + mod_ref[:, 5:6, :] * val["mlp"].reshape(tb, ts, d)
                y_ref[...] = _rms(x2, gfin_ref[...]) if final else x2
        return run

    stages = [project]
    for c in range(n_chunks):
        stages += [mlp_up(c), mlp_down(c)]
    return stages


def _post_sample(x, mod, pool, o_lat, w, *, final, tb, ts):
    b, s, d = x.shape
    pool_w = pool.shape[-1]
    kv_lora = o_lat.shape[-1]
    tile = lambda width: pl.BlockSpec((tb, ts, width), lambda i, j: (i, j, 0))
    const = lambda a: pl.BlockSpec(a.shape, lambda i, j: (0,) * a.ndim, pipeline_mode=pl.Buffered(1))
    kern = functools.partial(_post_kernel, absorb=True, final=final, pool_w=pool_w, kv_lora=kv_lora,
                             ff_chunk=min(1024, w["wup"].shape[1]))
    return pl.pallas_call(
        kern,
        grid=(b // tb, s // ts),
        in_specs=[
            tile(d),
            pl.BlockSpec((tb, N_MOD, d), lambda i, j: (i, 0, 0)),
            tile(pool_w),
            pl.BlockSpec((tb, N_HEADS, ts, kv_lora), lambda i, j: (i, 0, j, 0)),
            const(w["wout"]), const(w["wuvp"]), const(w["wup"]), const(w["wdown"]),
            const(w["gmlp"]), const(w["gfin"]),
        ],
        out_specs=tile(d),
        out_shape=jax.ShapeDtypeStruct((b, s, d), F32),
        compiler_params=_params(("arbitrary", "arbitrary")),
        name="post_sample",
    )(x, mod, pool, o_lat, w["wout"], w["wuvp"], w["wup"], w["wdown"], w["gmlp"], w["gfin"])


def _rot_half(wr):
    half = ROPE_DIM // 2
    return jnp.concatenate([-wr[..., half:], wr[..., :half]], axis=-1)


def _layer_weights(w_in, g_mix, g_q, w_uq, g_kv, w_uk, w_uv, w_pool, pool_scale, w_out, g_mlp,
                   w_up, w_down, g_final):
    d, _ = w_in.shape
    q_lora = g_q.shape[-1]
    kv_lora = g_kv.shape[-1]
    n_groups, gw, _ = w_pool.shape
    pool_w = n_groups * gw
    o3 = pool_w + q_lora + kv_lora
    zpad = lambda rows, cols: jnp.zeros((rows, cols), F32)

    w_kr = w_in[:, o3:]
    win = jnp.concatenate([w_in[:, :o3], w_kr, zpad(d, LANES - ROPE_DIM),
                           _rot_half(w_kr), zpad(d, LANES - ROPE_DIM)], axis=1)

    q_rope = w_uq[:, :, NOPE_DIM:]
    q_nope = w_uq[:, :, :NOPE_DIM]
    hz = jnp.zeros((q_lora, N_HEADS, HEAD_PAD - ROPE_DIM - NOPE_DIM), F32)
    wq_a = jnp.concatenate([q_rope, q_nope, hz], axis=-1).reshape(q_lora, N_HEADS * HEAD_PAD)
    wq_b = _rot_half(q_rope).reshape(q_lora, N_HEADS * ROPE_DIM)
    wq = jnp.concatenate([wq_a, wq_b], axis=1)

    kz_lo = jnp.zeros((kv_lora, N_HEADS, ROPE_DIM), F32)
    kz_hi = jnp.zeros((kv_lora, N_HEADS, HEAD_PAD - ROPE_DIM - NOPE_DIM), F32)
    wk = jnp.concatenate([kz_lo, w_uk, kz_hi], axis=-1).reshape(kv_lora, N_HEADS * HEAD_PAD)
    wkv = jnp.concatenate([wk, w_uv.reshape(kv_lora, N_HEADS * V_DIM)], axis=1)
    wabs = jnp.transpose(jnp.concatenate([kz_lo, w_uk, kz_hi], axis=-1), (1, 2, 0))

    uv = jnp.transpose(w_uv, (1, 0, 2))
    uz = jnp.zeros_like(uv)
    even = (jnp.arange(N_HEADS) % 2 == 0)[:, None, None]
    wuvp = jnp.where(even, jnp.concatenate([uv, uz], axis=-1), jnp.concatenate([uz, uv], axis=-1))

    wpool = jax.scipy.linalg.block_diag(*[w_pool[g] for g in range(n_groups)])

    bf = lambda a: a.astype(BF16)
    return dict(
        win=bf(win), wq=bf(wq), wkv=bf(wkv), wabs=bf(wabs), wuvp=bf(wuvp), wpool=bf(wpool),
        wout=bf(w_out), wup=bf(w_up), wdown=bf(w_down),
        gmix=g_mix.reshape(1, -1), gq=g_q.reshape(1, -1), gkv=g_kv.reshape(1, -1),
        pscale=pool_scale.reshape(1, -1), gmlp=g_mlp.reshape(1, -1), gfin=g_final.reshape(1, -1),
    )


def _rope_tables(positions):
    inv = ROPE_BASE ** (-jnp.arange(0, ROPE_DIM, 2, dtype=F32) / ROPE_DIM)
    ang = positions[:, None] * inv[None, :]
    cos2 = jnp.tile(jnp.cos(ang), (1, 2))
    sin2 = jnp.tile(jnp.sin(ang), (1, 2))
    n = positions.shape[0]
    qs = SOFTMAX_SCALE * LOG2E
    t_qa = jnp.concatenate([cos2, jnp.ones((n, NOPE_DIM), F32),
                            jnp.zeros((n, HEAD_PAD - ROPE_DIM - NOPE_DIM), F32)], axis=1) * qs
    zr = jnp.zeros((n, HEAD_PAD - ROPE_DIM), F32)
    t_qb = jnp.concatenate([sin2, zr], axis=1) * qs
    t_ka = jnp.concatenate([cos2, zr], axis=1)
    t_kb = jnp.concatenate([sin2, zr], axis=1)
    return jnp.concatenate([t_qa, t_qb, t_ka, t_kb], axis=1)


def _seq_tile(s, target):
    t = min(s, target)
    while s % t:
        t //= 2
    return t


def kernel(x_prompt, x_sample, cache_latent, cache_krope, state_pool, page_table, c_prompt, c_sample,
           w_mod, b_mod, g_mix, w_in, g_q, w_uq, g_kv, w_uk, w_uv, w_pool, pool_scale, w_out,
           g_mlp, w_up, w_down, g_final):
    bp, sp, d = x_prompt.shape
    db, ss, _ = x_sample.shape
    depth = w_mod.shape[0]
    n_pages = page_table.shape[1]
    page = cache_latent.shape[2]
    past = n_pages * page
    pool_w = state_pool.shape[-1]
    kv_lora = g_kv.shape[-1]

    tab_p = _rope_tables(jnp.arange(sp, dtype=F32))
    tab_s = _rope_tables(jnp.arange(ss, dtype=F32) + float(past))
    ts_p = _seq_tile(sp, 512)
    tb_s = _seq_tile(db, max(1, 512 // ss))

    xp, xs = x_prompt, x_sample
    outs = [[] for _ in range(6)]
    for l in range(depth):
        w = _layer_weights(w_in[l], g_mix[l], g_q[l], w_uq[l], g_kv[l], w_uk[l], w_uv[l], w_pool[l],
                           pool_scale[l], w_out[l], g_mlp[l], w_up[l], w_down[l], g_final)
        mod = _modulation(jnp.concatenate([c_prompt, c_sample], axis=0), w_mod[l], b_mod[l])
        mod = mod.reshape(bp + db, N_MOD, d)
        mod_p, mod_s = mod[:bp], mod[bp:]

        hist_p = jnp.zeros((bp, HIST_PAD, pool_w), F32)
        q, kcat, v, lat_p, kr_p, pool_p, nh_p = _pre(
            xp, mod_p, hist_p, tab_p, w, pos0=0.0, absorb=False, tb=1, ts=_seq_tile(sp, 1024))
        att_p = _attention(q, kcat, v, tq=_seq_tile(sp, 512), heads=4)
        final = l == depth - 1

        hist_s = jnp.pad(state_pool[l], ((0, 0), (HIST_PAD - POOL_HIST, 0), (0, 0)))
        qs, qlat, lat_s, kr_s, pool_s, nh_s = _pre(
            xs, mod_s, hist_s, tab_s, w, pos0=float(past), absorb=True, tb=tb_s, ts=ss)
        rows = ss * N_HEADS
        xp, o_lat = _post_prompt_with_decode(
            xp, mod_p, pool_p, att_p, w, page_table,
            qlat.reshape(db, rows, kv_lora),
            qs.reshape(db, rows, HEAD_PAD),
            lat_s, jnp.swapaxes(kr_s, 1, 2),
            cache_latent[l], jnp.swapaxes(cache_krope[l], 1, 2), final=final, ts=ts_p)
        xs = _post_sample(xs, mod_s, pool_s, o_lat.reshape(db, N_HEADS, ss, kv_lora), w,
                          final=final, tb=tb_s, ts=ss)

        for lst, val in zip(outs, (lat_p, jnp.swapaxes(kr_p, 1, 2), nh_p[:, HIST_PAD - POOL_HIST:],
                                   lat_s, kr_s, nh_s[:, HIST_PAD - POOL_HIST:])):
            lst.append(val)
    return (xp, xs) + tuple(jnp.stack(o) for o in outs)
```
